```python
import jax, jax.numpy as jnp
from jax import lax
import numpy as np

D_MODEL = 1024
BATCH = 8
SEQ = 2048
DEPTH = 1
DEC_BATCH = 8
DEC_SEQ = 16
PAST_LEN = 1024

CHUNK = 64
MIX_WIDTH = D_MODEL
HEAD_DIM = 64
ATTN_WIDTH = MIX_WIDTH // 2
ATTN_HEADS = ATTN_WIDTH // HEAD_DIM
CONV_DIM = MIX_WIDTH - ATTN_WIDTH
CONV_GROUPS = CONV_DIM // HEAD_DIM
CONV_K = 3
PROJ_COLS = 3 * ATTN_WIDTH + ATTN_HEADS + 3 * CONV_DIM
Q_BLOCK = 128
N_GROUPS = 4
EXPERTS_PER_GROUP = 8
TOP_K_IN_GROUP = 2
D_EXPERT = 256
RMS_EPS = 1e-6
FORGET_BIAS_INIT = 3.0

kernel_name = "hybrid_fox_shortconv_hiermoe_stream_step"


def rmsnorm(x, g):
    xf = x.astype(jnp.float32)
    y = xf * lax.rsqrt(jnp.mean(xf * xf, axis=-1, keepdims=True) + RMS_EPS)
    return (y * g.astype(jnp.float32)).astype(x.dtype)


def project(xn, w_in, b_forget):
    bsz, t = xn.shape[0], xn.shape[1]
    p = xn @ w_in
    A, H, C = ATTN_WIDTH, ATTN_HEADS, CONV_DIM
    q, k, v, f, bg, cg, h = jnp.split(p, [A, 2 * A, 3 * A, 3 * A + H, 3 * A + H + C, 3 * A + H + 2 * C], axis=-1)
    q = q.reshape(bsz, t, H, HEAD_DIM)
    k = k.reshape(bsz, t, H, HEAD_DIM)
    v = v.reshape(bsz, t, H, HEAD_DIM)
    lf = jax.nn.log_sigmoid(f.astype(jnp.float32) + b_forget.astype(jnp.float32))
    return q, k, v, lf, bg, cg, h


def fox_prompt(q, k, v, lf):
    bsz, s_len = q.shape[0], q.shape[1]
    nb = s_len // Q_BLOCK
    scale = HEAD_DIM ** -0.5
    cT = jnp.cumsum(lf, axis=1).transpose(0, 2, 1)
    kf = k.astype(jnp.float32)
    qb_all = q.astype(jnp.float32).reshape(bsz, nb, Q_BLOCK, ATTN_HEADS, HEAD_DIM).transpose(1, 0, 2, 3, 4)
    cq_all = cT.reshape(bsz, ATTN_HEADS, nb, Q_BLOCK).transpose(2, 0, 1, 3)
    kpos = jnp.arange(s_len)

    def block(args):
        qb, cqb, i = args
        s = jnp.einsum('bqhd,bkhd->bhqk', qb, kf) * scale + (cqb[..., None] - cT[:, :, None, :])
        qpos = i * Q_BLOCK + jnp.arange(Q_BLOCK)
        s = jnp.where(kpos[None, :] <= qpos[:, None], s, -jnp.inf)
        p = jax.nn.softmax(s, axis=-1)
        return jnp.einsum('bhqk,bkhd->bqhd', p.astype(v.dtype), v)

    o = lax.map(block, (qb_all, cq_all, jnp.arange(nb)))
    return o.transpose(1, 0, 2, 3, 4).reshape(bsz, s_len, ATTN_WIDTH)


def fox_sample(q, k, v, lf, cache_k, cache_v, cache_lf):
    bsz, t = q.shape[0], q.shape[1]
    p_len = cache_k.shape[1]
    scale = HEAD_DIM ** -0.5
    lf_p = cache_lf.astype(jnp.float32)
    c_past = jnp.cumsum(lf_p, axis=1) - jnp.sum(lf_p, axis=1, keepdims=True)
    c_new = jnp.cumsum(lf, axis=1)
    ck = jnp.concatenate([c_past, c_new], axis=1).transpose(0, 2, 1)
    keys = jnp.concatenate([cache_k.astype(k.dtype), k], axis=1)
    vals = jnp.concatenate([cache_v.astype(v.dtype), v], axis=1)
    s = jnp.einsum('bqhd,bkhd->bhqk', q.astype(jnp.float32), keys.astype(jnp.float32)) * scale
    s = s + (c_new.transpose(0, 2, 1)[..., None] - ck[:, :, None, :])
    mask = jnp.arange(p_len + t)[None, :] <= (p_len + jnp.arange(t))[:, None]
    s = jnp.where(mask, s, -jnp.inf)
    p = jax.nn.softmax(s, axis=-1)
    o = jnp.einsum('bhqk,bkhd->bqhd', p.astype(vals.dtype), vals)
    return o.reshape(bsz, t, ATTN_WIDTH)


def short_conv(bg, u, conv_state, conv_w):
    t = u.shape[1]
    up = jnp.concatenate([conv_state.astype(u.dtype), u], axis=1)
    y = conv_w[0] * up[:, 0:t]
    for i in range(1, CONV_K):
        y = y + conv_w[i] * up[:, i:i + t]
    return bg * y, up[:, -(CONV_K - 1):]


def hier_moe(xn, w_rg, b_rg, w_re, b_re, w_gu, w_dn):
    lead = xn.shape[:-1]
    t = xn.reshape(-1, D_MODEL)
    pg = jax.nn.softmax((t @ w_rg).astype(jnp.float32) + b_rg.astype(jnp.float32), axis=-1)
    pg_top, g_top = lax.top_k(pg, 1)
    le = jnp.einsum('nd,gde->nge', t, w_re).astype(jnp.float32) + b_re.astype(jnp.float32)
    le_sel = jnp.take_along_axis(le, g_top[:, :, None], axis=1)[:, 0]
    pe_top, e_top = lax.top_k(jax.nn.softmax(le_sel, axis=-1), TOP_K_IN_GROUP)
    w_top = pe_top / jnp.sum(pe_top, axis=-1, keepdims=True) * pg_top
    w_group = jnp.sum(jax.nn.one_hot(e_top, EXPERTS_PER_GROUP, dtype=jnp.float32) * w_top[..., None], axis=1)
    comb = jax.nn.one_hot(g_top[:, 0], N_GROUPS, dtype=jnp.float32)[:, :, None] * w_group[:, None, :]
    out = jnp.zeros_like(t)
    for gi in range(N_GROUPS):
        gate, up = jnp.split(jnp.einsum('nd,edf->nef', t, w_gu[gi]), 2, axis=-1)
        hmid = jax.nn.silu(gate) * up * comb[:, gi, :, None].astype(t.dtype)
        out = out + jnp.einsum('nef,efd->nd', hmid, w_dn[gi])
    return out.reshape(*lead, D_MODEL)


def trunk_layer(x, attend, conv_state, norm_mix_g, w_in, b_forget, conv_w, norm_attn_g, norm_conv_g,
                w_out, norm_ffn_g, w_router_group, b_router_group, w_router_expert, b_router_expert,
                w_expert_gate_up, w_expert_down):
    xn = rmsnorm(x, norm_mix_g)
    q, k, v, lf, bg, cg, h = project(xn, w_in, b_forget)
    o_a = attend(q, k, v, lf)
    o_b, new_conv = short_conv(bg, cg * h, conv_state, conv_w)
    merged = jnp.concatenate([rmsnorm(o_a, norm_attn_g), rmsnorm(o_b, norm_conv_g)], axis=-1)
    x = x + merged @ w_out
    x = x + hier_moe(rmsnorm(x, norm_ffn_g), w_router_group, b_router_group, w_router_expert,
                     b_router_expert, w_expert_gate_up, w_expert_down)
    return x, k, v, lf, new_conv


def setup_inputs(seed: int = 0) -> dict:
    key = jax.random.key(seed)
    ks = jax.random.split(key, 20)
    nrm = jax.random.normal
    f32 = jnp.float32
    return {
        "x_prompt": nrm(ks[0], (BATCH, SEQ, D_MODEL), f32),
        "x_sample": nrm(ks[1], (DEC_BATCH, DEC_SEQ, D_MODEL), f32),
        "cache_k": nrm(ks[2], (DEPTH, DEC_BATCH, PAST_LEN, ATTN_HEADS, HEAD_DIM), f32),
        "cache_v": nrm(ks[3], (DEPTH, DEC_BATCH, PAST_LEN, ATTN_HEADS, HEAD_DIM), f32),
        "cache_logf": jax.nn.log_sigmoid(FORGET_BIAS_INIT + nrm(ks[4], (DEPTH, DEC_BATCH, PAST_LEN, ATTN_HEADS), f32)),
        "state_conv": nrm(ks[5], (DEPTH, DEC_BATCH, CONV_K - 1, CONV_DIM), f32),
        "norm_mix_g": 1.0 + 0.02 * nrm(ks[6], (DEPTH, D_MODEL), f32),
        "w_in": nrm(ks[7], (DEPTH, D_MODEL, PROJ_COLS), f32) * D_MODEL ** -0.5,
        "b_forget": FORGET_BIAS_INIT + 0.5 * nrm(ks[8], (DEPTH, ATTN_HEADS), f32),
        "conv_w": nrm(ks[9], (DEPTH, CONV_K, CONV_DIM), f32) * CONV_K ** -0.5,
        "norm_attn_g": 1.0 + 0.02 * nrm(ks[10], (DEPTH, ATTN_WIDTH), f32),
        "norm_conv_g": 1.0 + 0.02 * nrm(ks[11], (DEPTH, CONV_DIM), f32),
        "w_out": nrm(ks[12], (DEPTH, MIX_WIDTH, D_MODEL), f32) * MIX_WIDTH ** -0.5,
        "norm_ffn_g": 1.0 + 0.02 * nrm(ks[13], (DEPTH, D_MODEL), f32),
        "w_router_group": nrm(ks[14], (DEPTH, D_MODEL, N_GROUPS), f32) * D_MODEL ** -0.5,
        "b_router_group": 0.01 * nrm(ks[15], (DEPTH, N_GROUPS), f32),
        "w_router_expert": nrm(ks[16], (DEPTH, N_GROUPS, D_MODEL, EXPERTS_PER_GROUP), f32) * D_MODEL ** -0.5,
        "b_router_expert": 0.01 * nrm(ks[17], (DEPTH, N_GROUPS, EXPERTS_PER_GROUP), f32),
        "w_expert_gate_up": nrm(ks[18], (DEPTH, N_GROUPS, EXPERTS_PER_GROUP, D_MODEL, 2 * D_EXPERT), f32) * D_MODEL ** -0.5,
        "w_expert_down": nrm(ks[19], (DEPTH, N_GROUPS, EXPERTS_PER_GROUP, D_EXPERT, D_MODEL), f32) * D_EXPERT ** -0.5,
        "norm_final_g": 1.0 + 0.02 * nrm(jax.random.fold_in(key, 99), (D_MODEL,), f32),
    }


def reference(x_prompt, x_sample, cache_k, cache_v, cache_logf, state_conv, norm_mix_g, w_in, b_forget,
              conv_w, norm_attn_g, norm_conv_g, w_out, norm_ffn_g, w_router_group, b_router_group,
              w_router_expert, b_router_expert, w_expert_gate_up, w_expert_down, norm_final_g):
    yp, ys = x_prompt, x_sample
    kp_l, vp_l, lfp_l, cvp_l = [], [], [], []
    ks_l, vs_l, lfs_l, cvs_l = [], [], [], []
    for l in range(DEPTH):
        w = (norm_mix_g[l], w_in[l], b_forget[l], conv_w[l], norm_attn_g[l], norm_conv_g[l], w_out[l],
             norm_ffn_g[l], w_router_group[l], b_router_group[l], w_router_expert[l], b_router_expert[l],
             w_expert_gate_up[l], w_expert_down[l])
        zero_conv = jnp.zeros((yp.shape[0], CONV_K - 1, CONV_DIM), yp.dtype)
        yp, kp, vp, lfp, cvp = trunk_layer(yp, fox_prompt, zero_conv, *w)
        ck, cv, clf = cache_k[l], cache_v[l], cache_logf[l]
        attend_s = lambda q, k, v, lf, ck=ck, cv=cv, clf=clf: fox_sample(q, k, v, lf, ck, cv, clf)
        ys, kn, vn, lfn, cvn = trunk_layer(ys, attend_s, state_conv[l], *w)
        kp_l.append(kp); vp_l.append(vp); lfp_l.append(lfp); cvp_l.append(cvp)
        ks_l.append(kn); vs_l.append(vn); lfs_l.append(lfn); cvs_l.append(cvn)
    y_prompt = rmsnorm(yp, norm_final_g)
    y_sample = rmsnorm(ys, norm_final_g)
    new_k_prompt = jnp.stack(kp_l, axis=0)
    new_v_prompt = jnp.stack(vp_l, axis=0)
    new_logf_prompt = jnp.stack(lfp_l, axis=0)
    new_conv_prompt = jnp.stack(cvp_l, axis=0)
    new_k_sample = jnp.stack(ks_l, axis=0)
    new_v_sample = jnp.stack(vs_l, axis=0)
    new_logf_sample = jnp.stack(lfs_l, axis=0)
    new_conv_sample = jnp.stack(cvs_l, axis=0)
    return (y_prompt, y_sample, new_k_prompt, new_v_prompt, new_logf_prompt, new_conv_prompt,
            new_k_sample, new_v_sample, new_logf_sample, new_conv_sample)
```

```python
import functools
import math

import jax
import jax.numpy as jnp
from jax import lax
from jax.experimental import pallas as pl
from jax.experimental.pallas import tpu as pltpu

F32 = jnp.float32
BF16 = jnp.bfloat16

D_MODEL = 1024
HEADS = 8
HEAD_DIM = 64
ATTN_W = HEADS * HEAD_DIM
CONV_W = 512
N_GROUPS = 4
EPG = 8
D_EXPERT = 256
EPS = 1e-6
LOG2E = math.log2(math.e)
Q_SCALE = HEAD_DIM ** -0.5 * LOG2E

LANES = 128
COL_Q, COL_K, COL_V, COL_F, COL_BG, COL_CG, COL_H = 0, 512, 1024, 1536, 1664, 2176, 2688
PROJ_PAD = 3200
F_OUT, F_HI, F_MID, F_LO = 0, 64, 72, 80

VMEM_LIMIT = 56 * 1024 * 1024


def _dot(a, b):
    return jnp.dot(a, b, preferred_element_type=F32)


def _dot_nt(a, b):
    return lax.dot_general(a, b, (((1,), (1,)), ((), ())), preferred_element_type=F32)


def _rms(x, g):
    return x * lax.rsqrt(jnp.mean(x * x, axis=-1, keepdims=True) + EPS) * g


def _log_sigmoid(x):
    return jnp.minimum(x, 0.0) - jnp.log(1.0 + jnp.exp(-jnp.abs(x)))


def _split3(x):
    a1 = x.astype(BF16)
    r1 = x - a1.astype(F32)
    a2 = r1.astype(BF16)
    a3 = (r1 - a2.astype(F32)).astype(BF16)
    return a1, a2, a3


def _cumsum_rows(x):
    n = x.shape[0]
    r = lax.broadcasted_iota(jnp.int32, (n, n), 0)
    c = lax.broadcasted_iota(jnp.int32, (n, n), 1)
    tri = jnp.where(r >= c, 1.0, 0.0).astype(BF16)
    parts = _dot(tri, jnp.concatenate(_split3(x), axis=1))
    return parts[:, :LANES] + parts[:, LANES:2 * LANES] + parts[:, 2 * LANES:]


def _key_bias_lanes(cneg):
    lane = lax.broadcasted_iota(jnp.int32, cneg.shape, 1)
    hi = cneg.astype(BF16).astype(F32)
    r1 = cneg - hi
    mid = r1.astype(BF16).astype(F32)
    lo = r1 - mid
    out = jnp.where(lane < F_MID, hi, jnp.where(lane < F_LO, mid, lo))
    return jnp.where((lane >= F_HI) & (lane < F_LO + HEADS), out, 0.0)


def _head_block(x, h):
    blk = x[:, (h // 2) * LANES:(h // 2 + 1) * LANES]
    return pltpu.roll(blk, HEAD_DIM, 1) if h % 2 else blk


def _key_operand(k, bias_lanes, h):
    lane = lax.broadcasted_iota(jnp.int32, bias_lanes.shape, 1)
    return jnp.where(lane < HEAD_DIM, _head_block(k, h), bias_lanes).astype(BF16)


def _proj_kernel(x_ref, st_ref, gm_ref, w_ref, bf_ref, cw_ref, gc_ref,
                 k_ref, v_ref, lf_ref, qa_ref, ka_ref, mb_ref, ncv_ref, ubuf, ccar):
    i = pl.program_id(1)
    tm = x_ref.shape[1]
    xn = _rms(x_ref[0], gm_ref[...]).astype(BF16)
    p = _dot(xn, w_ref[...])
    q = p[:, COL_Q:COL_Q + ATTN_W]
    k = p[:, COL_K:COL_K + ATTN_W]
    k_ref[0] = k
    v_ref[0] = p[:, COL_V:COL_V + ATTN_W]

    lfb = _log_sigmoid(p[:, COL_F:COL_F + LANES] + bf_ref[...])
    lf_ref[0] = lfb[:, F_OUT:F_OUT + HEADS]

    @pl.when(i == 0)
    def _():
        ccar[...] = jnp.zeros_like(ccar)

    c = _cumsum_rows(lfb) + ccar[0:1, :]
    ccar[...] = jnp.broadcast_to(c[tm - 1:tm, :], ccar.shape)
    bias_lanes = _key_bias_lanes(c * (-LOG2E))
    lane = lax.broadcasted_iota(jnp.int32, (tm, LANES), 1)
    for h in range(HEADS):
        ka_ref[0, h] = _key_operand(k, bias_lanes, h)
        sel = jnp.where((lane == F_HI + h) | (lane == F_MID + h) | (lane == F_LO + h), 1.0, 0.0)
        qa_ref[0, h] = jnp.where(lane < HEAD_DIM, _head_block(q, h) * Q_SCALE, sel).astype(BF16)

    u = p[:, COL_CG:COL_CG + CONV_W] * p[:, COL_H:COL_H + CONV_W]

    @pl.when(i == 0)
    def _():
        ubuf[0:8, :] = st_ref[0]

    @pl.when(i > 0)
    def _():
        ubuf[0:8, :] = ubuf[tm:tm + 8, :]

    ubuf[8:8 + tm, :] = u
    y = cw_ref[0:1, :] * ubuf[6:6 + tm, :] + cw_ref[1:2, :] * ubuf[7:7 + tm, :] + cw_ref[2:3, :] * u
    ob = p[:, COL_BG:COL_BG + CONV_W] * y
    mb_ref[0] = _rms(ob, gc_ref[...]).astype(BF16)
    ncv_ref[0] = u[tm - 2:tm, :]


def _proj(x, state, gm, w_all, bias_f, cw, gc, tm):
    b, s, _ = x.shape
    full = lambda shape: pl.BlockSpec(shape, lambda bi, i: (0,) * len(shape))
    return pl.pallas_call(
        _proj_kernel,
        grid=(b, s // tm),
        in_specs=[
            pl.BlockSpec((1, tm, D_MODEL), lambda bi, i: (bi, i, 0)),
            pl.BlockSpec((1, 8, CONV_W), lambda bi, i: (bi, 0, 0)),
            full((1, D_MODEL)),
            full((D_MODEL, PROJ_PAD)),
            full((1, LANES)),
            full((8, CONV_W)),
            full((1, CONV_W)),
        ],
        out_specs=[
            pl.BlockSpec((1, tm, ATTN_W), lambda bi, i: (bi, i, 0)),
            pl.BlockSpec((1, tm, ATTN_W), lambda bi, i: (bi, i, 0)),
            pl.BlockSpec((1, tm, HEADS), lambda bi, i: (bi, i, 0)),
            pl.BlockSpec((1, HEADS, tm, LANES), lambda bi, i: (bi, 0, i, 0)),
            pl.BlockSpec((1, HEADS, tm, LANES), lambda bi, i: (bi, 0, i, 0)),
            pl.BlockSpec((1, tm, CONV_W), lambda bi, i: (bi, i, 0)),
            pl.BlockSpec((1, 2, CONV_W), lambda bi, i: (bi, 0, 0)),
        ],
        out_shape=[
            jax.ShapeDtypeStruct((b, s, ATTN_W), F32),
            jax.ShapeDtypeStruct((b, s, ATTN_W), F32),
            jax.ShapeDtypeStruct((b, s, HEADS), F32),
            jax.ShapeDtypeStruct((b, HEADS, s, LANES), BF16),
            jax.ShapeDtypeStruct((b, HEADS, s, LANES), BF16),
            jax.ShapeDtypeStruct((b, s, CONV_W), BF16),
            jax.ShapeDtypeStruct((b, 2, CONV_W), F32),
        ],
        scratch_shapes=[pltpu.VMEM((tm + 8, CONV_W), F32), pltpu.VMEM((8, LANES), F32)],
        compiler_params=pltpu.CompilerParams(
            dimension_semantics=("arbitrary", "arbitrary"), vmem_limit_bytes=VMEM_LIMIT),
        name="proj",
    )(x, state, gm, w_all, bias_f, cw, gc)


ATT_BLK = 256


def _attn_kernel(qa_ref, ka_ref, v_ref, o_ref, vt):
    iq = pl.program_id(2)
    blk = ATT_BLK

    @pl.when(iq == 0)
    def _():
        vt[...] = v_ref[0].T.astype(BF16)

    r = lax.broadcasted_iota(jnp.int32, (blk, blk), 0)
    c = lax.broadcasted_iota(jnp.int32, (blk, blk), 1)
    causal = r <= c

    outs = []
    for hh in range(2):
        q = qa_ref[0, hh]

        def step(j, carry, masked):
            m, l, acc = carry
            start = pl.multiple_of(j * blk, blk)
            s = _dot_nt(ka_ref[0, hh, pl.ds(start, blk), :], q)
            if masked:
                s = jnp.where(causal, s, -jnp.inf)
            m_new = jnp.maximum(m, jnp.max(s, axis=0, keepdims=True))
            alpha = jnp.exp2(m - m_new)
            pr = jnp.exp2(s - m_new)
            l = alpha * l + jnp.sum(pr, axis=0, keepdims=True)
            vblk = vt[hh * HEAD_DIM:(hh + 1) * HEAD_DIM, pl.ds(start, blk)]
            acc = alpha * acc + _dot(vblk, pr.astype(BF16))
            return m_new, l, acc

        init = (jnp.full((1, blk), -jnp.inf, F32), jnp.zeros((1, blk), F32), jnp.zeros((HEAD_DIM, blk), F32))
        carry = lax.fori_loop(0, iq, lambda j, cr: step(j, cr, False), init)
        m, l, acc = step(iq, carry, True)
        outs.append(acc / l)
    o_ref[0] = jnp.concatenate(outs, axis=0).T


def _attention(qa, ka, v):
    b, _, s, _ = qa.shape
    return pl.pallas_call(
        _attn_kernel,
        grid=(b, HEADS // 2, s // ATT_BLK),
        in_specs=[
            pl.BlockSpec((1, 2, ATT_BLK, LANES), lambda bi, p, iq: (bi, p, iq, 0)),
            pl.BlockSpec((1, 2, s, LANES), lambda bi, p, iq: (bi, p, 0, 0)),
            pl.BlockSpec((1, s, LANES), lambda bi, p, iq: (bi, 0, p)),
        ],
        out_specs=pl.BlockSpec((1, ATT_BLK, LANES), lambda bi, p, iq: (bi, iq, p)),
        out_shape=jax.ShapeDtypeStruct((b, s, ATTN_W), F32),
        scratch_shapes=[pltpu.VMEM((LANES, s), BF16)],
        compiler_params=pltpu.CompilerParams(
            dimension_semantics=("arbitrary", "arbitrary", "arbitrary"), vmem_limit_bytes=VMEM_LIMIT),
        name="attention",
    )(qa, ka, v)


Q_PAD = 128


def _pad_rows(x, n):
    return jnp.concatenate([x, jnp.zeros((n - x.shape[0], x.shape[1]), x.dtype)], axis=0)


def _attn_cache_kernel(qa_ref, ka_ref, v_ref, ck_ref, cv_ref, clf_ref, o_ref):
    t = qa_ref.shape[2]
    plen = ck_ref.shape[1]
    cpast = _cumsum_rows(clf_ref[0])
    cpast = cpast - cpast[plen - 1:plen, :]
    bias_lanes = _key_bias_lanes(cpast * (-LOG2E))
    ck = ck_ref[0]
    r = lax.broadcasted_iota(jnp.int32, (Q_PAD, Q_PAD), 0)
    c = lax.broadcasted_iota(jnp.int32, (Q_PAD, Q_PAD), 1)
    new_ok = (r <= c) & (r < t)

    pairs = []
    for p in range(HEADS // 2):
        vct = cv_ref[0][:, p * LANES:(p + 1) * LANES].T.astype(BF16)
        vnt = _pad_rows(v_ref[0][:, p * LANES:(p + 1) * LANES], Q_PAD).T.astype(BF16)
        outs = []
        for hh in range(2):
            h = 2 * p + hh
            q = _pad_rows(qa_ref[0, h], Q_PAD)
            s_past = _dot_nt(_key_operand(ck, bias_lanes, h), q)
            s_new = jnp.where(new_ok, _dot_nt(_pad_rows(ka_ref[0, h], Q_PAD), q), -jnp.inf)
            m = jnp.maximum(jnp.max(s_past, axis=0, keepdims=True), jnp.max(s_new, axis=0, keepdims=True))
            p_past = jnp.exp2(s_past - m)
            p_new = jnp.exp2(s_new - m)
            l = jnp.sum(p_past, axis=0, keepdims=True) + jnp.sum(p_new, axis=0, keepdims=True)
            rows = slice(hh * HEAD_DIM, (hh + 1) * HEAD_DIM)
            acc = _dot(vct[rows], p_past.astype(BF16)) + _dot(vnt[rows], p_new.astype(BF16))
            outs.append(acc / l)
        pairs.append(jnp.concatenate(outs, axis=0).T[0:t, :])
    o_ref[0] = jnp.concatenate(pairs, axis=1)


def _attention_cache(qa, ka, v, ck, cv, clf):
    b, _, t, _ = qa.shape
    plen = ck.shape[1]
    return pl.pallas_call(
        _attn_cache_kernel,
        grid=(b,),
        in_specs=[
            pl.BlockSpec((1, HEADS, t, LANES), lambda bi: (bi, 0, 0, 0)),
            pl.BlockSpec((1, HEADS, t, LANES), lambda bi: (bi, 0, 0, 0)),
            pl.BlockSpec((1, t, ATTN_W), lambda bi: (bi, 0, 0)),
            pl.BlockSpec((1, plen, ATTN_W), lambda bi: (bi, 0, 0)),
            pl.BlockSpec((1, plen, ATTN_W), lambda bi: (bi, 0, 0)),
            pl.BlockSpec((1, plen, LANES), lambda bi: (bi, 0, 0)),
        ],
        out_specs=pl.BlockSpec((1, t, ATTN_W), lambda bi: (bi, 0, 0)),
        out_shape=jax.ShapeDtypeStruct((b, t, ATTN_W), F32),
        compiler_params=pltpu.CompilerParams(
            dimension_semantics=("arbitrary",), vmem_limit_bytes=VMEM_LIMIT),
        name="attention_cache",
    )(qa, ka, v, ck, cv, clf)


def _mix_kernel(x_ref, o_ref, mb_ref, wo_ref, ga_ref, gf_ref, wr_ref, br_ref, x2_ref, xn_ref, comb_ref):
    tm = x_ref.shape[0]
    oa = _rms(o_ref[...], ga_ref[...]).astype(BF16)
    merged = jnp.concatenate([oa, mb_ref[...]], axis=1)
    x2 = x_ref[...] + _dot(merged, wo_ref[...])
    x2_ref[...] = x2
    xb = _rms(x2, gf_ref[...]).astype(BF16)
    xn_ref[...] = xb

    lt = (_dot(xb, wr_ref[...]) + br_ref[...]).T
    row = lax.broadcasted_iota(jnp.int32, (EPG, tm), 0)
    lg = jnp.where(row < N_GROUPS, lt[0:EPG], -jnp.inf)
    eg = jnp.exp(lg - jnp.max(lg, axis=0, keepdims=True))
    pg = eg / jnp.sum(eg, axis=0, keepdims=True)
    pg_top = jnp.max(pg, axis=0, keepdims=True)
    g_top = jnp.min(jnp.where(pg == pg_top, row, EPG), axis=0, keepdims=True)

    le = lt[EPG:2 * EPG]
    for g in range(1, N_GROUPS):
        le = jnp.where(g_top == g, lt[EPG * (g + 1):EPG * (g + 2)], le)
    ee = jnp.exp(le - jnp.max(le, axis=0, keepdims=True))
    pe = ee / jnp.sum(ee, axis=0, keepdims=True)
    v1 = jnp.max(pe, axis=0, keepdims=True)
    i1 = jnp.min(jnp.where(pe == v1, row, EPG), axis=0, keepdims=True)
    pe2 = jnp.where(row == i1, -1.0, pe)
    v2 = jnp.max(pe2, axis=0, keepdims=True)
    i2 = jnp.min(jnp.where(pe2 == v2, row, EPG), axis=0, keepdims=True)
    den = v1 + v2
    wg = jnp.where(row == i1, v1 / den * pg_top, 0.0) + jnp.where(row == i2, v2 / den * pg_top, 0.0)
    slabs = [jnp.where(g_top == g, wg, 0.0) for g in range(N_GROUPS)]
    slabs.append(jnp.zeros((LANES - N_GROUPS * EPG, tm), F32))
    comb_ref[...] = jnp.concatenate(slabs, axis=0).T


def _mix(x, o, mb, wo, ga, gf, wr, br, tm):
    n = x.shape[0]
    full = lambda shape: pl.BlockSpec(shape, lambda i: (0,) * len(shape))
    return pl.pallas_call(
        _mix_kernel,
        grid=(n // tm,),
        in_specs=[
            pl.BlockSpec((tm, D_MODEL), lambda i: (i, 0)),
            pl.BlockSpec((tm, ATTN_W), lambda i: (i, 0)),
            pl.BlockSpec((tm, CONV_W), lambda i: (i, 0)),
            full((D_MODEL, D_MODEL)),
            full((1, ATTN_W)),
            full((1, D_MODEL)),
            full((D_MODEL, LANES)),
            full((1, LANES)),
        ],
        out_specs=[
            pl.BlockSpec((tm, D_MODEL), lambda i: (i, 0)),
            pl.BlockSpec((tm, D_MODEL), lambda i: (i, 0)),
            pl.BlockSpec((tm, LANES), lambda i: (i, 0)),
        ],
        out_shape=[
            jax.ShapeDtypeStruct((n, D_MODEL), F32),
            jax.ShapeDtypeStruct((n, D_MODEL), BF16),
            jax.ShapeDtypeStruct((n, LANES), F32),
        ],
        compiler_params=pltpu.CompilerParams(
            dimension_semantics=("arbitrary",), vmem_limit_bytes=VMEM_LIMIT),
        name="mix",
    )(x, o, mb, wo, ga, gf, wr, br)


def _experts_kernel(x2_ref, xn_ref, comb_ref, wgu_ref, wdn_ref, gn_ref, y_ref, acc):
    g = pl.program_id(1)

    @pl.when(g == 0)
    def _():
        acc[...] = jnp.zeros_like(acc)

    xb = xn_ref[...]
    comb = comb_ref[...]
    tot = acc[...]
    for e in range(EPG):
        w = comb[:, e:e + 1]
        for gg in range(1, N_GROUPS):
            w = jnp.where(g == gg, comb[:, gg * EPG + e:gg * EPG + e + 1], w)
        gu = _dot(xb, wgu_ref[0, e])
        gate = gu[:, :D_EXPERT]
        hmid = gate * jax.nn.sigmoid(gate) * gu[:, D_EXPERT:] * w
        tot = tot + _dot(hmid.astype(BF16), wdn_ref[0, e])
    acc[...] = tot

    @pl.when(g == N_GROUPS - 1)
    def _():
        y_ref[...] = _rms(x2_ref[...] + tot, gn_ref[...])


def _experts(x2, xn, comb, wgu, wdn, gn, tm):
    n = x2.shape[0]
    return pl.pallas_call(
        _experts_kernel,
        grid=(n // tm, N_GROUPS),
        in_specs=[
            pl.BlockSpec((tm, D_MODEL), lambda i, g: (i, 0)),
            pl.BlockSpec((tm, D_MODEL), lambda i, g: (i, 0)),
            pl.BlockSpec((tm, LANES), lambda i, g: (i, 0)),
            pl.BlockSpec((1, EPG, D_MODEL, 2 * D_EXPERT), lambda i, g: (g, 0, 0, 0)),
            pl.BlockSpec((1, EPG, D_EXPERT, D_MODEL), lambda i, g: (g, 0, 0, 0)),
            pl.BlockSpec((1, D_MODEL), lambda i, g: (0, 0)),
        ],
        out_specs=pl.BlockSpec((tm, D_MODEL), lambda i, g: (i, 0)),
        out_shape=jax.ShapeDtypeStruct((n, D_MODEL), F32),
        scratch_shapes=[pltpu.VMEM((tm, D_MODEL), F32)],
        compiler_params=pltpu.CompilerParams(
            dimension_semantics=("arbitrary", "arbitrary"), vmem_limit_bytes=VMEM_LIMIT),
        name="experts",
    )(x2, xn, comb, wgu, wdn, gn)


def _forget_lanes(f):
    z = lambda n: jnp.zeros(f.shape[:-1] + (n,), f.dtype)
    return jnp.concatenate([f, z(F_HI - HEADS), f, f, f, z(LANES - F_LO - HEADS)], axis=-1)


def _trunk(x, attend, state, w, tm_proj, tm_tok):
    b, s, _ = x.shape
    k, v, lf, qa, ka, mb, ncv = _proj(x, state, w["gm"], w["w_all"], w["bias_f"], w["cw"], w["gc"], tm_proj)
    o = attend(qa, ka, v)
    n = b * s
    x2, xn, comb = _mix(x.reshape(n, D_MODEL), o.reshape(n, ATTN_W), mb.reshape(n, CONV_W),
                        w["wo"], w["ga"], w["gf"], w["wr"], w["br"], tm_tok)
    y = _experts(x2, xn, comb, w["wgu"], w["wdn"], w["gn"], tm_tok)
    return (y.reshape(b, s, D_MODEL), k.reshape(1, b, s, HEADS, HEAD_DIM), v.reshape(1, b, s, HEADS, HEAD_DIM),
            lf.reshape(1, b, s, HEADS), ncv.reshape(1, b, 2, CONV_W))


def kernel(x_prompt, x_sample, cache_k, cache_v, cache_logf, state_conv, norm_mix_g, w_in, b_forget, conv_w, norm_attn_g, norm_conv_g, w_out, norm_ffn_g, w_router_group, b_router_group, w_router_expert, b_router_expert, w_expert_gate_up, w_expert_down, norm_final_g):
    assert w_in.shape[0] == 1, "single-layer trunk"
    win = w_in[0]
    a3 = 3 * ATTN_W
    w_all = jnp.concatenate(
        [win[:, :a3], _forget_lanes(win[:, a3:a3 + HEADS]), win[:, a3 + HEADS:]], axis=1).astype(BF16)
    wr = jnp.concatenate(
        [w_router_group[0], jnp.zeros((D_MODEL, EPG - N_GROUPS), F32),
         w_router_expert[0].transpose(1, 0, 2).reshape(D_MODEL, N_GROUPS * EPG),
         jnp.zeros((D_MODEL, LANES - EPG - N_GROUPS * EPG), F32)], axis=1).astype(BF16)
    br = jnp.concatenate(
        [b_router_group[0], jnp.zeros((EPG - N_GROUPS,), F32), b_router_expert[0].reshape(-1),
         jnp.zeros((LANES - EPG - N_GROUPS * EPG,), F32)]).reshape(1, LANES)
    w = dict(
        gm=norm_mix_g[0].reshape(1, D_MODEL), w_all=w_all, bias_f=_forget_lanes(b_forget[0]).reshape(1, LANES),
        cw=jnp.concatenate([conv_w[0], jnp.zeros((8 - conv_w.shape[1], CONV_W), F32)], axis=0),
        gc=norm_conv_g[0].reshape(1, CONV_W), wo=w_out[0].astype(BF16), ga=norm_attn_g[0].reshape(1, ATTN_W),
        gf=norm_ffn_g[0].reshape(1, D_MODEL), wr=wr, br=br,
        wgu=w_expert_gate_up[0].astype(BF16), wdn=w_expert_down[0].astype(BF16),
        gn=norm_final_g.reshape(1, D_MODEL))

    bp, sp, _ = x_prompt.shape
    bs, ss, _ = x_sample.shape
    plen = cache_k.shape[2]
    yp, kp, vp, lfp, cvp = _trunk(x_prompt, _attention, jnp.zeros((bp, 8, CONV_W), F32), w, 512, 512)

    ck = cache_k[0].reshape(bs, plen, ATTN_W)
    cv = cache_v[0].reshape(bs, plen, ATTN_W)
    clf = _forget_lanes(cache_logf[0])
    st = jnp.concatenate([jnp.zeros((bs, 6, CONV_W), F32), state_conv[0]], axis=1)
    attend_s = lambda qa, ka, v: _attention_cache(qa, ka, v, ck, cv, clf)
    ys, ks, vs, lfs, cvs = _trunk(x_sample, attend_s, st, w, ss, bs * ss)
    return (yp, ys, kp, vp, lfp, cvp, ks, vs, lfs, cvs)
```

```python
import functools
import math

import jax
import jax.numpy as jnp
from jax import lax
from jax.experimental import pallas as pl
from jax.experimental.pallas import tpu as pltpu

F32 = jnp.float32
BF16 = jnp.bfloat16

D_MODEL = 1024
HEADS = 8
HEAD_DIM = 64
ATTN_W = HEADS * HEAD_DIM
CONV_W = 512
N_GROUPS = 4
EPG = 8
D_EXPERT = 256
EPS = 1e-6
LOG2E = math.log2(math.e)
Q_SCALE = HEAD_DIM ** -0.5 * LOG2E

LANES = 128
COL_Q, COL_K, COL_V, COL_F, COL_BG, COL_CG, COL_H = 0, 512, 1024, 1536, 1664, 2176, 2688
PROJ_PAD = 3200
F_OUT, F_HI, F_MID, F_LO = 0, 64, 72, 80

VMEM_LIMIT = 56 * 1024 * 1024


def _dot(a, b):
    return jnp.dot(a, b, preferred_element_type=F32)


def _dot_nt(a, b):
    return lax.dot_general(a, b, (((1,), (1,)), ((), ())), preferred_element_type=F32)


def _rms(x, g):
    return x * lax.rsqrt(jnp.mean(x * x, axis=-1, keepdims=True) + EPS) * g


def _log_sigmoid(x):
    return jnp.minimum(x, 0.0) - jnp.log(1.0 + jnp.exp(-jnp.abs(x)))


def _split3(x):
    a1 = x.astype(BF16)
    r1 = x - a1.astype(F32)
    a2 = r1.astype(BF16)
    a3 = (r1 - a2.astype(F32)).astype(BF16)
    return a1, a2, a3


def _cumsum_rows(x):
    n = x.shape[0]
    r = lax.broadcasted_iota(jnp.int32, (n, n), 0)
    c = lax.broadcasted_iota(jnp.int32, (n, n), 1)
    tri = jnp.where(r >= c, 1.0, 0.0).astype(BF16)
    parts = _dot(tri, jnp.concatenate(_split3(x), axis=1))
    return parts[:, :LANES] + parts[:, LANES:2 * LANES] + parts[:, 2 * LANES:]


def _key_bias_lanes(cneg):
    lane = lax.broadcasted_iota(jnp.int32, cneg.shape, 1)
    hi = cneg.astype(BF16).astype(F32)
    r1 = cneg - hi
    mid = r1.astype(BF16).astype(F32)
    lo = r1 - mid
    out = jnp.where(lane < F_MID, hi, jnp.where(lane < F_LO, mid, lo))
    return jnp.where((lane >= F_HI) & (lane < F_LO + HEADS), out, 0.0)


def _head_block(x, h):
    blk = x[:, (h // 2) * LANES:(h // 2 + 1) * LANES]
    return pltpu.roll(blk, HEAD_DIM, 1) if h % 2 else blk


def _key_operand(k, bias_lanes, h):
    lane = lax.broadcasted_iota(jnp.int32, bias_lanes.shape, 1)
    return jnp.where(lane < HEAD_DIM, _head_block(k, h), bias_lanes).astype(BF16)


def _proj_kernel(x_ref, st_ref, gm_ref, w_ref, bf_ref, cw_ref, gc_ref,
                 k_ref, v_ref, lf_ref, qa_ref, ka_ref, mb_ref, ncv_ref, ubuf, ccar):
    i = pl.program_id(1)
    tm = x_ref.shape[1]
    xn = _rms(x_ref[0], gm_ref[...]).astype(BF16)
    p = _dot(xn, w_ref[...])
    q = p[:, COL_Q:COL_Q + ATTN_W]
    k = p[:, COL_K:COL_K + ATTN_W]
    k_ref[0] = k
    v_ref[0] = p[:, COL_V:COL_V + ATTN_W]

    lfb = _log_sigmoid(p[:, COL_F:COL_F + LANES] + bf_ref[...])
    lf_ref[0] = lfb[:, F_OUT:F_OUT + HEADS]

    @pl.when(i == 0)
    def _():
        ccar[...] = jnp.zeros_like(ccar)

    c = _cumsum_rows(lfb) + ccar[0:1, :]
    ccar[...] = jnp.broadcast_to(c[tm - 1:tm, :], ccar.shape)
    bias_lanes = _key_bias_lanes(c * (-LOG2E))
    lane = lax.broadcasted_iota(jnp.int32, (tm, LANES), 1)
    for h in range(HEADS):
        ka_ref[0, h] = _key_operand(k, bias_lanes, h)
        sel = jnp.where((lane == F_HI + h) | (lane == F_MID + h) | (lane == F_LO + h), 1.0, 0.0)
        qa_ref[0, h] = jnp.where(lane < HEAD_DIM, _head_block(q, h) * Q_SCALE, sel).astype(BF16)

    u = p[:, COL_CG:COL_CG + CONV_W] * p[:, COL_H:COL_H + CONV_W]

    @pl.when(i == 0)
    def _():
        ubuf[0:8, :] = st_ref[0]

    @pl.when(i > 0)
    def _():
        ubuf[0:8, :] = ubuf[tm:tm + 8, :]

    ubuf[8:8 + tm, :] = u
    y = cw_ref[0:1, :] * ubuf[6:6 + tm, :] + cw_ref[1:2, :] * ubuf[7:7 + tm, :] + cw_ref[2:3, :] * u
    ob = p[:, COL_BG:COL_BG + CONV_W] * y
    mb_ref[0] = _rms(ob, gc_ref[...]).astype(BF16)
    ncv_ref[0] = u[tm - 2:tm, :]


def _proj(x, state, gm, w_all, bias_f, cw, gc, tm):
    b, s, _ = x.shape
    full = lambda shape: pl.BlockSpec(shape, lambda bi, i: (0,) * len(shape))
    return pl.pallas_call(
        _proj_kernel,
        grid=(b, s // tm),
        in_specs=[
            pl.BlockSpec((1, tm, D_MODEL), lambda bi, i: (bi, i, 0)),
            pl.BlockSpec((1, 8, CONV_W), lambda bi, i: (bi, 0, 0)),
            full((1, D_MODEL)),
            full((D_MODEL, PROJ_PAD)),
            full((1, LANES)),
            full((8, CONV_W)),
            full((1, CONV_W)),
        ],
        out_specs=[
            pl.BlockSpec((1, tm, ATTN_W), lambda bi, i: (bi, i, 0)),
            pl.BlockSpec((1, tm, ATTN_W), lambda bi, i: (bi, i, 0)),
            pl.BlockSpec((1, tm, HEADS), lambda bi, i: (bi, i, 0)),
            pl.BlockSpec((1, HEADS, tm, LANES), lambda bi, i: (bi, 0, i, 0)),
            pl.BlockSpec((1, HEADS, tm, LANES), lambda bi, i: (bi, 0, i, 0)),
            pl.BlockSpec((1, tm, CONV_W), lambda bi, i: (bi, i, 0)),
            pl.BlockSpec((1, 2, CONV_W), lambda bi, i: (bi, 0, 0)),
        ],
        out_shape=[
            jax.ShapeDtypeStruct((b, s, ATTN_W), F32),
            jax.ShapeDtypeStruct((b, s, ATTN_W), F32),
            jax.ShapeDtypeStruct((b, s, HEADS), F32),
            jax.ShapeDtypeStruct((b, HEADS, s, LANES), BF16),
            jax.ShapeDtypeStruct((b, HEADS, s, LANES), BF16),
            jax.ShapeDtypeStruct((b, s, CONV_W), BF16),
            jax.ShapeDtypeStruct((b, 2, CONV_W), F32),
        ],
        scratch_shapes=[pltpu.VMEM((tm + 8, CONV_W), F32), pltpu.VMEM((8, LANES), F32)],
        compiler_params=pltpu.CompilerParams(
            dimension_semantics=("arbitrary", "arbitrary"), vmem_limit_bytes=VMEM_LIMIT),
        name="proj",
    )(x, state, gm, w_all, bias_f, cw, gc)


ATT_BLK = 256


def _attn_kernel(qa_ref, ka_ref, v_ref, o_ref, vt):
    iq = pl.program_id(2)
    blk = ATT_BLK

    @pl.when(iq == 0)
    def _():
        vt[...] = v_ref[0].T.astype(BF16)

    r = lax.broadcasted_iota(jnp.int32, (blk, blk), 0)
    c = lax.broadcasted_iota(jnp.int32, (blk, blk), 1)
    causal = r <= c

    def run(nkv):
        noff = (nkv - 1) * blk
        outs = []
        for hh in range(2):
            q = qa_ref[0, hh]
            rows = slice(hh * HEAD_DIM, (hh + 1) * HEAD_DIM)
            s_d = jnp.where(causal, _dot_nt(ka_ref[0, hh, noff:noff + blk, :], q), -jnp.inf)
            m = jnp.max(s_d, axis=0, keepdims=True)
            if nkv > 1:
                s_o = _dot_nt(ka_ref[0, hh, 0:noff, :], q)
                m = jnp.maximum(m, jnp.max(s_o, axis=0, keepdims=True))
            p_d = jnp.exp2(s_d - m)
            l = jnp.sum(p_d, axis=0, keepdims=True)
            acc = _dot(vt[rows, noff:noff + blk], p_d.astype(BF16))
            if nkv > 1:
                p_o = jnp.exp2(s_o - m)
                l = l + jnp.sum(p_o, axis=0, keepdims=True)
                acc = acc + _dot(vt[rows, 0:noff], p_o.astype(BF16))
            outs.append(acc / l)
        o_ref[0] = jnp.concatenate(outs, axis=0).T

    for nkv in range(1, ka_ref.shape[2] // blk + 1):
        pl.when(iq == nkv - 1)(functools.partial(run, nkv))


def _attention(qa, ka, v):
    b, _, s, _ = qa.shape
    return pl.pallas_call(
        _attn_kernel,
        grid=(b, HEADS // 2, s // ATT_BLK),
        in_specs=[
            pl.BlockSpec((1, 2, ATT_BLK, LANES), lambda bi, p, iq: (bi, p, iq, 0)),
            pl.BlockSpec((1, 2, s, LANES), lambda bi, p, iq: (bi, p, 0, 0)),
            pl.BlockSpec((1, s, LANES), lambda bi, p, iq: (bi, 0, p)),
        ],
        out_specs=pl.BlockSpec((1, ATT_BLK, LANES), lambda bi, p, iq: (bi, iq, p)),
        out_shape=jax.ShapeDtypeStruct((b, s, ATTN_W), F32),
        scratch_shapes=[pltpu.VMEM((LANES, s), BF16)],
        compiler_params=pltpu.CompilerParams(
            dimension_semantics=("arbitrary", "arbitrary", "arbitrary"), vmem_limit_bytes=VMEM_LIMIT),
        name="attention",
    )(qa, ka, v)


Q_PAD = 128


def _pad_rows(x, n):
    return jnp.concatenate([x, jnp.zeros((n - x.shape[0], x.shape[1]), x.dtype)], axis=0)


def _attn_cache_kernel(qa_ref, ka_ref, v_ref, ck_ref, cv_ref, clf_ref, o_ref):
    t = qa_ref.shape[2]
    plen = ck_ref.shape[1]
    cpast = _cumsum_rows(clf_ref[0])
    cpast = cpast - cpast[plen - 1:plen, :]
    bias_lanes = _key_bias_lanes(cpast * (-LOG2E))
    ck = ck_ref[0]
    r = lax.broadcasted_iota(jnp.int32, (Q_PAD, Q_PAD), 0)
    c = lax.broadcasted_iota(jnp.int32, (Q_PAD, Q_PAD), 1)
    new_ok = (r <= c) & (r < t)

    pairs = []
    for p in range(HEADS // 2):
        vct = cv_ref[0][:, p * LANES:(p + 1) * LANES].T.astype(BF16)
        vnt = _pad_rows(v_ref[0][:, p * LANES:(p + 1) * LANES], Q_PAD).T.astype(BF16)
        outs = []
        for hh in range(2):
            h = 2 * p + hh
            q = _pad_rows(qa_ref[0, h], Q_PAD)
            s_past = _dot_nt(_key_operand(ck, bias_lanes, h), q)
            s_new = jnp.where(new_ok, _dot_nt(_pad_rows(ka_ref[0, h], Q_PAD), q), -jnp.inf)
            m = jnp.maximum(jnp.max(s_past, axis=0, keepdims=True), jnp.max(s_new, axis=0, keepdims=True))
            p_past = jnp.exp2(s_past - m)
            p_new = jnp.exp2(s_new - m)
            l = jnp.sum(p_past, axis=0, keepdims=True) + jnp.sum(p_new, axis=0, keepdims=True)
            rows = slice(hh * HEAD_DIM, (hh + 1) * HEAD_DIM)
            acc = _dot(vct[rows], p_past.astype(BF16)) + _dot(vnt[rows], p_new.astype(BF16))
            outs.append(acc / l)
        pairs.append(jnp.concatenate(outs, axis=0).T[0:t, :])
    o_ref[0] = jnp.concatenate(pairs, axis=1)


def _attention_cache(qa, ka, v, ck, cv, clf):
    b, _, t, _ = qa.shape
    plen = ck.shape[1]
    return pl.pallas_call(
        _attn_cache_kernel,
        grid=(b,),
        in_specs=[
            pl.BlockSpec((1, HEADS, t, LANES), lambda bi: (bi, 0, 0, 0)),
            pl.BlockSpec((1, HEADS, t, LANES), lambda bi: (bi, 0, 0, 0)),
            pl.BlockSpec((1, t, ATTN_W), lambda bi: (bi, 0, 0)),
            pl.BlockSpec((1, plen, ATTN_W), lambda bi: (bi, 0, 0)),
            pl.BlockSpec((1, plen, ATTN_W), lambda bi: (bi, 0, 0)),
            pl.BlockSpec((1, plen, LANES), lambda bi: (bi, 0, 0)),
        ],
        out_specs=pl.BlockSpec((1, t, ATTN_W), lambda bi: (bi, 0, 0)),
        out_shape=jax.ShapeDtypeStruct((b, t, ATTN_W), F32),
        compiler_params=pltpu.CompilerParams(
            dimension_semantics=("arbitrary",), vmem_limit_bytes=VMEM_LIMIT),
        name="attention_cache",
    )(qa, ka, v, ck, cv, clf)


def _mix_kernel(x_ref, o_ref, mb_ref, wo_ref, ga_ref, gf_ref, wr_ref, br_ref, x2_ref, xn_ref, comb_ref):
    tm = x_ref.shape[0]
    oa = _rms(o_ref[...], ga_ref[...]).astype(BF16)
    merged = jnp.concatenate([oa, mb_ref[...]], axis=1)
    x2 = x_ref[...] + _dot(merged, wo_ref[...])
    x2_ref[...] = x2
    xb = _rms(x2, gf_ref[...]).astype(BF16)
    xn_ref[...] = xb

    lt = (_dot(xb, wr_ref[...]) + br_ref[...]).T
    row = lax.broadcasted_iota(jnp.int32, (EPG, tm), 0)
    lg = jnp.where(row < N_GROUPS, lt[0:EPG], -jnp.inf)
    eg = jnp.exp(lg - jnp.max(lg, axis=0, keepdims=True))
    pg = eg / jnp.sum(eg, axis=0, keepdims=True)
    pg_top = jnp.max(pg, axis=0, keepdims=True)
    g_top = jnp.min(jnp.where(pg == pg_top, row, EPG), axis=0, keepdims=True)

    le = lt[EPG:2 * EPG]
    for g in range(1, N_GROUPS):
        le = jnp.where(g_top == g, lt[EPG * (g + 1):EPG * (g + 2)], le)
    ee = jnp.exp(le - jnp.max(le, axis=0, keepdims=True))
    pe = ee / jnp.sum(ee, axis=0, keepdims=True)
    v1 = jnp.max(pe, axis=0, keepdims=True)
    i1 = jnp.min(jnp.where(pe == v1, row, EPG), axis=0, keepdims=True)
    pe2 = jnp.where(row == i1, -1.0, pe)
    v2 = jnp.max(pe2, axis=0, keepdims=True)
    i2 = jnp.min(jnp.where(pe2 == v2, row, EPG), axis=0, keepdims=True)
    den = v1 + v2
    wg = jnp.where(row == i1, v1 / den * pg_top, 0.0) + jnp.where(row == i2, v2 / den * pg_top, 0.0)
    slabs = [jnp.where(g_top == g, wg, 0.0) for g in range(N_GROUPS)]
    slabs.append(jnp.zeros((LANES - N_GROUPS * EPG, tm), F32))
    comb_ref[...] = jnp.concatenate(slabs, axis=0).T


def _mix(x, o, mb, wo, ga, gf, wr, br, tm):
    n = x.shape[0]
    full = lambda shape: pl.BlockSpec(shape, lambda i: (0,) * len(shape))
    return pl.pallas_call(
        _mix_kernel,
        grid=(n // tm,),
        in_specs=[
            pl.BlockSpec((tm, D_MODEL), lambda i: (i, 0)),
            pl.BlockSpec((tm, ATTN_W), lambda i: (i, 0)),
            pl.BlockSpec((tm, CONV_W), lambda i: (i, 0)),
            full((D_MODEL, D_MODEL)),
            full((1, ATTN_W)),
            full((1, D_MODEL)),
            full((D_MODEL, LANES)),
            full((1, LANES)),
        ],
        out_specs=[
            pl.BlockSpec((tm, D_MODEL), lambda i: (i, 0)),
            pl.BlockSpec((tm, D_MODEL), lambda i: (i, 0)),
            pl.BlockSpec((tm, LANES), lambda i: (i, 0)),
        ],
        out_shape=[
            jax.ShapeDtypeStruct((n, D_MODEL), F32),
            jax.ShapeDtypeStruct((n, D_MODEL), BF16),
            jax.ShapeDtypeStruct((n, LANES), F32),
        ],
        compiler_params=pltpu.CompilerParams(
            dimension_semantics=("arbitrary",), vmem_limit_bytes=VMEM_LIMIT),
        name="mix",
    )(x, o, mb, wo, ga, gf, wr, br)


def _experts_kernel(x2_ref, xn_ref, comb_ref, wgu_ref, wdn_ref, gn_ref, y_ref, acc):
    g = pl.program_id(1)

    @pl.when(g == 0)
    def _():
        acc[...] = jnp.zeros_like(acc)

    xb = xn_ref[...]
    comb = comb_ref[...]
    tot = acc[...]
    for e in range(EPG):
        w = comb[:, e:e + 1]
        for gg in range(1, N_GROUPS):
            w = jnp.where(g == gg, comb[:, gg * EPG + e:gg * EPG + e + 1], w)
        gu = _dot(xb, wgu_ref[0, e])
        gate = gu[:, :D_EXPERT]
        hmid = gate * jax.nn.sigmoid(gate) * gu[:, D_EXPERT:] * w
        tot = tot + _dot(hmid.astype(BF16), wdn_ref[0, e])
    acc[...] = tot

    @pl.when(g == N_GROUPS - 1)
    def _():
        y_ref[...] = _rms(x2_ref[...] + tot, gn_ref[...])


def _experts(x2, xn, comb, wgu, wdn, gn, tm):
    n = x2.shape[0]
    return pl.pallas_call(
        _experts_kernel,
        grid=(n // tm, N_GROUPS),
        in_specs=[
            pl.BlockSpec((tm, D_MODEL), lambda i, g: (i, 0)),
            pl.BlockSpec((tm, D_MODEL), lambda i, g: (i, 0)),
            pl.BlockSpec((tm, LANES), lambda i, g: (i, 0)),
            pl.BlockSpec((1, EPG, D_MODEL, 2 * D_EXPERT), lambda i, g: (g, 0, 0, 0)),
            pl.BlockSpec((1, EPG, D_EXPERT, D_MODEL), lambda i, g: (g, 0, 0, 0)),
            pl.BlockSpec((1, D_MODEL), lambda i, g: (0, 0)),
        ],
        out_specs=pl.BlockSpec((tm, D_MODEL), lambda i, g: (i, 0)),
        out_shape=jax.ShapeDtypeStruct((n, D_MODEL), F32),
        scratch_shapes=[pltpu.VMEM((tm, D_MODEL), F32)],
        compiler_params=pltpu.CompilerParams(
            dimension_semantics=("arbitrary", "arbitrary"), vmem_limit_bytes=VMEM_LIMIT),
        name="experts",
    )(x2, xn, comb, wgu, wdn, gn)


def _forget_lanes(f):
    z = lambda n: jnp.zeros(f.shape[:-1] + (n,), f.dtype)
    return jnp.concatenate([f, z(F_HI - HEADS), f, f, f, z(LANES - F_LO - HEADS)], axis=-1)


def _trunk(x, attend, state, w, tm_proj, tm_tok):
    b, s, _ = x.shape
    k, v, lf, qa, ka, mb, ncv = _proj(x, state, w["gm"], w["w_all"], w["bias_f"], w["cw"], w["gc"], tm_proj)
    o = attend(qa, ka, v)
    n = b * s
    x2, xn, comb = _mix(x.reshape(n, D_MODEL), o.reshape(n, ATTN_W), mb.reshape(n, CONV_W),
                        w["wo"], w["ga"], w["gf"], w["wr"], w["br"], tm_tok)
    y = _experts(x2, xn, comb, w["wgu"], w["wdn"], w["gn"], tm_tok)
    return (y.reshape(b, s, D_MODEL), k.reshape(1, b, s, HEADS, HEAD_DIM), v.reshape(1, b, s, HEADS, HEAD_DIM),
            lf.reshape(1, b, s, HEADS), ncv.reshape(1, b, 2, CONV_W))


def kernel(x_prompt, x_sample, cache_k, cache_v, cache_logf, state_conv, norm_mix_g, w_in, b_forget, conv_w, norm_attn_g, norm_conv_g, w_out, norm_ffn_g, w_router_group, b_router_group, w_router_expert, b_router_expert, w_expert_gate_up, w_expert_down, norm_final_g):
    assert w_in.shape[0] == 1, "single-layer trunk"
    win = w_in[0]
    a3 = 3 * ATTN_W
    w_all = jnp.concatenate(
        [win[:, :a3], _forget_lanes(win[:, a3:a3 + HEADS]), win[:, a3 + HEADS:]], axis=1).astype(BF16)
    wr = jnp.concatenate(
        [w_router_group[0], jnp.zeros((D_MODEL, EPG - N_GROUPS), F32),
         w_router_expert[0].transpose(1, 0, 2).reshape(D_MODEL, N_GROUPS * EPG),
         jnp.zeros((D_MODEL, LANES - EPG - N_GROUPS * EPG), F32)], axis=1).astype(BF16)
    br = jnp.concatenate(
        [b_router_group[0], jnp.zeros((EPG - N_GROUPS,), F32), b_router_expert[0].reshape(-1),
         jnp.zeros((LANES - EPG - N_GROUPS * EPG,), F32)]).reshape(1, LANES)
    w = dict(
        gm=norm_mix_g[0].reshape(1, D_MODEL), w_all=w_all, bias_f=_forget_lanes(b_forget[0]).reshape(1, LANES),
        cw=jnp.concatenate([conv_w[0], jnp.zeros((8 - conv_w.shape[1], CONV_W), F32)], axis=0),
        gc=norm_conv_g[0].reshape(1, CONV_W), wo=w_out[0].astype(BF16), ga=norm_attn_g[0].reshape(1, ATTN_W),
        gf=norm_ffn_g[0].reshape(1, D_MODEL), wr=wr, br=br,
        wgu=w_expert_gate_up[0].astype(BF16), wdn=w_expert_down[0].astype(BF16),
        gn=norm_final_g.reshape(1, D_MODEL))

    bp, sp, _ = x_prompt.shape
    bs, ss, _ = x_sample.shape
    plen = cache_k.shape[2]
    yp, kp, vp, lfp, cvp = _trunk(x_prompt, _attention, jnp.zeros((bp, 8, CONV_W), F32), w, 512, 512)

    ck = cache_k[0].reshape(bs, plen, ATTN_W)
    cv = cache_v[0].reshape(bs, plen, ATTN_W)
    clf = _forget_lanes(cache_logf[0])
    st = jnp.concatenate([jnp.zeros((bs, 6, CONV_W), F32), state_conv[0]], axis=1)
    attend_s = lambda qa, ka, v: _attention_cache(qa, ka, v, ck, cv, clf)
    ys, ks, vs, lfs, cvs = _trunk(x_sample, attend_s, st, w, ss, bs * ss)
    return (yp, ys, kp, vp, lfp, cvp, ks, vs, lfs, cvs)
```

```python
import functools
import math

import jax
import jax.numpy as jnp
from jax import lax
from jax.experimental import pallas as pl
from jax.experimental.pallas import tpu as pltpu

F32 = jnp.float32
BF16 = jnp.bfloat16

D_MODEL = 1024
HEADS = 8
HEAD_DIM = 64
ATTN_W = HEADS * HEAD_DIM
CONV_W = 512
N_GROUPS = 4
EPG = 8
D_EXPERT = 256
EPS = 1e-6
LOG2E = math.log2(math.e)
Q_SCALE = HEAD_DIM ** -0.5 * LOG2E

LANES = 128
COL_Q, COL_K, COL_V, COL_F, COL_BG, COL_CG, COL_H = 0, 512, 1024, 1536, 1664, 2176, 2688
PROJ_PAD = 3200
F_OUT, F_HI, F_MID, F_LO = 0, 64, 72, 80

GROUP_SHIFT = 20
ROW_W = D_MODEL + LANES
TILE_SHIFT = 8
EXPERT_TILE = 1 << TILE_SHIFT

VMEM_LIMIT = 56 * 1024 * 1024


def _dot(a, b):
    return jnp.dot(a, b, preferred_element_type=F32)


def _dot_nt(a, b):
    return lax.dot_general(a, b, (((1,), (1,)), ((), ())), preferred_element_type=F32)


def _rms(x, g):
    return x * lax.rsqrt(jnp.mean(x * x, axis=-1, keepdims=True) + EPS) * g


def _log_sigmoid(x):
    return jnp.minimum(x, 0.0) - jnp.log(1.0 + jnp.exp(-jnp.abs(x)))


def _split3(x):
    a1 = x.astype(BF16)
    r1 = x - a1.astype(F32)
    a2 = r1.astype(BF16)
    a3 = (r1 - a2.astype(F32)).astype(BF16)
    return a1, a2, a3


def _cumsum_rows(x):
    n = x.shape[0]
    r = lax.broadcasted_iota(jnp.int32, (n, n), 0)
    c = lax.broadcasted_iota(jnp.int32, (n, n), 1)
    tri = jnp.where(r >= c, 1.0, 0.0).astype(BF16)
    parts = _dot(tri, jnp.concatenate(_split3(x), axis=1))
    return parts[:, :LANES] + parts[:, LANES:2 * LANES] + parts[:, 2 * LANES:]


def _key_bias_lanes(cneg):
    lane = lax.broadcasted_iota(jnp.int32, cneg.shape, 1)
    hi = cneg.astype(BF16).astype(F32)
    r1 = cneg - hi
    mid = r1.astype(BF16).astype(F32)
    lo = r1 - mid
    out = jnp.where(lane < F_MID, hi, jnp.where(lane < F_LO, mid, lo))
    return jnp.where((lane >= F_HI) & (lane < F_LO + HEADS), out, 0.0)


def _head_block(x, h):
    blk = x[:, (h // 2) * LANES:(h // 2 + 1) * LANES]
    return pltpu.roll(blk, HEAD_DIM, 1) if h % 2 else blk


def _key_operand(k, bias_lanes, h):
    lane = lax.broadcasted_iota(jnp.int32, bias_lanes.shape, 1)
    return jnp.where(lane < HEAD_DIM, _head_block(k, h), bias_lanes).astype(BF16)


def _proj_kernel(x_ref, st_ref, gm_ref, w_ref, bf_ref, cw_ref, gc_ref,
                 k_ref, v_ref, lf_ref, qa_ref, ka_ref, mb_ref, ncv_ref, ubuf, ccar):
    i = pl.program_id(1)
    tm = x_ref.shape[1]
    xn = _rms(x_ref[0], gm_ref[...]).astype(BF16)
    p = _dot(xn, w_ref[...])
    q = p[:, COL_Q:COL_Q + ATTN_W]
    k = p[:, COL_K:COL_K + ATTN_W]
    k_ref[0] = k
    v_ref[0] = p[:, COL_V:COL_V + ATTN_W]

    lfb = _log_sigmoid(p[:, COL_F:COL_F + LANES] + bf_ref[...])
    lf_ref[0] = lfb[:, F_OUT:F_OUT + HEADS]

    @pl.when(i == 0)
    def _():
        ccar[...] = jnp.zeros_like(ccar)

    c = _cumsum_rows(lfb) + ccar[0:1, :]
    ccar[...] = jnp.broadcast_to(c[tm - 1:tm, :], ccar.shape)
    bias_lanes = _key_bias_lanes(c * (-LOG2E))
    lane = lax.broadcasted_iota(jnp.int32, (tm, LANES), 1)
    for h in range(HEADS):
        ka_ref[0, h] = _key_operand(k, bias_lanes, h)
        sel = jnp.where((lane == F_HI + h) | (lane == F_MID + h) | (lane == F_LO + h), 1.0, 0.0)
        qa_ref[0, h] = jnp.where(lane < HEAD_DIM, _head_block(q, h) * Q_SCALE, sel).astype(BF16)

    u = p[:, COL_CG:COL_CG + CONV_W] * p[:, COL_H:COL_H + CONV_W]

    @pl.when(i == 0)
    def _():
        ubuf[0:8, :] = st_ref[0]

    @pl.when(i > 0)
    def _():
        ubuf[0:8, :] = ubuf[tm:tm + 8, :]

    ubuf[8:8 + tm, :] = u
    y = cw_ref[0:1, :] * ubuf[6:6 + tm, :] + cw_ref[1:2, :] * ubuf[7:7 + tm, :] + cw_ref[2:3, :] * u
    ob = p[:, COL_BG:COL_BG + CONV_W] * y
    mb_ref[0] = _rms(ob, gc_ref[...]).astype(BF16)
    ncv_ref[0] = u[tm - 2:tm, :]


def _proj(x, state, gm, w_all, bias_f, cw, gc, tm):
    b, s, _ = x.shape
    full = lambda shape: pl.BlockSpec(shape, lambda bi, i: (0,) * len(shape))
    return pl.pallas_call(
        _proj_kernel,
        grid=(b, s // tm),
        in_specs=[
            pl.BlockSpec((1, tm, D_MODEL), lambda bi, i: (bi, i, 0)),
            pl.BlockSpec((1, 8, CONV_W), lambda bi, i: (bi, 0, 0)),
            full((1, D_MODEL)),
            full((D_MODEL, PROJ_PAD)),
            full((1, LANES)),
            full((8, CONV_W)),
            full((1, CONV_W)),
        ],
        out_specs=[
            pl.BlockSpec((1, tm, ATTN_W), lambda bi, i: (bi, i, 0)),
            pl.BlockSpec((1, tm, ATTN_W), lambda bi, i: (bi, i, 0)),
            pl.BlockSpec((1, tm, HEADS), lambda bi, i: (bi, i, 0)),
            pl.BlockSpec((1, HEADS, tm, LANES), lambda bi, i: (bi, 0, i, 0)),
            pl.BlockSpec((1, HEADS, tm, LANES), lambda bi, i: (bi, 0, i, 0)),
            pl.BlockSpec((1, tm, CONV_W), lambda bi, i: (bi, i, 0)),
            pl.BlockSpec((1, 2, CONV_W), lambda bi, i: (bi, 0, 0)),
        ],
        out_shape=[
            jax.ShapeDtypeStruct((b, s, ATTN_W), F32),
            jax.ShapeDtypeStruct((b, s, ATTN_W), F32),
            jax.ShapeDtypeStruct((b, s, HEADS), F32),
            jax.ShapeDtypeStruct((b, HEADS, s, LANES), BF16),
            jax.ShapeDtypeStruct((b, HEADS, s, LANES), BF16),
            jax.ShapeDtypeStruct((b, s, CONV_W), BF16),
            jax.ShapeDtypeStruct((b, 2, CONV_W), F32),
        ],
        scratch_shapes=[pltpu.VMEM((tm + 8, CONV_W), F32), pltpu.VMEM((8, LANES), F32)],
        compiler_params=pltpu.CompilerParams(
            dimension_semantics=("arbitrary", "arbitrary"), vmem_limit_bytes=VMEM_LIMIT),
        name="proj",
    )(x, state, gm, w_all, bias_f, cw, gc)


ATT_BLK = 256


def _attn_kernel(qa_ref, ka_ref, v_ref, o_ref, vt):
    iq = pl.program_id(2)
    blk = ATT_BLK

    @pl.when(iq == 0)
    def _():
        vt[...] = v_ref[0].T.astype(BF16)

    r = lax.broadcasted_iota(jnp.int32, (blk, blk), 0)
    c = lax.broadcasted_iota(jnp.int32, (blk, blk), 1)
    causal = r <= c

    def run(nkv):
        noff = (nkv - 1) * blk
        outs = []
        for hh in range(2):
            q = qa_ref[0, hh]
            rows = slice(hh * HEAD_DIM, (hh + 1) * HEAD_DIM)
            s_d = jnp.where(causal, _dot_nt(ka_ref[0, hh, noff:noff + blk, :], q), -jnp.inf)
            m = jnp.max(s_d, axis=0, keepdims=True)
            if nkv > 1:
                s_o = _dot_nt(ka_ref[0, hh, 0:noff, :], q)
                m = jnp.maximum(m, jnp.max(s_o, axis=0, keepdims=True))
            p_d = jnp.exp2(s_d - m)
            l = jnp.sum(p_d, axis=0, keepdims=True)
            acc = _dot(vt[rows, noff:noff + blk], p_d.astype(BF16))
            if nkv > 1:
                p_o = jnp.exp2(s_o - m)
                l = l + jnp.sum(p_o, axis=0, keepdims=True)
                acc = acc + _dot(vt[rows, 0:noff], p_o.astype(BF16))
            outs.append(acc / l)
        o_ref[0] = jnp.concatenate(outs, axis=0).T

    for nkv in range(1, ka_ref.shape[2] // blk + 1):
        pl.when(iq == nkv - 1)(functools.partial(run, nkv))


def _attention(qa, ka, v):
    b, _, s, _ = qa.shape
    return pl.pallas_call(
        _attn_kernel,
        grid=(b, HEADS // 2, s // ATT_BLK),
        in_specs=[
            pl.BlockSpec((1, 2, ATT_BLK, LANES), lambda bi, p, iq: (bi, p, iq, 0)),
            pl.BlockSpec((1, 2, s, LANES), lambda bi, p, iq: (bi, p, 0, 0)),
            pl.BlockSpec((1, s, LANES), lambda bi, p, iq: (bi, 0, p)),
        ],
        out_specs=pl.BlockSpec((1, ATT_BLK, LANES), lambda bi, p, iq: (bi, iq, p)),
        out_shape=jax.ShapeDtypeStruct((b, s, ATTN_W), F32),
        scratch_shapes=[pltpu.VMEM((LANES, s), BF16)],
        compiler_params=pltpu.CompilerParams(
            dimension_semantics=("arbitrary", "arbitrary", "arbitrary"), vmem_limit_bytes=VMEM_LIMIT),
        name="attention",
    )(qa, ka, v)


Q_PAD = 128


def _pad_rows(x, n):
    return jnp.concatenate([x, jnp.zeros((n - x.shape[0], x.shape[1]), x.dtype)], axis=0)


def _attn_cache_kernel(qa_ref, ka_ref, v_ref, ck_ref, cv_ref, clf_ref, o_ref):
    t = qa_ref.shape[2]
    plen = ck_ref.shape[1]
    cpast = _cumsum_rows(clf_ref[0])
    cpast = cpast - cpast[plen - 1:plen, :]
    bias_lanes = _key_bias_lanes(cpast * (-LOG2E))
    ck = ck_ref[0]
    r = lax.broadcasted_iota(jnp.int32, (Q_PAD, Q_PAD), 0)
    c = lax.broadcasted_iota(jnp.int32, (Q_PAD, Q_PAD), 1)
    new_ok = (r <= c) & (r < t)

    pairs = []
    for p in range(HEADS // 2):
        vct = cv_ref[0][:, p * LANES:(p + 1) * LANES].T.astype(BF16)
        vnt = _pad_rows(v_ref[0][:, p * LANES:(p + 1) * LANES], Q_PAD).T.astype(BF16)
        outs = []
        for hh in range(2):
            h = 2 * p + hh
            q = _pad_rows(qa_ref[0, h], Q_PAD)
            s_past = _dot_nt(_key_operand(ck, bias_lanes, h), q)
            s_new = jnp.where(new_ok, _dot_nt(_pad_rows(ka_ref[0, h], Q_PAD), q), -jnp.inf)
            m = jnp.maximum(jnp.max(s_past, axis=0, keepdims=True), jnp.max(s_new, axis=0, keepdims=True))
            p_past = jnp.exp2(s_past - m)
            p_new = jnp.exp2(s_new - m)
            l = jnp.sum(p_past, axis=0, keepdims=True) + jnp.sum(p_new, axis=0, keepdims=True)
            rows = slice(hh * HEAD_DIM, (hh + 1) * HEAD_DIM)
            acc = _dot(vct[rows], p_past.astype(BF16)) + _dot(vnt[rows], p_new.astype(BF16))
            outs.append(acc / l)
        pairs.append(jnp.concatenate(outs, axis=0).T[0:t, :])
    o_ref[0] = jnp.concatenate(pairs, axis=1)


def _attention_cache(qa, ka, v, ck, cv, clf):
    b, _, t, _ = qa.shape
    plen = ck.shape[1]
    return pl.pallas_call(
        _attn_cache_kernel,
        grid=(b,),
        in_specs=[
            pl.BlockSpec((1, HEADS, t, LANES), lambda bi: (bi, 0, 0, 0)),
            pl.BlockSpec((1, HEADS, t, LANES), lambda bi: (bi, 0, 0, 0)),
            pl.BlockSpec((1, t, ATTN_W), lambda bi: (bi, 0, 0)),
            pl.BlockSpec((1, plen, ATTN_W), lambda bi: (bi, 0, 0)),
            pl.BlockSpec((1, plen, ATTN_W), lambda bi: (bi, 0, 0)),
            pl.BlockSpec((1, plen, LANES), lambda bi: (bi, 0, 0)),
        ],
        out_specs=pl.BlockSpec((1, t, ATTN_W), lambda bi: (bi, 0, 0)),
        out_shape=jax.ShapeDtypeStruct((b, t, ATTN_W), F32),
        compiler_params=pltpu.CompilerParams(
            dimension_semantics=("arbitrary",), vmem_limit_bytes=VMEM_LIMIT),
        name="attention_cache",
    )(qa, ka, v, ck, cv, clf)


def _mix_kernel(x_ref, o_ref, mb_ref, wo_ref, ga_ref, gf_ref, wr_ref, br_ref,
                x2_ref, rows_ref, code_ref, cnt_ref, carry):
    i = pl.program_id(0)
    tm = x_ref.shape[0]
    oa = _rms(o_ref[...], ga_ref[...]).astype(BF16)
    merged = jnp.concatenate([oa, mb_ref[...]], axis=1)
    x2 = x_ref[...] + _dot(merged, wo_ref[...])
    x2_ref[...] = x2
    xb = _rms(x2, gf_ref[...]).astype(BF16)
    rows_ref[:, 0:D_MODEL] = xb.astype(F32)

    lt = (_dot(xb, wr_ref[...]) + br_ref[...]).T
    row = lax.broadcasted_iota(jnp.int32, (EPG, tm), 0)
    lg = jnp.where(row < N_GROUPS, lt[0:EPG], -jnp.inf)
    eg = jnp.exp(lg - jnp.max(lg, axis=0, keepdims=True))
    pg = eg / jnp.sum(eg, axis=0, keepdims=True)
    pg_top = jnp.max(pg, axis=0, keepdims=True)
    g_top = jnp.min(jnp.where(pg == pg_top, row, EPG), axis=0, keepdims=True)

    le = lt[EPG:2 * EPG]
    for g in range(1, N_GROUPS):
        le = jnp.where(g_top == g, lt[EPG * (g + 1):EPG * (g + 2)], le)
    ee = jnp.exp(le - jnp.max(le, axis=0, keepdims=True))
    pe = ee / jnp.sum(ee, axis=0, keepdims=True)
    v1 = jnp.max(pe, axis=0, keepdims=True)
    i1 = jnp.min(jnp.where(pe == v1, row, EPG), axis=0, keepdims=True)
    pe2 = jnp.where(row == i1, -1.0, pe)
    v2 = jnp.max(pe2, axis=0, keepdims=True)
    i2 = jnp.min(jnp.where(pe2 == v2, row, EPG), axis=0, keepdims=True)
    den = v1 + v2
    wg = jnp.where(row == i1, v1 / den * pg_top, 0.0) + jnp.where(row == i2, v2 / den * pg_top, 0.0)
    rows_ref[:, D_MODEL:] = jnp.concatenate([wg, jnp.zeros((LANES - EPG, tm), F32)], axis=0).T

    @pl.when(i == 0)
    def _():
        carry[...] = jnp.zeros_like(carry)

    onehot = jnp.where(row == g_top, 1.0, 0.0)
    r2 = lax.broadcasted_iota(jnp.int32, (tm, tm), 0)
    c2 = lax.broadcasted_iota(jnp.int32, (tm, tm), 1)
    earlier = jnp.where(r2 < c2, 1.0, 0.0).astype(BF16)
    base = carry[:, 0:1]
    rank = jnp.sum(onehot * (_dot(onehot.astype(BF16), earlier) + base), axis=0, keepdims=True)
    code_ref[0] = g_top * (1 << GROUP_SHIFT) + rank.astype(jnp.int32)
    carry[...] = jnp.broadcast_to(base + jnp.sum(onehot, axis=1, keepdims=True), carry.shape)
    cnt_ref[...] = carry[...].astype(jnp.int32)


def _mix(x, o, mb, wo, ga, gf, wr, br, tm):
    n = x.shape[0]
    assert n < (1 << GROUP_SHIFT)
    full = lambda shape: pl.BlockSpec(shape, lambda i: (0,) * len(shape))
    return pl.pallas_call(
        _mix_kernel,
        grid=(n // tm,),
        in_specs=[
            pl.BlockSpec((tm, D_MODEL), lambda i: (i, 0)),
            pl.BlockSpec((tm, ATTN_W), lambda i: (i, 0)),
            pl.BlockSpec((tm, CONV_W), lambda i: (i, 0)),
            full((D_MODEL, D_MODEL)),
            full((1, ATTN_W)),
            full((1, D_MODEL)),
            full((D_MODEL, LANES)),
            full((1, LANES)),
        ],
        out_specs=[
            pl.BlockSpec((tm, D_MODEL), lambda i: (i, 0)),
            pl.BlockSpec((tm, ROW_W), lambda i: (i, 0)),
            pl.BlockSpec((1, 1, tm), lambda i: (i, 0, 0)),
            full((EPG, LANES)),
        ],
        out_shape=[
            jax.ShapeDtypeStruct((n, D_MODEL), F32),
            jax.ShapeDtypeStruct((n, ROW_W), F32),
            jax.ShapeDtypeStruct((n // tm, 1, tm), jnp.int32),
            jax.ShapeDtypeStruct((EPG, LANES), jnp.int32),
        ],
        scratch_shapes=[pltpu.VMEM((EPG, LANES), F32)],
        compiler_params=pltpu.CompilerParams(
            dimension_semantics=("arbitrary",), vmem_limit_bytes=VMEM_LIMIT),
        name="mix",
    )(x, o, mb, wo, ga, gf, wr, br)


def _sorted_tiles(n):
    return n // EXPERT_TILE + N_GROUPS


def _group_tiles(cnt_ref):
    nt = [(cnt_ref[g] + (EXPERT_TILE - 1)) >> TILE_SHIFT for g in range(N_GROUPS)]
    s1 = nt[0]
    s2 = s1 + nt[1]
    s3 = s2 + nt[2]
    return s1, s2, s3, s3 + nt[3]


def _store_group_rows(cnt_ref, offs):
    s1, s2, s3, _ = _group_tiles(cnt_ref)
    offs[0] = 0
    offs[1] = s1 * EXPERT_TILE
    offs[2] = s2 * EXPERT_TILE
    offs[3] = s3 * EXPERT_TILE


def _slot(code, offs):
    return offs[code >> GROUP_SHIFT] + (code & ((1 << GROUP_SHIFT) - 1))


def _scatter_kernel(cnt_ref, rows_ref, code_ref, xs_ref, zeros, offs, sem, zsem):
    tm = rows_ref.shape[0]
    _store_group_rows(cnt_ref, offs)

    @pl.when(pl.program_id(0) == 0)
    def _():
        zeros[...] = jnp.zeros_like(zeros)
        s1, s2, s3, tot = _group_tiles(cnt_ref)
        ends = (s1, s2, s3, tot)

        def fill(tile):
            return pltpu.make_async_copy(zeros, xs_ref.at[pl.ds(tile * EXPERT_TILE, EXPERT_TILE), :], zsem)

        for g in range(N_GROUPS):
            partial = (cnt_ref[g] & (EXPERT_TILE - 1)) != 0

            @pl.when(partial)
            def _():
                fill(ends[g] - 1).start()
                fill(ends[g] - 1).wait()

        def fill_tail(tile, c):
            fill(tile).start()
            fill(tile).wait()
            return c

        lax.fori_loop(tot, xs_ref.shape[0] // EXPERT_TILE, fill_tail, 0)

    def row_copy(r):
        slot = _slot(code_ref[0, 0, r], offs)
        return pltpu.make_async_copy(rows_ref.at[pl.ds(r, 1), :], xs_ref.at[pl.ds(slot, 1), :], sem)

    def start(r, c):
        row_copy(r).start()
        return c

    def wait(r, c):
        row_copy(r).wait()
        return c

    lax.fori_loop(0, tm, start, 0, unroll=8)
    lax.fori_loop(0, tm, wait, 0, unroll=8)


def _scatter(cnt, rows, code, tm):
    n = rows.shape[0]
    return pl.pallas_call(
        _scatter_kernel,
        grid_spec=pltpu.PrefetchScalarGridSpec(
            num_scalar_prefetch=1,
            grid=(n // tm,),
            in_specs=[
                pl.BlockSpec((tm, ROW_W), lambda i, c: (i, 0)),
                pl.BlockSpec((1, 1, tm), lambda i, c: (i, 0, 0), memory_space=pltpu.SMEM),
            ],
            out_specs=pl.BlockSpec(memory_space=pl.ANY),
            scratch_shapes=[pltpu.VMEM((EXPERT_TILE, ROW_W), F32), pltpu.SMEM((N_GROUPS,), jnp.int32),
                            pltpu.SemaphoreType.DMA(()), pltpu.SemaphoreType.DMA(())],
        ),
        out_shape=jax.ShapeDtypeStruct((_sorted_tiles(n) * EXPERT_TILE, ROW_W), F32),
        compiler_params=pltpu.CompilerParams(
            dimension_semantics=("arbitrary",), vmem_limit_bytes=VMEM_LIMIT),
        name="scatter",
    )(cnt, rows, code)


def _tile_of(t, cnt_ref):
    s1, s2, s3, tot = _group_tiles(cnt_ref)
    tc = jnp.minimum(t, tot - 1)
    g = (tc >= s1).astype(jnp.int32) + (tc >= s2).astype(jnp.int32) + (tc >= s3).astype(jnp.int32)
    first = jnp.where(g == 0, 0, jnp.where(g == 1, s1, jnp.where(g == 2, s2, s3)))
    cnt = jnp.where(g == 0, cnt_ref[0], jnp.where(g == 1, cnt_ref[1], jnp.where(g == 2, cnt_ref[2], cnt_ref[3])))
    return tc, g, jnp.where(t < tot, cnt - (tc - first) * EXPERT_TILE, 0)


def _experts_kernel(cnt_ref, xs_ref, wgu_ref, wdn_ref, ys_ref):
    _, _, valid = _tile_of(pl.program_id(0), cnt_ref)

    @pl.when(valid <= 0)
    def _():
        ys_ref[...] = jnp.zeros_like(ys_ref)

    @pl.when(valid > 0)
    def _():
        xb = xs_ref[:, 0:D_MODEL].astype(BF16)
        wts = xs_ref[:, D_MODEL:]
        tot = jnp.zeros((EXPERT_TILE, D_MODEL), F32)
        for e in range(EPG):
            gu = _dot(xb, wgu_ref[0, e])
            gate = gu[:, :D_EXPERT]
            hmid = gate * jax.nn.sigmoid(gate) * gu[:, D_EXPERT:] * wts[:, e:e + 1]
            tot = tot + _dot(hmid.astype(BF16), wdn_ref[0, e])
        ys_ref[...] = tot


def _experts(cnt, xs, wgu, wdn):
    tiles = xs.shape[0] // EXPERT_TILE
    grp = lambda t, c: (_tile_of(t, c)[1], 0, 0, 0)
    return pl.pallas_call(
        _experts_kernel,
        grid_spec=pltpu.PrefetchScalarGridSpec(
            num_scalar_prefetch=1,
            grid=(tiles,),
            in_specs=[
                pl.BlockSpec((EXPERT_TILE, ROW_W), lambda t, c: (_tile_of(t, c)[0], 0)),
                pl.BlockSpec((1, EPG, D_MODEL, 2 * D_EXPERT), grp),
                pl.BlockSpec((1, EPG, D_EXPERT, D_MODEL), grp),
            ],
            out_specs=pl.BlockSpec((EXPERT_TILE, D_MODEL), lambda t, c: (t, 0)),
        ),
        out_shape=jax.ShapeDtypeStruct((tiles * EXPERT_TILE, D_MODEL), F32),
        compiler_params=pltpu.CompilerParams(
            dimension_semantics=("arbitrary",), vmem_limit_bytes=VMEM_LIMIT),
        name="experts",
    )(cnt, xs, wgu, wdn)


def _gather_kernel(cnt_ref, x2_ref, code_ref, gn_ref, ys_ref, y_ref, buf, offs, sem):
    tm = x2_ref.shape[0]
    _store_group_rows(cnt_ref, offs)

    def row_copy(r):
        slot = _slot(code_ref[0, 0, r], offs)
        return pltpu.make_async_copy(ys_ref.at[pl.ds(slot, 1), :], buf.at[pl.ds(r, 1), :], sem)

    def start(r, c):
        row_copy(r).start()
        return c

    def wait(r, c):
        row_copy(r).wait()
        return c

    lax.fori_loop(0, tm, start, 0, unroll=8)
    lax.fori_loop(0, tm, wait, 0, unroll=8)
    y_ref[...] = _rms(x2_ref[...] + buf[...], gn_ref[...])


def _gather(cnt, x2, code, gn, ys, tm):
    n = x2.shape[0]
    return pl.pallas_call(
        _gather_kernel,
        grid_spec=pltpu.PrefetchScalarGridSpec(
            num_scalar_prefetch=1,
            grid=(n // tm,),
            in_specs=[
                pl.BlockSpec((tm, D_MODEL), lambda i, c: (i, 0)),
                pl.BlockSpec((1, 1, tm), lambda i, c: (i, 0, 0), memory_space=pltpu.SMEM),
                pl.BlockSpec((1, D_MODEL), lambda i, c: (0, 0)),
                pl.BlockSpec(memory_space=pl.ANY),
            ],
            out_specs=pl.BlockSpec((tm, D_MODEL), lambda i, c: (i, 0)),
            scratch_shapes=[pltpu.VMEM((tm, D_MODEL), F32), pltpu.SMEM((N_GROUPS,), jnp.int32),
                            pltpu.SemaphoreType.DMA(())],
        ),
        out_shape=jax.ShapeDtypeStruct((n, D_MODEL), F32),
        compiler_params=pltpu.CompilerParams(
            dimension_semantics=("arbitrary",), vmem_limit_bytes=VMEM_LIMIT),
        name="gather",
    )(cnt, x2, code, gn, ys)


def _forget_lanes(f):
    z = lambda n: jnp.zeros(f.shape[:-1] + (n,), f.dtype)
    return jnp.concatenate([f, z(F_HI - HEADS), f, f, f, z(LANES - F_LO - HEADS)], axis=-1)


def _trunk(x, attend, state, w, tm_proj, tm_tok):
    b, s, _ = x.shape
    k, v, lf, qa, ka, mb, ncv = _proj(x, state, w["gm"], w["w_all"], w["bias_f"], w["cw"], w["gc"], tm_proj)
    o = attend(qa, ka, v)
    n = b * s
    x2, rows, code, cnt = _mix(x.reshape(n, D_MODEL), o.reshape(n, ATTN_W), mb.reshape(n, CONV_W),
                               w["wo"], w["ga"], w["gf"], w["wr"], w["br"], tm_tok)
    cnt = cnt[:, 0]
    ys = _experts(cnt, _scatter(cnt, rows, code, tm_tok), w["wgu"], w["wdn"])
    y = _gather(cnt, x2, code, w["gn"], ys, tm_tok)
    return (y.reshape(b, s, D_MODEL), k.reshape(1, b, s, HEADS, HEAD_DIM), v.reshape(1, b, s, HEADS, HEAD_DIM),
            lf.reshape(1, b, s, HEADS), ncv.reshape(1, b, 2, CONV_W))


def kernel(x_prompt, x_sample, cache_k, cache_v, cache_logf, state_conv, norm_mix_g, w_in, b_forget, conv_w, norm_attn_g, norm_conv_g, w_out, norm_ffn_g, w_router_group, b_router_group, w_router_expert, b_router_expert, w_expert_gate_up, w_expert_down, norm_final_g):
    assert w_in.shape[0] == 1, "single-layer trunk"
    win = w_in[0]
    a3 = 3 * ATTN_W
    w_all = jnp.concatenate(
        [win[:, :a3], _forget_lanes(win[:, a3:a3 + HEADS]), win[:, a3 + HEADS:]], axis=1).astype(BF16)
    wr = jnp.concatenate(
        [w_router_group[0], jnp.zeros((D_MODEL, EPG - N_GROUPS), F32),
         w_router_expert[0].transpose(1, 0, 2).reshape(D_MODEL, N_GROUPS * EPG),
         jnp.zeros((D_MODEL, LANES - EPG - N_GROUPS * EPG), F32)], axis=1).astype(BF16)
    br = jnp.concatenate(
        [b_router_group[0], jnp.zeros((EPG - N_GROUPS,), F32), b_router_expert[0].reshape(-1),
         jnp.zeros((LANES - EPG - N_GROUPS * EPG,), F32)]).reshape(1, LANES)
    w = dict(
        gm=norm_mix_g[0].reshape(1, D_MODEL), w_all=w_all, bias_f=_forget_lanes(b_forget[0]).reshape(1, LANES),
        cw=jnp.concatenate([conv_w[0], jnp.zeros((8 - conv_w.shape[1], CONV_W), F32)], axis=0),
        gc=norm_conv_g[0].reshape(1, CONV_W), wo=w_out[0].astype(BF16), ga=norm_attn_g[0].reshape(1, ATTN_W),
        gf=norm_ffn_g[0].reshape(1, D_MODEL), wr=wr, br=br,
        wgu=w_expert_gate_up[0].astype(BF16), wdn=w_expert_down[0].astype(BF16),
        gn=norm_final_g.reshape(1, D_MODEL))

    bp, sp, _ = x_prompt.shape
    bs, ss, _ = x_sample.shape
    plen = cache_k.shape[2]
    yp, kp, vp, lfp, cvp = _trunk(x_prompt, _attention, jnp.zeros((bp, 8, CONV_W), F32), w, 512, 512)

    ck = cache_k[0].reshape(bs, plen, ATTN_W)
    cv = cache_v[0].reshape(bs, plen, ATTN_W)
    clf = _forget_lanes(cache_logf[0])
    st = jnp.concatenate([jnp.zeros((bs, 6, CONV_W), F32), state_conv[0]], axis=1)
    attend_s = lambda qa, ka, v: _attention_cache(qa, ka, v, ck, cv, clf)
    ys, ks, vs, lfs, cvs = _trunk(x_sample, attend_s, st, w, ss, bs * ss)
    return (yp, ys, kp, vp, lfp, cvp, ks, vs, lfs, cvs)
```

```python
import functools
import math

import jax
import jax.numpy as jnp
from jax import lax
from jax.experimental import pallas as pl
from jax.experimental.pallas import tpu as pltpu

F32 = jnp.float32
BF16 = jnp.bfloat16

D_MODEL = 1024
HEADS = 8
HEAD_DIM = 64
ATTN_W = HEADS * HEAD_DIM
CONV_W = 512
N_GROUPS = 4
EPG = 8
D_EXPERT = 256
EPS = 1e-6
LOG2E = math.log2(math.e)
Q_SCALE = HEAD_DIM ** -0.5 * LOG2E

LANES = 128
COL_Q, COL_K, COL_V, COL_F, COL_BG, COL_CG, COL_H = 0, 512, 1024, 1536, 1664, 2176, 2688
PROJ_PAD = 3200
F_OUT, F_HI, F_MID, F_LO = 0, 64, 72, 80

GROUP_SHIFT = 20
ROW_W = D_MODEL + LANES
TILE_SHIFT = 8
EXPERT_TILE = 1 << TILE_SHIFT

VMEM_LIMIT = 56 * 1024 * 1024


def _dot(a, b):
    return jnp.dot(a, b, preferred_element_type=F32)


def _dot_nt(a, b):
    return lax.dot_general(a, b, (((1,), (1,)), ((), ())), preferred_element_type=F32)


def _rms(x, g):
    return x * lax.rsqrt(jnp.mean(x * x, axis=-1, keepdims=True) + EPS) * g


def _log_sigmoid(x):
    return jnp.minimum(x, 0.0) - jnp.log(1.0 + jnp.exp(-jnp.abs(x)))


def _split3(x):
    a1 = x.astype(BF16)
    r1 = x - a1.astype(F32)
    a2 = r1.astype(BF16)
    a3 = (r1 - a2.astype(F32)).astype(BF16)
    return a1, a2, a3


def _cumsum_rows(x):
    n = x.shape[0]
    r = lax.broadcasted_iota(jnp.int32, (n, n), 0)
    c = lax.broadcasted_iota(jnp.int32, (n, n), 1)
    tri = jnp.where(r >= c, 1.0, 0.0).astype(BF16)
    parts = _dot(tri, jnp.concatenate(_split3(x), axis=1))
    return parts[:, :LANES] + parts[:, LANES:2 * LANES] + parts[:, 2 * LANES:]


def _key_bias_lanes(cneg):
    lane = lax.broadcasted_iota(jnp.int32, cneg.shape, 1)
    hi = cneg.astype(BF16).astype(F32)
    r1 = cneg - hi
    mid = r1.astype(BF16).astype(F32)
    lo = r1 - mid
    out = jnp.where(lane < F_MID, hi, jnp.where(lane < F_LO, mid, lo))
    return jnp.where((lane >= F_HI) & (lane < F_LO + HEADS), out, 0.0)


def _head_block(x, h):
    blk = x[:, (h // 2) * LANES:(h // 2 + 1) * LANES]
    return pltpu.roll(blk, HEAD_DIM, 1) if h % 2 else blk


def _key_operand(k, bias_lanes, h):
    lane = lax.broadcasted_iota(jnp.int32, bias_lanes.shape, 1)
    return jnp.where(lane < HEAD_DIM, _head_block(k, h), bias_lanes).astype(BF16)


def _proj_kernel(x_ref, st_ref, gm_ref, w_ref, bf_ref, cw_ref, gc_ref,
                 k_ref, v_ref, lf_ref, qa_ref, ka_ref, mb_ref, ncv_ref, ubuf, ccar):
    i = pl.program_id(1)
    tm = x_ref.shape[1]
    xn = _rms(x_ref[0], gm_ref[...]).astype(BF16)
    p = _dot(xn, w_ref[...])
    q = p[:, COL_Q:COL_Q + ATTN_W]
    k = p[:, COL_K:COL_K + ATTN_W]
    k_ref[0] = k
    v_ref[0] = p[:, COL_V:COL_V + ATTN_W]

    lfb = _log_sigmoid(p[:, COL_F:COL_F + LANES] + bf_ref[...])
    lf_ref[0] = lfb[:, F_OUT:F_OUT + HEADS]

    @pl.when(i == 0)
    def _():
        ccar[...] = jnp.zeros_like(ccar)

    c = _cumsum_rows(lfb) + ccar[0:1, :]
    ccar[...] = jnp.broadcast_to(c[tm - 1:tm, :], ccar.shape)
    bias_lanes = _key_bias_lanes(c * (-LOG2E))
    lane = lax.broadcasted_iota(jnp.int32, (tm, LANES), 1)
    for h in range(HEADS):
        ka_ref[0, h] = _key_operand(k, bias_lanes, h)
        sel = jnp.where((lane == F_HI + h) | (lane == F_MID + h) | (lane == F_LO + h), 1.0, 0.0)
        qa_ref[0, h] = jnp.where(lane < HEAD_DIM, _head_block(q, h) * Q_SCALE, sel).astype(BF16)

    u = p[:, COL_CG:COL_CG + CONV_W] * p[:, COL_H:COL_H + CONV_W]

    @pl.when(i == 0)
    def _():
        ubuf[0:8, :] = st_ref[0]

    @pl.when(i > 0)
    def _():
        ubuf[0:8, :] = ubuf[tm:tm + 8, :]

    ubuf[8:8 + tm, :] = u
    y = cw_ref[0:1, :] * ubuf[6:6 + tm, :] + cw_ref[1:2, :] * ubuf[7:7 + tm, :] + cw_ref[2:3, :] * u
    ob = p[:, COL_BG:COL_BG + CONV_W] * y
    mb_ref[0] = _rms(ob, gc_ref[...]).astype(BF16)
    ncv_ref[0] = u[tm - 2:tm, :]


def _proj(x, state, gm, w_all, bias_f, cw, gc, tm):
    b, s, _ = x.shape
    full = lambda shape: pl.BlockSpec(shape, lambda bi, i: (0,) * len(shape))
    return pl.pallas_call(
        _proj_kernel,
        grid=(b, s // tm),
        in_specs=[
            pl.BlockSpec((1, tm, D_MODEL), lambda bi, i: (bi, i, 0)),
            pl.BlockSpec((1, 8, CONV_W), lambda bi, i: (bi, 0, 0)),
            full((1, D_MODEL)),
            full((D_MODEL, PROJ_PAD)),
            full((1, LANES)),
            full((8, CONV_W)),
            full((1, CONV_W)),
        ],
        out_specs=[
            pl.BlockSpec((1, tm, ATTN_W), lambda bi, i: (bi, i, 0)),
            pl.BlockSpec((1, tm, ATTN_W), lambda bi, i: (bi, i, 0)),
            pl.BlockSpec((1, tm, HEADS), lambda bi, i: (bi, i, 0)),
            pl.BlockSpec((1, HEADS, tm, LANES), lambda bi, i: (bi, 0, i, 0)),
            pl.BlockSpec((1, HEADS, tm, LANES), lambda bi, i: (bi, 0, i, 0)),
            pl.BlockSpec((1, tm, CONV_W), lambda bi, i: (bi, i, 0)),
            pl.BlockSpec((1, 2, CONV_W), lambda bi, i: (bi, 0, 0)),
        ],
        out_shape=[
            jax.ShapeDtypeStruct((b, s, ATTN_W), F32),
            jax.ShapeDtypeStruct((b, s, ATTN_W), F32),
            jax.ShapeDtypeStruct((b, s, HEADS), F32),
            jax.ShapeDtypeStruct((b, HEADS, s, LANES), BF16),
            jax.ShapeDtypeStruct((b, HEADS, s, LANES), BF16),
            jax.ShapeDtypeStruct((b, s, CONV_W), BF16),
            jax.ShapeDtypeStruct((b, 2, CONV_W), F32),
        ],
        scratch_shapes=[pltpu.VMEM((tm + 8, CONV_W), F32), pltpu.VMEM((8, LANES), F32)],
        compiler_params=pltpu.CompilerParams(
            dimension_semantics=("arbitrary", "arbitrary"), vmem_limit_bytes=VMEM_LIMIT),
        name="proj",
    )(x, state, gm, w_all, bias_f, cw, gc)


ATT_BLK = 256


def _attn_kernel(qa_ref, ka_ref, v_ref, o_ref, vt):
    iq = pl.program_id(2)
    blk = ATT_BLK

    @pl.when(iq == 0)
    def _():
        vt[...] = v_ref[0].T.astype(BF16)

    r = lax.broadcasted_iota(jnp.int32, (blk, blk), 0)
    c = lax.broadcasted_iota(jnp.int32, (blk, blk), 1)
    causal = r <= c

    def run(nkv):
        noff = (nkv - 1) * blk
        outs = []
        for hh in range(2):
            q = qa_ref[0, hh]
            rows = slice(hh * HEAD_DIM, (hh + 1) * HEAD_DIM)
            s_d = jnp.where(causal, _dot_nt(ka_ref[0, hh, noff:noff + blk, :], q), -jnp.inf)
            m = jnp.max(s_d, axis=0, keepdims=True)
            if nkv > 1:
                s_o = _dot_nt(ka_ref[0, hh, 0:noff, :], q)
                m = jnp.maximum(m, jnp.max(s_o, axis=0, keepdims=True))
            p_d = jnp.exp2(s_d - m)
            l = jnp.sum(p_d, axis=0, keepdims=True)
            acc = _dot(vt[rows, noff:noff + blk], p_d.astype(BF16))
            if nkv > 1:
                p_o = jnp.exp2(s_o - m)
                l = l + jnp.sum(p_o, axis=0, keepdims=True)
                acc = acc + _dot(vt[rows, 0:noff], p_o.astype(BF16))
            outs.append(acc / l)
        o_ref[0] = jnp.concatenate(outs, axis=0).T

    for nkv in range(1, ka_ref.shape[2] // blk + 1):
        pl.when(iq == nkv - 1)(functools.partial(run, nkv))


def _attention(qa, ka, v):
    b, _, s, _ = qa.shape
    return pl.pallas_call(
        _attn_kernel,
        grid=(b, HEADS // 2, s // ATT_BLK),
        in_specs=[
            pl.BlockSpec((1, 2, ATT_BLK, LANES), lambda bi, p, iq: (bi, p, iq, 0)),
            pl.BlockSpec((1, 2, s, LANES), lambda bi, p, iq: (bi, p, 0, 0)),
            pl.BlockSpec((1, s, LANES), lambda bi, p, iq: (bi, 0, p)),
        ],
        out_specs=pl.BlockSpec((1, ATT_BLK, LANES), lambda bi, p, iq: (bi, iq, p)),
        out_shape=jax.ShapeDtypeStruct((b, s, ATTN_W), F32),
        scratch_shapes=[pltpu.VMEM((LANES, s), BF16)],
        compiler_params=pltpu.CompilerParams(
            dimension_semantics=("arbitrary", "arbitrary", "arbitrary"), vmem_limit_bytes=VMEM_LIMIT),
        name="attention",
    )(qa, ka, v)


Q_PAD = 128


def _pad_rows(x, n):
    return jnp.concatenate([x, jnp.zeros((n - x.shape[0], x.shape[1]), x.dtype)], axis=0)


def _attn_cache_kernel(qa_ref, ka_ref, v_ref, ck_ref, cv_ref, clf_ref, o_ref):
    t = qa_ref.shape[2]
    plen = ck_ref.shape[1]
    cpast = _cumsum_rows(clf_ref[0])
    cpast = cpast - cpast[plen - 1:plen, :]
    bias_lanes = _key_bias_lanes(cpast * (-LOG2E))
    ck = ck_ref[0]
    r = lax.broadcasted_iota(jnp.int32, (Q_PAD, Q_PAD), 0)
    c = lax.broadcasted_iota(jnp.int32, (Q_PAD, Q_PAD), 1)
    new_ok = (r <= c) & (r < t)

    pairs = []
    for p in range(HEADS // 2):
        vct = cv_ref[0][:, p * LANES:(p + 1) * LANES].T.astype(BF16)
        vnt = _pad_rows(v_ref[0][:, p * LANES:(p + 1) * LANES], Q_PAD).T.astype(BF16)
        outs = []
        for hh in range(2):
            h = 2 * p + hh
            q = _pad_rows(qa_ref[0, h], Q_PAD)
            s_past = _dot_nt(_key_operand(ck, bias_lanes, h), q)
            s_new = jnp.where(new_ok, _dot_nt(_pad_rows(ka_ref[0, h], Q_PAD), q), -jnp.inf)
            m = jnp.maximum(jnp.max(s_past, axis=0, keepdims=True), jnp.max(s_new, axis=0, keepdims=True))
            p_past = jnp.exp2(s_past - m)
            p_new = jnp.exp2(s_new - m)
            l = jnp.sum(p_past, axis=0, keepdims=True) + jnp.sum(p_new, axis=0, keepdims=True)
            rows = slice(hh * HEAD_DIM, (hh + 1) * HEAD_DIM)
            acc = _dot(vct[rows], p_past.astype(BF16)) + _dot(vnt[rows], p_new.astype(BF16))
            outs.append(acc / l)
        pairs.append(jnp.concatenate(outs, axis=0).T[0:t, :])
    o_ref[0] = jnp.concatenate(pairs, axis=1)


def _attention_cache(qa, ka, v, ck, cv, clf):
    b, _, t, _ = qa.shape
    plen = ck.shape[1]
    return pl.pallas_call(
        _attn_cache_kernel,
        grid=(b,),
        in_specs=[
            pl.BlockSpec((1, HEADS, t, LANES), lambda bi: (bi, 0, 0, 0)),
            pl.BlockSpec((1, HEADS, t, LANES), lambda bi: (bi, 0, 0, 0)),
            pl.BlockSpec((1, t, ATTN_W), lambda bi: (bi, 0, 0)),
            pl.BlockSpec((1, plen, ATTN_W), lambda bi: (bi, 0, 0)),
            pl.BlockSpec((1, plen, ATTN_W), lambda bi: (bi, 0, 0)),
            pl.BlockSpec((1, plen, LANES), lambda bi: (bi, 0, 0)),
        ],
        out_specs=pl.BlockSpec((1, t, ATTN_W), lambda bi: (bi, 0, 0)),
        out_shape=jax.ShapeDtypeStruct((b, t, ATTN_W), F32),
        compiler_params=pltpu.CompilerParams(
            dimension_semantics=("arbitrary",), vmem_limit_bytes=VMEM_LIMIT),
        name="attention_cache",
    )(qa, ka, v, ck, cv, clf)


def _mix_kernel(x_ref, o_ref, mb_ref, wo_ref, ga_ref, gf_ref, wr_ref, br_ref,
                x2_ref, rows_ref, slot_ref, cnt_ref, carry):
    i = pl.program_id(0)
    tm = x_ref.shape[0]
    oa = _rms(o_ref[...], ga_ref[...]).astype(BF16)
    merged = jnp.concatenate([oa, mb_ref[...]], axis=1)
    x2 = x_ref[...] + _dot(merged, wo_ref[...])
    x2_ref[...] = x2
    xb = _rms(x2, gf_ref[...]).astype(BF16)
    rows_ref[:, 0:D_MODEL] = xb.astype(F32)

    lt = (_dot(xb, wr_ref[...]) + br_ref[...]).T
    row = lax.broadcasted_iota(jnp.int32, (EPG, tm), 0)
    lg = jnp.where(row < N_GROUPS, lt[0:EPG], -jnp.inf)
    eg = jnp.exp(lg - jnp.max(lg, axis=0, keepdims=True))
    pg = eg / jnp.sum(eg, axis=0, keepdims=True)
    pg_top = jnp.max(pg, axis=0, keepdims=True)
    g_top = jnp.min(jnp.where(pg == pg_top, row, EPG), axis=0, keepdims=True)

    le = lt[EPG:2 * EPG]
    for g in range(1, N_GROUPS):
        le = jnp.where(g_top == g, lt[EPG * (g + 1):EPG * (g + 2)], le)
    ee = jnp.exp(le - jnp.max(le, axis=0, keepdims=True))
    pe = ee / jnp.sum(ee, axis=0, keepdims=True)
    v1 = jnp.max(pe, axis=0, keepdims=True)
    i1 = jnp.min(jnp.where(pe == v1, row, EPG), axis=0, keepdims=True)
    pe2 = jnp.where(row == i1, -1.0, pe)
    v2 = jnp.max(pe2, axis=0, keepdims=True)
    i2 = jnp.min(jnp.where(pe2 == v2, row, EPG), axis=0, keepdims=True)
    den = v1 + v2
    wg = jnp.where(row == i1, v1 / den * pg_top, 0.0) + jnp.where(row == i2, v2 / den * pg_top, 0.0)
    rows_ref[:, D_MODEL:] = jnp.concatenate([wg, jnp.zeros((LANES - EPG, tm), F32)], axis=0).T

    @pl.when(i == 0)
    def _():
        carry[...] = jnp.zeros_like(carry)

    onehot = jnp.where(row == g_top, 1.0, 0.0)
    r2 = lax.broadcasted_iota(jnp.int32, (tm, tm), 0)
    c2 = lax.broadcasted_iota(jnp.int32, (tm, tm), 1)
    earlier = jnp.where(r2 < c2, 1.0, 0.0).astype(BF16)
    base = carry[:, 0:1]
    rank = jnp.sum(onehot * (_dot(onehot.astype(BF16), earlier) + base), axis=0, keepdims=True)
    slot_ref[i] = g_top * (1 << GROUP_SHIFT) + rank.astype(jnp.int32)
    total = base + jnp.sum(onehot, axis=1, keepdims=True)
    carry[...] = jnp.broadcast_to(total, carry.shape)
    cnt_ref[...] = carry[...].astype(jnp.int32)

    @pl.when(i == pl.num_programs(0) - 1)
    def _():
        padded = jnp.floor((total + (EXPERT_TILE - 1)) * (1.0 / EXPERT_TILE)) * EXPERT_TILE
        o1 = padded[0:1]
        o2 = o1 + padded[1:2]
        o3 = o2 + padded[2:3]
        code = slot_ref[...]
        g = code >> GROUP_SHIFT
        first = jnp.where(g == 0, 0.0, jnp.where(g == 1, o1, jnp.where(g == 2, o2, o3)))
        slot_ref[...] = first.astype(jnp.int32) + (code & ((1 << GROUP_SHIFT) - 1))


def _mix(x, o, mb, wo, ga, gf, wr, br, tm):
    n = x.shape[0]
    assert n < (1 << GROUP_SHIFT)
    full = lambda shape: pl.BlockSpec(shape, lambda i: (0,) * len(shape))
    return pl.pallas_call(
        _mix_kernel,
        grid=(n // tm,),
        in_specs=[
            pl.BlockSpec((tm, D_MODEL), lambda i: (i, 0)),
            pl.BlockSpec((tm, ATTN_W), lambda i: (i, 0)),
            pl.BlockSpec((tm, CONV_W), lambda i: (i, 0)),
            full((D_MODEL, D_MODEL)),
            full((1, ATTN_W)),
            full((1, D_MODEL)),
            full((D_MODEL, LANES)),
            full((1, LANES)),
        ],
        out_specs=[
            pl.BlockSpec((tm, D_MODEL), lambda i: (i, 0)),
            pl.BlockSpec((tm, ROW_W), lambda i: (i, 0)),
            full((n // tm, 1, tm)),
            full((EPG, LANES)),
        ],
        out_shape=[
            jax.ShapeDtypeStruct((n, D_MODEL), F32),
            jax.ShapeDtypeStruct((n, ROW_W), F32),
            jax.ShapeDtypeStruct((n // tm, 1, tm), jnp.int32),
            jax.ShapeDtypeStruct((EPG, LANES), jnp.int32),
        ],
        scratch_shapes=[pltpu.VMEM((EPG, LANES), F32)],
        compiler_params=pltpu.CompilerParams(
            dimension_semantics=("arbitrary",), vmem_limit_bytes=VMEM_LIMIT),
        name="mix",
    )(x, o, mb, wo, ga, gf, wr, br)


def _sorted_tiles(n):
    return n // EXPERT_TILE + N_GROUPS


def _group_tiles(cnt_ref):
    nt = [(cnt_ref[g] + (EXPERT_TILE - 1)) >> TILE_SHIFT for g in range(N_GROUPS)]
    s1 = nt[0]
    s2 = s1 + nt[1]
    s3 = s2 + nt[2]
    return s1, s2, s3, s3 + nt[3]


def _row_copies(n, copy_of_row):
    for r in range(n):
        copy_of_row(r).start(priority=r % 2)
    for r in range(n):
        copy_of_row(r).wait()


def _scatter_kernel(cnt_ref, rows_ref, slot_ref, xs_ref, zeros, sem, zsem):
    tm = rows_ref.shape[0]

    @pl.when(pl.program_id(0) == 0)
    def _():
        zeros[...] = jnp.zeros_like(zeros)
        s1, s2, s3, tot = _group_tiles(cnt_ref)
        ends = (s1, s2, s3, tot)

        def fill(tile):
            return pltpu.make_async_copy(zeros, xs_ref.at[pl.ds(tile * EXPERT_TILE, EXPERT_TILE), :], zsem)

        for g in range(N_GROUPS):
            partial = (cnt_ref[g] & (EXPERT_TILE - 1)) != 0

            @pl.when(partial)
            def _():
                fill(ends[g] - 1).start()
                fill(ends[g] - 1).wait()

        def fill_tail(tile, c):
            fill(tile).start()
            fill(tile).wait()
            return c

        lax.fori_loop(tot, xs_ref.shape[0] // EXPERT_TILE, fill_tail, 0)

    _row_copies(tm, lambda r: pltpu.make_async_copy(
        rows_ref.at[pl.ds(r, 1), :], xs_ref.at[pl.ds(slot_ref[0, 0, r], 1), :], sem))


def _scatter(cnt, rows, slots, tm):
    n = rows.shape[0]
    return pl.pallas_call(
        _scatter_kernel,
        grid_spec=pltpu.PrefetchScalarGridSpec(
            num_scalar_prefetch=1,
            grid=(n // tm,),
            in_specs=[
                pl.BlockSpec((tm, ROW_W), lambda i, c: (i, 0)),
                pl.BlockSpec((1, 1, tm), lambda i, c: (i, 0, 0), memory_space=pltpu.SMEM),
            ],
            out_specs=pl.BlockSpec(memory_space=pl.ANY),
            scratch_shapes=[pltpu.VMEM((EXPERT_TILE, ROW_W), F32),
                            pltpu.SemaphoreType.DMA(()), pltpu.SemaphoreType.DMA(())],
        ),
        out_shape=jax.ShapeDtypeStruct((_sorted_tiles(n) * EXPERT_TILE, ROW_W), F32),
        compiler_params=pltpu.CompilerParams(
            dimension_semantics=("arbitrary",), vmem_limit_bytes=VMEM_LIMIT),
        name="scatter",
    )(cnt, rows, slots)


def _tile_of(t, cnt_ref):
    s1, s2, s3, tot = _group_tiles(cnt_ref)
    tc = jnp.minimum(t, tot - 1)
    g = (tc >= s1).astype(jnp.int32) + (tc >= s2).astype(jnp.int32) + (tc >= s3).astype(jnp.int32)
    first = jnp.where(g == 0, 0, jnp.where(g == 1, s1, jnp.where(g == 2, s2, s3)))
    cnt = jnp.where(g == 0, cnt_ref[0], jnp.where(g == 1, cnt_ref[1], jnp.where(g == 2, cnt_ref[2], cnt_ref[3])))
    return tc, g, jnp.where(t < tot, cnt - (tc - first) * EXPERT_TILE, 0)


def _experts_kernel(cnt_ref, xs_ref, wgu_ref, wdn_ref, ys_ref):
    _, _, valid = _tile_of(pl.program_id(0), cnt_ref)

    @pl.when(valid <= 0)
    def _():
        ys_ref[...] = jnp.zeros_like(ys_ref)

    @pl.when(valid > 0)
    def _():
        xb = xs_ref[:, 0:D_MODEL].astype(BF16)
        wts = xs_ref[:, D_MODEL:]
        tot = jnp.zeros((EXPERT_TILE, D_MODEL), F32)
        for e in range(EPG):
            gu = _dot(xb, wgu_ref[0, e])
            gate = gu[:, :D_EXPERT]
            hmid = gate * jax.nn.sigmoid(gate) * gu[:, D_EXPERT:] * wts[:, e:e + 1]
            tot = tot + _dot(hmid.astype(BF16), wdn_ref[0, e])
        ys_ref[...] = tot


def _experts(cnt, xs, wgu, wdn):
    tiles = xs.shape[0] // EXPERT_TILE
    grp = lambda t, c: (_tile_of(t, c)[1], 0, 0, 0)
    return pl.pallas_call(
        _experts_kernel,
        grid_spec=pltpu.PrefetchScalarGridSpec(
            num_scalar_prefetch=1,
            grid=(tiles,),
            in_specs=[
                pl.BlockSpec((EXPERT_TILE, ROW_W), lambda t, c: (_tile_of(t, c)[0], 0)),
                pl.BlockSpec((1, EPG, D_MODEL, 2 * D_EXPERT), grp),
                pl.BlockSpec((1, EPG, D_EXPERT, D_MODEL), grp),
            ],
            out_specs=pl.BlockSpec((EXPERT_TILE, D_MODEL), lambda t, c: (t, 0)),
        ),
        out_shape=jax.ShapeDtypeStruct((tiles * EXPERT_TILE, D_MODEL), F32),
        compiler_params=pltpu.CompilerParams(
            dimension_semantics=("arbitrary",), vmem_limit_bytes=VMEM_LIMIT),
        name="experts",
    )(cnt, xs, wgu, wdn)


def _gather_kernel(x2_ref, slot_ref, gn_ref, ys_ref, y_ref, buf, sem):
    tm = x2_ref.shape[0]
    _row_copies(tm, lambda r: pltpu.make_async_copy(
        ys_ref.at[pl.ds(slot_ref[0, 0, r], 1), :], buf.at[pl.ds(r, 1), :], sem))
    y_ref[...] = _rms(x2_ref[...] + buf[...], gn_ref[...])


def _gather(x2, slots, gn, ys, tm):
    n = x2.shape[0]
    return pl.pallas_call(
        _gather_kernel,
        grid=(n // tm,),
        in_specs=[
            pl.BlockSpec((tm, D_MODEL), lambda i: (i, 0)),
            pl.BlockSpec((1, 1, tm), lambda i: (i, 0, 0), memory_space=pltpu.SMEM),
            pl.BlockSpec((1, D_MODEL), lambda i: (0, 0)),
            pl.BlockSpec(memory_space=pl.ANY),
        ],
        out_specs=pl.BlockSpec((tm, D_MODEL), lambda i: (i, 0)),
        out_shape=jax.ShapeDtypeStruct((n, D_MODEL), F32),
        scratch_shapes=[pltpu.VMEM((tm, D_MODEL), F32), pltpu.SemaphoreType.DMA(())],
        compiler_params=pltpu.CompilerParams(
            dimension_semantics=("arbitrary",), vmem_limit_bytes=VMEM_LIMIT),
        name="gather",
    )(x2, slots, gn, ys)


def _forget_lanes(f):
    z = lambda n: jnp.zeros(f.shape[:-1] + (n,), f.dtype)
    return jnp.concatenate([f, z(F_HI - HEADS), f, f, f, z(LANES - F_LO - HEADS)], axis=-1)


def _trunk(x, attend, state, w, tm_proj, tm_tok):
    b, s, _ = x.shape
    k, v, lf, qa, ka, mb, ncv = _proj(x, state, w["gm"], w["w_all"], w["bias_f"], w["cw"], w["gc"], tm_proj)
    o = attend(qa, ka, v)
    n = b * s
    x2, rows, slots, cnt = _mix(x.reshape(n, D_MODEL), o.reshape(n, ATTN_W), mb.reshape(n, CONV_W),
                                w["wo"], w["ga"], w["gf"], w["wr"], w["br"], tm_tok)
    cnt = cnt[:, 0]
    ys = _experts(cnt, _scatter(cnt, rows, slots, tm_tok), w["wgu"], w["wdn"])
    y = _gather(x2, slots, w["gn"], ys, tm_tok)
    return (y.reshape(b, s, D_MODEL), k.reshape(1, b, s, HEADS, HEAD_DIM), v.reshape(1, b, s, HEADS, HEAD_DIM),
            lf.reshape(1, b, s, HEADS), ncv.reshape(1, b, 2, CONV_W))


def kernel(x_prompt, x_sample, cache_k, cache_v, cache_logf, state_conv, norm_mix_g, w_in, b_forget, conv_w, norm_attn_g, norm_conv_g, w_out, norm_ffn_g, w_router_group, b_router_group, w_router_expert, b_router_expert, w_expert_gate_up, w_expert_down, norm_final_g):
    assert w_in.shape[0] == 1, "single-layer trunk"
    win = w_in[0]
    a3 = 3 * ATTN_W
    w_all = jnp.concatenate(
        [win[:, :a3], _forget_lanes(win[:, a3:a3 + HEADS]), win[:, a3 + HEADS:]], axis=1).astype(BF16)
    wr = jnp.concatenate(
        [w_router_group[0], jnp.zeros((D_MODEL, EPG - N_GROUPS), F32),
         w_router_expert[0].transpose(1, 0, 2).reshape(D_MODEL, N_GROUPS * EPG),
         jnp.zeros((D_MODEL, LANES - EPG - N_GROUPS * EPG), F32)], axis=1).astype(BF16)
    br = jnp.concatenate(
        [b_router_group[0], jnp.zeros((EPG - N_GROUPS,), F32), b_router_expert[0].reshape(-1),
         jnp.zeros((LANES - EPG - N_GROUPS * EPG,), F32)]).reshape(1, LANES)
    w = dict(
        gm=norm_mix_g[0].reshape(1, D_MODEL), w_all=w_all, bias_f=_forget_lanes(b_forget[0]).reshape(1, LANES),
        cw=jnp.concatenate([conv_w[0], jnp.zeros((8 - conv_w.shape[1], CONV_W), F32)], axis=0),
        gc=norm_conv_g[0].reshape(1, CONV_W), wo=w_out[0].astype(BF16), ga=norm_attn_g[0].reshape(1, ATTN_W),
        gf=norm_ffn_g[0].reshape(1, D_MODEL), wr=wr, br=br,
        wgu=w_expert_gate_up[0].astype(BF16), wdn=w_expert_down[0].astype(BF16),
        gn=norm_final_g.reshape(1, D_MODEL))

    bp, sp, _ = x_prompt.shape
    bs, ss, _ = x_sample.shape
    plen = cache_k.shape[2]
    yp, kp, vp, lfp, cvp = _trunk(x_prompt, _attention, jnp.zeros((bp, 8, CONV_W), F32), w, 512, 512)

    ck = cache_k[0].reshape(bs, plen, ATTN_W)
    cv = cache_v[0].reshape(bs, plen, ATTN_W)
    clf = _forget_lanes(cache_logf[0])
    st = jnp.concatenate([jnp.zeros((bs, 6, CONV_W), F32), state_conv[0]], axis=1)
    attend_s = lambda qa, ka, v: _attention_cache(qa, ka, v, ck, cv, clf)
    ys, ks, vs, lfs, cvs = _trunk(x_sample, attend_s, st, w, ss, bs * ss)
    return (yp, ys, kp, vp, lfp, cvp, ks, vs, lfs, cvs)
```

```python
import functools
import math

import jax
import jax.numpy as jnp
from jax import lax
from jax.experimental import pallas as pl
from jax.experimental.pallas import tpu as pltpu

F32 = jnp.float32
BF16 = jnp.bfloat16

D_MODEL = 1024
HEADS = 8
HEAD_DIM = 64
ATTN_W = HEADS * HEAD_DIM
CONV_W = 512
N_GROUPS = 4
EPG = 8
D_EXPERT = 256
EPS = 1e-6
LOG2E = math.log2(math.e)
Q_SCALE = HEAD_DIM ** -0.5 * LOG2E

LANES = 128
COL_Q, COL_K, COL_V, COL_F, COL_BG, COL_CG, COL_H = 0, 512, 1024, 1536, 1664, 2176, 2688
PROJ_PAD = 3200
F_OUT, F_HI, F_MID, F_LO = 0, 64, 72, 80

GROUP_SHIFT = 20
ROW_W = D_MODEL + LANES
TILE_SHIFT = 8
EXPERT_TILE = 1 << TILE_SHIFT

VMEM_LIMIT = 56 * 1024 * 1024


def _dot(a, b):
    return jnp.dot(a, b, preferred_element_type=F32)


def _dot_nt(a, b):
    return lax.dot_general(a, b, (((1,), (1,)), ((), ())), preferred_element_type=F32)


def _rms(x, g):
    return x * lax.rsqrt(jnp.mean(x * x, axis=-1, keepdims=True) + EPS) * g


def _log_sigmoid(x):
    return jnp.minimum(x, 0.0) - jnp.log(1.0 + jnp.exp(-jnp.abs(x)))


def _split3(x):
    a1 = x.astype(BF16)
    r1 = x - a1.astype(F32)
    a2 = r1.astype(BF16)
    a3 = (r1 - a2.astype(F32)).astype(BF16)
    return a1, a2, a3


def _cumsum_rows(x):
    n = x.shape[0]
    r = lax.broadcasted_iota(jnp.int32, (n, n), 0)
    c = lax.broadcasted_iota(jnp.int32, (n, n), 1)
    tri = jnp.where(r >= c, 1.0, 0.0).astype(BF16)
    parts = _dot(tri, jnp.concatenate(_split3(x), axis=1))
    return parts[:, :LANES] + parts[:, LANES:2 * LANES] + parts[:, 2 * LANES:]


def _key_bias_lanes(cneg):
    lane = lax.broadcasted_iota(jnp.int32, cneg.shape, 1)
    hi = cneg.astype(BF16).astype(F32)
    r1 = cneg - hi
    mid = r1.astype(BF16).astype(F32)
    lo = r1 - mid
    out = jnp.where(lane < F_MID, hi, jnp.where(lane < F_LO, mid, lo))
    return jnp.where((lane >= F_HI) & (lane < F_LO + HEADS), out, 0.0)


def _head_block(x, h):
    blk = x[:, (h // 2) * LANES:(h // 2 + 1) * LANES]
    return pltpu.roll(blk, HEAD_DIM, 1) if h % 2 else blk


def _key_operand(k, bias_lanes, h):
    lane = lax.broadcasted_iota(jnp.int32, bias_lanes.shape, 1)
    return jnp.where(lane < HEAD_DIM, _head_block(k, h), bias_lanes).astype(BF16)


def _proj_kernel(x_ref, st_ref, gm_ref, w_ref, bf_ref, cw_ref, gc_ref,
                 k_ref, v_ref, lf_ref, qa_ref, ka_ref, mb_ref, ncv_ref, ubuf, ccar):
    i = pl.program_id(1)
    tm = x_ref.shape[1]
    xn = _rms(x_ref[0], gm_ref[...]).astype(BF16)
    p = _dot(xn, w_ref[...])
    q = p[:, COL_Q:COL_Q + ATTN_W]
    k = p[:, COL_K:COL_K + ATTN_W]
    k_ref[0] = k
    v_ref[0] = p[:, COL_V:COL_V + ATTN_W]

    lfb = _log_sigmoid(p[:, COL_F:COL_F + LANES] + bf_ref[...])
    lf_ref[0] = lfb[:, F_OUT:F_OUT + HEADS]

    @pl.when(i == 0)
    def _():
        ccar[...] = jnp.zeros_like(ccar)

    c = _cumsum_rows(lfb) + ccar[0:1, :]
    ccar[...] = jnp.broadcast_to(c[tm - 1:tm, :], ccar.shape)
    bias_lanes = _key_bias_lanes(c * (-LOG2E))
    lane = lax.broadcasted_iota(jnp.int32, (tm, LANES), 1)
    for h in range(HEADS):
        ka_ref[0, h] = _key_operand(k, bias_lanes, h)
        sel = jnp.where((lane == F_HI + h) | (lane == F_MID + h) | (lane == F_LO + h), 1.0, 0.0)
        qa_ref[0, h] = jnp.where(lane < HEAD_DIM, _head_block(q, h) * Q_SCALE, sel).astype(BF16)

    u = p[:, COL_CG:COL_CG + CONV_W] * p[:, COL_H:COL_H + CONV_W]

    @pl.when(i == 0)
    def _():
        ubuf[0:8, :] = st_ref[0]

    @pl.when(i > 0)
    def _():
        ubuf[0:8, :] = ubuf[tm:tm + 8, :]

    ubuf[8:8 + tm, :] = u
    y = cw_ref[0:1, :] * ubuf[6:6 + tm, :] + cw_ref[1:2, :] * ubuf[7:7 + tm, :] + cw_ref[2:3, :] * u
    ob = p[:, COL_BG:COL_BG + CONV_W] * y
    mb_ref[0] = _rms(ob, gc_ref[...]).astype(BF16)
    ncv_ref[0] = u[tm - 2:tm, :]


def _proj(x, state, gm, w_all, bias_f, cw, gc, tm):
    b, s, _ = x.shape
    full = lambda shape: pl.BlockSpec(shape, lambda bi, i: (0,) * len(shape))
    return pl.pallas_call(
        _proj_kernel,
        grid=(b, s // tm),
        in_specs=[
            pl.BlockSpec((1, tm, D_MODEL), lambda bi, i: (bi, i, 0)),
            pl.BlockSpec((1, 8, CONV_W), lambda bi, i: (bi, 0, 0)),
            full((1, D_MODEL)),
            full((D_MODEL, PROJ_PAD)),
            full((1, LANES)),
            full((8, CONV_W)),
            full((1, CONV_W)),
        ],
        out_specs=[
            pl.BlockSpec((1, tm, ATTN_W), lambda bi, i: (bi, i, 0)),
            pl.BlockSpec((1, tm, ATTN_W), lambda bi, i: (bi, i, 0)),
            pl.BlockSpec((1, tm, HEADS), lambda bi, i: (bi, i, 0)),
            pl.BlockSpec((1, HEADS, tm, LANES), lambda bi, i: (bi, 0, i, 0)),
            pl.BlockSpec((1, HEADS, tm, LANES), lambda bi, i: (bi, 0, i, 0)),
            pl.BlockSpec((1, tm, CONV_W), lambda bi, i: (bi, i, 0)),
            pl.BlockSpec((1, 2, CONV_W), lambda bi, i: (bi, 0, 0)),
        ],
        out_shape=[
            jax.ShapeDtypeStruct((b, s, ATTN_W), F32),
            jax.ShapeDtypeStruct((b, s, ATTN_W), F32),
            jax.ShapeDtypeStruct((b, s, HEADS), F32),
            jax.ShapeDtypeStruct((b, HEADS, s, LANES), BF16),
            jax.ShapeDtypeStruct((b, HEADS, s, LANES), BF16),
            jax.ShapeDtypeStruct((b, s, CONV_W), BF16),
            jax.ShapeDtypeStruct((b, 2, CONV_W), F32),
        ],
        scratch_shapes=[pltpu.VMEM((tm + 8, CONV_W), F32), pltpu.VMEM((8, LANES), F32)],
        compiler_params=pltpu.CompilerParams(
            dimension_semantics=("arbitrary", "arbitrary"), vmem_limit_bytes=VMEM_LIMIT),
        name="proj",
    )(x, state, gm, w_all, bias_f, cw, gc)


ATT_BLK = 256


def _attn_kernel(qa_ref, ka_ref, v_ref, o_ref, sbuf):
    blk = ATT_BLK
    vt = v_ref[0].T.astype(BF16)
    r = lax.broadcasted_iota(jnp.int32, (blk, blk), 0)
    c = lax.broadcasted_iota(jnp.int32, (blk, blk), 1)
    causal = r <= c
    units = [(iq, hh) for iq in range(ka_ref.shape[2] // blk) for hh in range(2)]

    def scores(u):
        iq, hh = units[u]
        n = (iq + 1) * blk
        sbuf[u % nbuf, 0:n, :] = _dot_nt(ka_ref[0, hh, 0:n, :], qa_ref[0, hh, iq * blk:n, :])

    nbuf = sbuf.shape[0]
    for u in range(nbuf - 1):
        scores(u)
    outs = []
    for u, (iq, hh) in enumerate(units):
        if u + nbuf - 1 < len(units):
            scores(u + nbuf - 1)
        noff = iq * blk
        rows = slice(hh * HEAD_DIM, (hh + 1) * HEAD_DIM)
        s_d = jnp.where(causal, sbuf[u % nbuf, noff:noff + blk, :], -jnp.inf)
        m = jnp.max(s_d, axis=0, keepdims=True)
        if iq > 0:
            m = jnp.maximum(m, jnp.max(sbuf[u % nbuf, 0:noff, :], axis=0, keepdims=True))
        p_d = jnp.exp2(s_d - m)
        l = jnp.sum(p_d, axis=0, keepdims=True)
        acc = _dot(vt[rows, noff:noff + blk], p_d.astype(BF16))
        if iq > 0:
            p_o = jnp.exp2(sbuf[u % nbuf, 0:noff, :] - m)
            l = l + jnp.sum(p_o, axis=0, keepdims=True)
            acc = acc + _dot(vt[rows, 0:noff], p_o.astype(BF16))
        outs.append(acc / l)
        if hh == 1:
            o_ref[0, noff:noff + blk, :] = jnp.concatenate(outs, axis=0).T
            outs = []


def _attention(qa, ka, v):
    b, _, s, _ = qa.shape
    return pl.pallas_call(
        _attn_kernel,
        grid=(b, HEADS // 2),
        in_specs=[
            pl.BlockSpec((1, 2, s, LANES), lambda bi, p: (bi, p, 0, 0)),
            pl.BlockSpec((1, 2, s, LANES), lambda bi, p: (bi, p, 0, 0)),
            pl.BlockSpec((1, s, LANES), lambda bi, p: (bi, 0, p)),
        ],
        out_specs=pl.BlockSpec((1, s, LANES), lambda bi, p: (bi, 0, p)),
        out_shape=jax.ShapeDtypeStruct((b, s, ATTN_W), F32),
        scratch_shapes=[pltpu.VMEM((3, s, ATT_BLK), F32)],
        compiler_params=pltpu.CompilerParams(
            dimension_semantics=("arbitrary", "arbitrary"), vmem_limit_bytes=VMEM_LIMIT),
        name="attention",
    )(qa, ka, v)


Q_PAD = 128


def _pad_rows(x, n):
    return jnp.concatenate([x, jnp.zeros((n - x.shape[0], x.shape[1]), x.dtype)], axis=0)


def _attn_cache_kernel(qa_ref, ka_ref, v_ref, ck_ref, cv_ref, clf_ref, o_ref):
    t = qa_ref.shape[2]
    plen = ck_ref.shape[1]
    cpast = _cumsum_rows(clf_ref[0])
    cpast = cpast - cpast[plen - 1:plen, :]
    bias_lanes = _key_bias_lanes(cpast * (-LOG2E))
    ck = ck_ref[0]
    r = lax.broadcasted_iota(jnp.int32, (Q_PAD, Q_PAD), 0)
    c = lax.broadcasted_iota(jnp.int32, (Q_PAD, Q_PAD), 1)
    new_ok = (r <= c) & (r < t)

    pairs = []
    for p in range(HEADS // 2):
        vct = cv_ref[0][:, p * LANES:(p + 1) * LANES].T.astype(BF16)
        vnt = _pad_rows(v_ref[0][:, p * LANES:(p + 1) * LANES], Q_PAD).T.astype(BF16)
        outs = []
        for hh in range(2):
            h = 2 * p + hh
            q = _pad_rows(qa_ref[0, h], Q_PAD)
            s_past = _dot_nt(_key_operand(ck, bias_lanes, h), q)
            s_new = jnp.where(new_ok, _dot_nt(_pad_rows(ka_ref[0, h], Q_PAD), q), -jnp.inf)
            m = jnp.maximum(jnp.max(s_past, axis=0, keepdims=True), jnp.max(s_new, axis=0, keepdims=True))
            p_past = jnp.exp2(s_past - m)
            p_new = jnp.exp2(s_new - m)
            l = jnp.sum(p_past, axis=0, keepdims=True) + jnp.sum(p_new, axis=0, keepdims=True)
            rows = slice(hh * HEAD_DIM, (hh + 1) * HEAD_DIM)
            acc = _dot(vct[rows], p_past.astype(BF16)) + _dot(vnt[rows], p_new.astype(BF16))
            outs.append(acc / l)
        pairs.append(jnp.concatenate(outs, axis=0).T[0:t, :])
    o_ref[0] = jnp.concatenate(pairs, axis=1)


def _attention_cache(qa, ka, v, ck, cv, clf):
    b, _, t, _ = qa.shape
    plen = ck.shape[1]
    return pl.pallas_call(
        _attn_cache_kernel,
        grid=(b,),
        in_specs=[
            pl.BlockSpec((1, HEADS, t, LANES), lambda bi: (bi, 0, 0, 0)),
            pl.BlockSpec((1, HEADS, t, LANES), lambda bi: (bi, 0, 0, 0)),
            pl.BlockSpec((1, t, ATTN_W), lambda bi: (bi, 0, 0)),
            pl.BlockSpec((1, plen, ATTN_W), lambda bi: (bi, 0, 0)),
            pl.BlockSpec((1, plen, ATTN_W), lambda bi: (bi, 0, 0)),
            pl.BlockSpec((1, plen, LANES), lambda bi: (bi, 0, 0)),
        ],
        out_specs=pl.BlockSpec((1, t, ATTN_W), lambda bi: (bi, 0, 0)),
        out_shape=jax.ShapeDtypeStruct((b, t, ATTN_W), F32),
        compiler_params=pltpu.CompilerParams(
            dimension_semantics=("arbitrary",), vmem_limit_bytes=VMEM_LIMIT),
        name="attention_cache",
    )(qa, ka, v, ck, cv, clf)


def _mix_kernel(x_ref, o_ref, mb_ref, wo_ref, ga_ref, gf_ref, wr_ref, br_ref,
                x2_ref, rows_ref, slot_ref, cnt_ref, carry):
    i = pl.program_id(0)
    tm = x_ref.shape[0]
    oa = _rms(o_ref[...], ga_ref[...]).astype(BF16)
    merged = jnp.concatenate([oa, mb_ref[...]], axis=1)
    x2 = x_ref[...] + _dot(merged, wo_ref[...])
    x2_ref[...] = x2
    xb = _rms(x2, gf_ref[...]).astype(BF16)
    rows_ref[:, 0:D_MODEL] = xb.astype(F32)

    lt = (_dot(xb, wr_ref[...]) + br_ref[...]).T
    row = lax.broadcasted_iota(jnp.int32, (EPG, tm), 0)
    lg = jnp.where(row < N_GROUPS, lt[0:EPG], -jnp.inf)
    eg = jnp.exp(lg - jnp.max(lg, axis=0, keepdims=True))
    pg = eg / jnp.sum(eg, axis=0, keepdims=True)
    pg_top = jnp.max(pg, axis=0, keepdims=True)
    g_top = jnp.min(jnp.where(pg == pg_top, row, EPG), axis=0, keepdims=True)

    le = lt[EPG:2 * EPG]
    for g in range(1, N_GROUPS):
        le = jnp.where(g_top == g, lt[EPG * (g + 1):EPG * (g + 2)], le)
    ee = jnp.exp(le - jnp.max(le, axis=0, keepdims=True))
    pe = ee / jnp.sum(ee, axis=0, keepdims=True)
    v1 = jnp.max(pe, axis=0, keepdims=True)
    i1 = jnp.min(jnp.where(pe == v1, row, EPG), axis=0, keepdims=True)
    pe2 = jnp.where(row == i1, -1.0, pe)
    v2 = jnp.max(pe2, axis=0, keepdims=True)
    i2 = jnp.min(jnp.where(pe2 == v2, row, EPG), axis=0, keepdims=True)
    den = v1 + v2
    wg = jnp.where(row == i1, v1 / den * pg_top, 0.0) + jnp.where(row == i2, v2 / den * pg_top, 0.0)
    rows_ref[:, D_MODEL:] = jnp.concatenate([wg, jnp.zeros((LANES - EPG, tm), F32)], axis=0).T

    @pl.when(i == 0)
    def _():
        carry[...] = jnp.zeros_like(carry)

    onehot = jnp.where(row == g_top, 1.0, 0.0)
    r2 = lax.broadcasted_iota(jnp.int32, (tm, tm), 0)
    c2 = lax.broadcasted_iota(jnp.int32, (tm, tm), 1)
    earlier = jnp.where(r2 < c2, 1.0, 0.0).astype(BF16)
    base = carry[:, 0:1]
    rank = jnp.sum(onehot * (_dot(onehot.astype(BF16), earlier) + base), axis=0, keepdims=True)
    slot_ref[i] = g_top * (1 << GROUP_SHIFT) + rank.astype(jnp.int32)
    total = base + jnp.sum(onehot, axis=1, keepdims=True)
    carry[...] = jnp.broadcast_to(total, carry.shape)
    cnt_ref[...] = carry[...].astype(jnp.int32)

    @pl.when(i == pl.num_programs(0) - 1)
    def _():
        padded = jnp.floor((total + (EXPERT_TILE - 1)) * (1.0 / EXPERT_TILE)) * EXPERT_TILE
        o1 = padded[0:1]
        o2 = o1 + padded[1:2]
        o3 = o2 + padded[2:3]
        code = slot_ref[...]
        g = code >> GROUP_SHIFT
        first = jnp.where(g == 0, 0.0, jnp.where(g == 1, o1, jnp.where(g == 2, o2, o3)))
        slot_ref[...] = first.astype(jnp.int32) + (code & ((1 << GROUP_SHIFT) - 1))


def _mix(x, o, mb, wo, ga, gf, wr, br, tm):
    n = x.shape[0]
    assert n < (1 << GROUP_SHIFT)
    full = lambda shape: pl.BlockSpec(shape, lambda i: (0,) * len(shape))
    return pl.pallas_call(
        _mix_kernel,
        grid=(n // tm,),
        in_specs=[
            pl.BlockSpec((tm, D_MODEL), lambda i: (i, 0)),
            pl.BlockSpec((tm, ATTN_W), lambda i: (i, 0)),
            pl.BlockSpec((tm, CONV_W), lambda i: (i, 0)),
            full((D_MODEL, D_MODEL)),
            full((1, ATTN_W)),
            full((1, D_MODEL)),
            full((D_MODEL, LANES)),
            full((1, LANES)),
        ],
        out_specs=[
            pl.BlockSpec((tm, D_MODEL), lambda i: (i, 0)),
            pl.BlockSpec((tm, ROW_W), lambda i: (i, 0)),
            full((n // tm, 1, tm)),
            full((EPG, LANES)),
        ],
        out_shape=[
            jax.ShapeDtypeStruct((n, D_MODEL), F32),
            jax.ShapeDtypeStruct((n, ROW_W), F32),
            jax.ShapeDtypeStruct((n // tm, 1, tm), jnp.int32),
            jax.ShapeDtypeStruct((EPG, LANES), jnp.int32),
        ],
        scratch_shapes=[pltpu.VMEM((EPG, LANES), F32)],
        compiler_params=pltpu.CompilerParams(
            dimension_semantics=("arbitrary",), vmem_limit_bytes=VMEM_LIMIT),
        name="mix",
    )(x, o, mb, wo, ga, gf, wr, br)


def _sorted_tiles(n):
    return n // EXPERT_TILE + N_GROUPS


def _group_tiles(cnt_ref):
    nt = [(cnt_ref[g] + (EXPERT_TILE - 1)) >> TILE_SHIFT for g in range(N_GROUPS)]
    s1 = nt[0]
    s2 = s1 + nt[1]
    s3 = s2 + nt[2]
    return s1, s2, s3, s3 + nt[3]


def _row_copies(n, copy_of_row):
    for r in range(n):
        copy_of_row(r).start(priority=r % 2)
    for r in range(n):
        copy_of_row(r).wait()


def _scatter_kernel(cnt_ref, rows_ref, slot_ref, xs_ref, zeros, sem, zsem):
    tm = rows_ref.shape[0]

    @pl.when(pl.program_id(0) == 0)
    def _():
        zeros[...] = jnp.zeros_like(zeros)
        s1, s2, s3, tot = _group_tiles(cnt_ref)
        ends = (s1, s2, s3, tot)

        def fill(tile):
            return pltpu.make_async_copy(zeros, xs_ref.at[pl.ds(tile * EXPERT_TILE, EXPERT_TILE), :], zsem)

        for g in range(N_GROUPS):
            partial = (cnt_ref[g] & (EXPERT_TILE - 1)) != 0

            @pl.when(partial)
            def _():
                fill(ends[g] - 1).start()
                fill(ends[g] - 1).wait()

        def fill_tail(tile, c):
            fill(tile).start()
            fill(tile).wait()
            return c

        lax.fori_loop(tot, xs_ref.shape[0] // EXPERT_TILE, fill_tail, 0)

    _row_copies(tm, lambda r: pltpu.make_async_copy(
        rows_ref.at[pl.ds(r, 1), :], xs_ref.at[pl.ds(slot_ref[0, 0, r], 1), :], sem))


def _scatter(cnt, rows, slots, tm):
    n = rows.shape[0]
    return pl.pallas_call(
        _scatter_kernel,
        grid_spec=pltpu.PrefetchScalarGridSpec(
            num_scalar_prefetch=1,
            grid=(n // tm,),
            in_specs=[
                pl.BlockSpec((tm, ROW_W), lambda i, c: (i, 0)),
                pl.BlockSpec((1, 1, tm), lambda i, c: (i, 0, 0), memory_space=pltpu.SMEM),
            ],
            out_specs=pl.BlockSpec(memory_space=pl.ANY),
            scratch_shapes=[pltpu.VMEM((EXPERT_TILE, ROW_W), F32),
                            pltpu.SemaphoreType.DMA(()), pltpu.SemaphoreType.DMA(())],
        ),
        out_shape=jax.ShapeDtypeStruct((_sorted_tiles(n) * EXPERT_TILE, ROW_W), F32),
        compiler_params=pltpu.CompilerParams(
            dimension_semantics=("arbitrary",), vmem_limit_bytes=VMEM_LIMIT),
        name="scatter",
    )(cnt, rows, slots)


def _tile_of(t, cnt_ref):
    s1, s2, s3, tot = _group_tiles(cnt_ref)
    tc = jnp.minimum(t, tot - 1)
    g = (tc >= s1).astype(jnp.int32) + (tc >= s2).astype(jnp.int32) + (tc >= s3).astype(jnp.int32)
    first = jnp.where(g == 0, 0, jnp.where(g == 1, s1, jnp.where(g == 2, s2, s3)))
    cnt = jnp.where(g == 0, cnt_ref[0], jnp.where(g == 1, cnt_ref[1], jnp.where(g == 2, cnt_ref[2], cnt_ref[3])))
    return tc, g, jnp.where(t < tot, cnt - (tc - first) * EXPERT_TILE, 0)


def _experts_kernel(cnt_ref, xs_ref, wgu_ref, wdn_ref, ys_ref):
    _, _, valid = _tile_of(pl.program_id(0), cnt_ref)

    @pl.when(valid <= 0)
    def _():
        ys_ref[...] = jnp.zeros_like(ys_ref)

    @pl.when(valid > 0)
    def _():
        xb = xs_ref[:, 0:D_MODEL].astype(BF16)
        wts = xs_ref[:, D_MODEL:]
        tot = jnp.zeros((EXPERT_TILE, D_MODEL), F32)
        for e in range(EPG):
            gu = _dot(xb, wgu_ref[0, e])
            gate = gu[:, :D_EXPERT]
            hmid = gate * jax.nn.sigmoid(gate) * gu[:, D_EXPERT:] * wts[:, e:e + 1]
            tot = tot + _dot(hmid.astype(BF16), wdn_ref[0, e])
        ys_ref[...] = tot


def _experts(cnt, xs, wgu, wdn):
    tiles = xs.shape[0] // EXPERT_TILE
    grp = lambda t, c: (_tile_of(t, c)[1], 0, 0, 0)
    return pl.pallas_call(
        _experts_kernel,
        grid_spec=pltpu.PrefetchScalarGridSpec(
            num_scalar_prefetch=1,
            grid=(tiles,),
            in_specs=[
                pl.BlockSpec((EXPERT_TILE, ROW_W), lambda t, c: (_tile_of(t, c)[0], 0)),
                pl.BlockSpec((1, EPG, D_MODEL, 2 * D_EXPERT), grp),
                pl.BlockSpec((1, EPG, D_EXPERT, D_MODEL), grp),
            ],
            out_specs=pl.BlockSpec((EXPERT_TILE, D_MODEL), lambda t, c: (t, 0)),
        ),
        out_shape=jax.ShapeDtypeStruct((tiles * EXPERT_TILE, D_MODEL), F32),
        compiler_params=pltpu.CompilerParams(
            dimension_semantics=("arbitrary",), vmem_limit_bytes=VMEM_LIMIT),
        name="experts",
    )(cnt, xs, wgu, wdn)


def _gather_kernel(x2_ref, slot_ref, gn_ref, ys_ref, y_ref, buf, sem):
    tm = x2_ref.shape[0]
    _row_copies(tm, lambda r: pltpu.make_async_copy(
        ys_ref.at[pl.ds(slot_ref[0, 0, r], 1), :], buf.at[pl.ds(r, 1), :], sem))
    y_ref[...] = _rms(x2_ref[...] + buf[...], gn_ref[...])


def _gather(x2, slots, gn, ys, tm):
    n = x2.shape[0]
    return pl.pallas_call(
        _gather_kernel,
        grid=(n // tm,),
        in_specs=[
            pl.BlockSpec((tm, D_MODEL), lambda i: (i, 0)),
            pl.BlockSpec((1, 1, tm), lambda i: (i, 0, 0), memory_space=pltpu.SMEM),
            pl.BlockSpec((1, D_MODEL), lambda i: (0, 0)),
            pl.BlockSpec(memory_space=pl.ANY),
        ],
        out_specs=pl.BlockSpec((tm, D_MODEL), lambda i: (i, 0)),
        out_shape=jax.ShapeDtypeStruct((n, D_MODEL), F32),
        scratch_shapes=[pltpu.VMEM((tm, D_MODEL), F32), pltpu.SemaphoreType.DMA(())],
        compiler_params=pltpu.CompilerParams(
            dimension_semantics=("arbitrary",), vmem_limit_bytes=VMEM_LIMIT),
        name="gather",
    )(x2, slots, gn, ys)


def _forget_lanes(f):
    z = lambda n: jnp.zeros(f.shape[:-1] + (n,), f.dtype)
    return jnp.concatenate([f, z(F_HI - HEADS), f, f, f, z(LANES - F_LO - HEADS)], axis=-1)


def _trunk(x, attend, state, w, tm_proj, tm_tok):
    b, s, _ = x.shape
    k, v, lf, qa, ka, mb, ncv = _proj(x, state, w["gm"], w["w_all"], w["bias_f"], w["cw"], w["gc"], tm_proj)
    o = attend(qa, ka, v)
    n = b * s
    x2, rows, slots, cnt = _mix(x.reshape(n, D_MODEL), o.reshape(n, ATTN_W), mb.reshape(n, CONV_W),
                                w["wo"], w["ga"], w["gf"], w["wr"], w["br"], tm_tok)
    cnt = cnt[:, 0]
    ys = _experts(cnt, _scatter(cnt, rows, slots, tm_tok), w["wgu"], w["wdn"])
    y = _gather(x2, slots, w["gn"], ys, tm_tok)
    return (y.reshape(b, s, D_MODEL), k.reshape(1, b, s, HEADS, HEAD_DIM), v.reshape(1, b, s, HEADS, HEAD_DIM),
            lf.reshape(1, b, s, HEADS), ncv.reshape(1, b, 2, CONV_W))


def kernel(x_prompt, x_sample, cache_k, cache_v, cache_logf, state_conv, norm_mix_g, w_in, b_forget, conv_w, norm_attn_g, norm_conv_g, w_out, norm_ffn_g, w_router_group, b_router_group, w_router_expert, b_router_expert, w_expert_gate_up, w_expert_down, norm_final_g):
    assert w_in.shape[0] == 1, "single-layer trunk"
    win = w_in[0]
    a3 = 3 * ATTN_W
    w_all = jnp.concatenate(
        [win[:, :a3], _forget_lanes(win[:, a3:a3 + HEADS]), win[:, a3 + HEADS:]], axis=1).astype(BF16)
    wr = jnp.concatenate(
        [w_router_group[0], jnp.zeros((D_MODEL, EPG - N_GROUPS), F32),
         w_router_expert[0].transpose(1, 0, 2).reshape(D_MODEL, N_GROUPS * EPG),
         jnp.zeros((D_MODEL, LANES - EPG - N_GROUPS * EPG), F32)], axis=1).astype(BF16)
    br = jnp.concatenate(
        [b_router_group[0], jnp.zeros((EPG - N_GROUPS,), F32), b_router_expert[0].reshape(-1),
         jnp.zeros((LANES - EPG - N_GROUPS * EPG,), F32)]).reshape(1, LANES)
    w = dict(
        gm=norm_mix_g[0].reshape(1, D_MODEL), w_all=w_all, bias_f=_forget_lanes(b_forget[0]).reshape(1, LANES),
        cw=jnp.concatenate([conv_w[0], jnp.zeros((8 - conv_w.shape[1], CONV_W), F32)], axis=0),
        gc=norm_conv_g[0].reshape(1, CONV_W), wo=w_out[0].astype(BF16), ga=norm_attn_g[0].reshape(1, ATTN_W),
        gf=norm_ffn_g[0].reshape(1, D_MODEL), wr=wr, br=br,
        wgu=w_expert_gate_up[0].astype(BF16), wdn=w_expert_down[0].astype(BF16),
        gn=norm_final_g.reshape(1, D_MODEL))

    bp, sp, _ = x_prompt.shape
    bs, ss, _ = x_sample.shape
    plen = cache_k.shape[2]
    yp, kp, vp, lfp, cvp = _trunk(x_prompt, _attention, jnp.zeros((bp, 8, CONV_W), F32), w, 512, 512)

    ck = cache_k[0].reshape(bs, plen, ATTN_W)
    cv = cache_v[0].reshape(bs, plen, ATTN_W)
    clf = _forget_lanes(cache_logf[0])
    st = jnp.concatenate([jnp.zeros((bs, 6, CONV_W), F32), state_conv[0]], axis=1)
    attend_s = lambda qa, ka, v: _attention_cache(qa, ka, v, ck, cv, clf)
    ys, ks, vs, lfs, cvs = _trunk(x_sample, attend_s, st, w, ss, bs * ss)
    return (yp, ys, kp, vp, lfp, cvp, ks, vs, lfs, cvs)
```

```python
import functools
import math

import jax
import jax.numpy as jnp
from jax import lax
from jax.experimental import pallas as pl
from jax.experimental.pallas import tpu as pltpu

F32 = jnp.float32
BF16 = jnp.bfloat16

D_MODEL = 1024
HEADS = 8
HEAD_DIM = 64
ATTN_W = HEADS * HEAD_DIM
CONV_W = 512
N_GROUPS = 4
EPG = 8
D_EXPERT = 256
EPS = 1e-6
LOG2E = math.log2(math.e)
Q_SCALE = HEAD_DIM ** -0.5 * LOG2E

LANES = 128
COL_Q, COL_K, COL_V, COL_F, COL_BG, COL_CG, COL_H = 0, 512, 1024, 1536, 1664, 2176, 2688
PROJ_PAD = 3200
W_SECTIONS = ((0, COL_Q), (512, COL_K), (1024, COL_V), (1544, COL_BG), (2056, COL_CG), (2568, COL_H))
W_F_ROW = 1536
F_OUT, F_HI, F_MID, F_LO = 0, 64, 72, 80

GROUP_SHIFT = 20
ROW_W = D_MODEL + LANES
TILE_SHIFT = 8
EXPERT_TILE = 1 << TILE_SHIFT

VMEM_LIMIT = 60 * 1024 * 1024


def _dot(a, b):
    return jnp.dot(a, b, preferred_element_type=F32)


def _dot_nt(a, b):
    return lax.dot_general(a, b, (((1,), (1,)), ((), ())), preferred_element_type=F32)


def _rms(x, g):
    return x * lax.rsqrt(jnp.mean(x * x, axis=-1, keepdims=True) + EPS) * g


def _log_sigmoid(x):
    return jnp.minimum(x, 0.0) - jnp.log(1.0 + jnp.exp(-jnp.abs(x)))


def _pad_rows(x, n):
    return jnp.concatenate([x, jnp.zeros((n - x.shape[0], x.shape[1]), x.dtype)], axis=0)


def _split3(x):
    a1 = x.astype(BF16)
    r1 = x - a1.astype(F32)
    a2 = r1.astype(BF16)
    a3 = (r1 - a2.astype(F32)).astype(BF16)
    return a1, a2, a3


def _cumsum_rows(x):
    n = x.shape[0]
    r = lax.broadcasted_iota(jnp.int32, (n, n), 0)
    c = lax.broadcasted_iota(jnp.int32, (n, n), 1)
    tri = jnp.where(r >= c, 1.0, 0.0).astype(BF16)
    parts = _dot(tri, jnp.concatenate(_split3(x), axis=1))
    return parts[:, :LANES] + parts[:, LANES:2 * LANES] + parts[:, 2 * LANES:]


def _key_bias_lanes(cneg):
    lane = lax.broadcasted_iota(jnp.int32, cneg.shape, 1)
    hi = cneg.astype(BF16).astype(F32)
    r1 = cneg - hi
    mid = r1.astype(BF16).astype(F32)
    lo = r1 - mid
    out = jnp.where(lane < F_MID, hi, jnp.where(lane < F_LO, mid, lo))
    return jnp.where((lane >= F_HI) & (lane < F_LO + HEADS), out, 0.0)


def _head_block(x, h):
    blk = x[:, (h // 2) * LANES:(h // 2 + 1) * LANES]
    return pltpu.roll(blk, HEAD_DIM, 1) if h % 2 else blk


def _key_operand(k, bias_lanes, h):
    lane = lax.broadcasted_iota(jnp.int32, bias_lanes.shape, 1)
    return jnp.where(lane < HEAD_DIM, _head_block(k, h), bias_lanes).astype(BF16)


def _prepare_weight(wt_hbm, w_ref, stage, sem):
    for r0, c0 in W_SECTIONS:
        cp = pltpu.make_async_copy(wt_hbm.at[pl.ds(r0, ATTN_W), :], stage, sem)
        cp.start()
        cp.wait()
        w_ref[:, c0:c0 + ATTN_W] = stage[...].T.astype(BF16)
    cp = pltpu.make_async_copy(wt_hbm.at[pl.ds(W_F_ROW, HEADS), :], stage.at[pl.ds(0, HEADS), :], sem)
    cp.start()
    cp.wait()
    f = stage[0:HEADS, :]
    z = lambda n: jnp.zeros((n, D_MODEL), F32)
    fblk = jnp.concatenate([f, z(F_HI - HEADS), f, f, f, z(LANES - F_LO - HEADS)], axis=0)
    w_ref[:, COL_F:COL_F + LANES] = fblk.T.astype(BF16)


def _proj_kernel(prepare, x_ref, st_ref, gm_ref, w_in_ref, bf_ref, cw_ref, gc_ref,
                 k_ref, v_ref, lf_ref, qa_ref, ka_ref, mb_ref, ncv_ref, *rest):
    i = pl.program_id(1)
    tm = x_ref.shape[1]
    if prepare:
        w_ref, ubuf, ccar, stage, sem = rest

        @pl.when((pl.program_id(0) == 0) & (i == 0))
        def _():
            _prepare_weight(w_in_ref, w_ref, stage, sem)
    else:
        w_ref = w_in_ref
        ubuf, ccar = rest

    xn = _rms(x_ref[0], gm_ref[...]).astype(BF16)
    proj = lambda c0, n: _dot(xn, w_ref[:, c0:c0 + n])

    lfb = _log_sigmoid(proj(COL_F, LANES) + bf_ref[...])
    pad = (-tm) % LANES
    lf_ref[0] = (_pad_rows(lfb, tm + pad) if pad else lfb).T[F_OUT:F_OUT + HEADS, 0:tm]

    @pl.when(i == 0)
    def _():
        ccar[...] = jnp.zeros_like(ccar)

    c = _cumsum_rows(lfb) + ccar[0:1, :]
    ccar[...] = jnp.broadcast_to(c[tm - 1:tm, :], ccar.shape)
    bias_lanes = _key_bias_lanes(c * (-LOG2E))
    lane = lax.broadcasted_iota(jnp.int32, (tm, LANES), 1)
    k = proj(COL_K, ATTN_W)
    k_ref[0] = k
    for h in range(HEADS):
        ka_ref[0, h] = _key_operand(k, bias_lanes, h)
    q = proj(COL_Q, ATTN_W)
    for h in range(HEADS):
        sel = jnp.where((lane == F_HI + h) | (lane == F_MID + h) | (lane == F_LO + h), 1.0, 0.0)
        qa_ref[0, h] = jnp.where(lane < HEAD_DIM, _head_block(q, h) * Q_SCALE, sel).astype(BF16)
    v_ref[0] = proj(COL_V, ATTN_W)

    u = proj(COL_CG, CONV_W) * proj(COL_H, CONV_W)

    @pl.when(i == 0)
    def _():
        ubuf[0:8, :] = st_ref[0]

    @pl.when(i > 0)
    def _():
        ubuf[0:8, :] = ubuf[tm:tm + 8, :]

    ubuf[8:8 + tm, :] = u
    y = cw_ref[0:1, :] * ubuf[6:6 + tm, :] + cw_ref[1:2, :] * ubuf[7:7 + tm, :] + cw_ref[2:3, :] * u
    ob = proj(COL_BG, CONV_W) * y
    mb_ref[0] = _rms(ob, gc_ref[...]).astype(BF16)
    ncv_ref[0] = u[tm - 2:tm, :]


def _proj(x, state, gm, w, bias_f, cw, gc, tm):
    b, s, _ = x.shape
    prepare = w.dtype == F32
    full = lambda shape: pl.BlockSpec(shape, lambda bi, i: (0,) * len(shape))
    out_specs = [
        pl.BlockSpec((1, tm, ATTN_W), lambda bi, i: (bi, i, 0)),
        pl.BlockSpec((1, tm, ATTN_W), lambda bi, i: (bi, i, 0)),
        pl.BlockSpec((1, HEADS, tm), lambda bi, i: (bi, 0, i)),
        pl.BlockSpec((1, HEADS, tm, LANES), lambda bi, i: (bi, 0, i, 0)),
        pl.BlockSpec((1, HEADS, tm, LANES), lambda bi, i: (bi, 0, i, 0)),
        pl.BlockSpec((1, tm, CONV_W), lambda bi, i: (bi, i, 0)),
        pl.BlockSpec((1, 2, CONV_W), lambda bi, i: (bi, 0, 0)),
    ]
    out_shape = [
        jax.ShapeDtypeStruct((b, s, ATTN_W), F32),
        jax.ShapeDtypeStruct((b, s, ATTN_W), F32),
        jax.ShapeDtypeStruct((b, HEADS, s), F32),
        jax.ShapeDtypeStruct((b, HEADS, s, LANES), BF16),
        jax.ShapeDtypeStruct((b, HEADS, s, LANES), BF16),
        jax.ShapeDtypeStruct((b, s, CONV_W), BF16),
        jax.ShapeDtypeStruct((b, 2, CONV_W), F32),
    ]
    scratch = [pltpu.VMEM((tm + 8, CONV_W), F32), pltpu.VMEM((8, LANES), F32)]
    if prepare:
        out_specs.append(full((D_MODEL, PROJ_PAD)))
        out_shape.append(jax.ShapeDtypeStruct((D_MODEL, PROJ_PAD), BF16))
        scratch += [pltpu.VMEM((ATTN_W, D_MODEL), F32), pltpu.SemaphoreType.DMA(())]
    return pl.pallas_call(
        functools.partial(_proj_kernel, prepare),
        grid=(b, s // tm),
        in_specs=[
            pl.BlockSpec((1, tm, D_MODEL), lambda bi, i: (bi, i, 0)),
            pl.BlockSpec((1, 8, CONV_W), lambda bi, i: (bi, 0, 0)),
            full((1, D_MODEL)),
            pl.BlockSpec(memory_space=pl.ANY) if prepare else full((D_MODEL, PROJ_PAD)),
            full((1, LANES)),
            full((8, CONV_W)),
            full((1, CONV_W)),
        ],
        out_specs=out_specs,
        out_shape=out_shape,
        scratch_shapes=scratch,
        compiler_params=pltpu.CompilerParams(
            dimension_semantics=("arbitrary", "arbitrary"), vmem_limit_bytes=VMEM_LIMIT),
        name="proj",
    )(x, state, gm, w, bias_f, cw, gc)


ATT_BLK = 256


def _attn_kernel(qa_ref, ka_ref, v_ref, o_ref, sbuf):
    blk = ATT_BLK
    vt = v_ref[0].T.astype(BF16)
    r = lax.broadcasted_iota(jnp.int32, (blk, blk), 0)
    c = lax.broadcasted_iota(jnp.int32, (blk, blk), 1)
    causal = r <= c
    units = [(iq, hh) for iq in range(ka_ref.shape[2] // blk) for hh in range(2)]

    def scores(u):
        iq, hh = units[u]
        n = (iq + 1) * blk
        sbuf[u % nbuf, 0:n, :] = _dot_nt(ka_ref[0, hh, 0:n, :], qa_ref[0, hh, iq * blk:n, :])

    nbuf = sbuf.shape[0]
    for u in range(nbuf - 1):
        scores(u)
    outs = []
    for u, (iq, hh) in enumerate(units):
        if u + nbuf - 1 < len(units):
            scores(u + nbuf - 1)
        noff = iq * blk
        rows = slice(hh * HEAD_DIM, (hh + 1) * HEAD_DIM)
        s_d = jnp.where(causal, sbuf[u % nbuf, noff:noff + blk, :], -jnp.inf)
        m = jnp.max(s_d, axis=0, keepdims=True)
        if iq > 0:
            m = jnp.maximum(m, jnp.max(sbuf[u % nbuf, 0:noff, :], axis=0, keepdims=True))
        p_d = jnp.exp2(s_d - m)
        l = jnp.sum(p_d, axis=0, keepdims=True)
        acc = _dot(vt[rows, noff:noff + blk], p_d.astype(BF16))
        if iq > 0:
            p_o = jnp.exp2(sbuf[u % nbuf, 0:noff, :] - m)
            l = l + jnp.sum(p_o, axis=0, keepdims=True)
            acc = acc + _dot(vt[rows, 0:noff], p_o.astype(BF16))
        outs.append(acc / l)
        if hh == 1:
            o_ref[0, noff:noff + blk, :] = jnp.concatenate(outs, axis=0).T
            outs = []


def _attention(qa, ka, v):
    b, _, s, _ = qa.shape
    return pl.pallas_call(
        _attn_kernel,
        grid=(b, HEADS // 2),
        in_specs=[
            pl.BlockSpec((1, 2, s, LANES), lambda bi, p: (bi, p, 0, 0)),
            pl.BlockSpec((1, 2, s, LANES), lambda bi, p: (bi, p, 0, 0)),
            pl.BlockSpec((1, s, LANES), lambda bi, p: (bi, 0, p)),
        ],
        out_specs=pl.BlockSpec((1, s, LANES), lambda bi, p: (bi, 0, p)),
        out_shape=jax.ShapeDtypeStruct((b, s, ATTN_W), F32),
        scratch_shapes=[pltpu.VMEM((3, s, ATT_BLK), F32)],
        compiler_params=pltpu.CompilerParams(
            dimension_semantics=("arbitrary", "arbitrary"), vmem_limit_bytes=VMEM_LIMIT),
        name="attention",
    )(qa, ka, v)


Q_PAD = 128


def _cumsum_lanes(x):
    n = x.shape[1]
    r = lax.broadcasted_iota(jnp.int32, (n, n), 0)
    c = lax.broadcasted_iota(jnp.int32, (n, n), 1)
    tri = jnp.where(r <= c, 1.0, 0.0).astype(BF16)
    terms = jnp.concatenate([a.astype(F32) for a in _split3(x)], axis=0).astype(BF16)
    parts = _dot(terms, tri)
    rows = x.shape[0]
    return parts[0:rows] + parts[rows:2 * rows] + parts[2 * rows:]


def _attn_cache_kernel(qa_ref, ka_ref, v_ref, ckt_ref, cvt_ref, clft_ref, o_ref):
    t = qa_ref.shape[2]
    plen = ckt_ref.shape[2]
    cinc = _cumsum_lanes(clft_ref[0])
    bias = (cinc - cinc[:, plen - 1:plen]) * (-LOG2E)
    lane = lax.broadcasted_iota(jnp.int32, (t, LANES), 1)
    rq = lax.broadcasted_iota(jnp.int32, (t, Q_PAD), 0)
    ck = lax.broadcasted_iota(jnp.int32, (t, Q_PAD), 1)
    new_ok = ck <= rq

    pairs = []
    for p in range(HEADS // 2):
        rows = slice(p * LANES, (p + 1) * LANES)
        kt = ckt_ref[0, rows, :].astype(BF16)
        vt = cvt_ref[0, rows, :].astype(BF16)
        vn = _pad_rows(v_ref[0][:, rows], Q_PAD).astype(BF16)
        res = []
        for hh in range(2):
            h = 2 * p + hh
            qa = qa_ref[0, h]
            qh = jnp.where(lane < HEAD_DIM, qa.astype(F32), 0.0)
            qh = (pltpu.roll(qh, HEAD_DIM, 1) if hh else qh).astype(BF16)
            s_past = _dot(qh, kt) + bias[h:h + 1, :]
            s_new = jnp.where(new_ok, _dot_nt(qa, _pad_rows(ka_ref[0, h], Q_PAD)), -jnp.inf)
            m = jnp.maximum(jnp.max(s_past, axis=1, keepdims=True), jnp.max(s_new, axis=1, keepdims=True))
            p_past = jnp.exp2(s_past - m)
            p_new = jnp.exp2(s_new - m)
            l = jnp.sum(p_past, axis=1, keepdims=True) + jnp.sum(p_new, axis=1, keepdims=True)
            acc = _dot_nt(p_past.astype(BF16), vt) + _dot(p_new.astype(BF16), vn)
            res.append(acc / l)
        pairs.append(jnp.where(lane < HEAD_DIM, res[0], res[1]))
    o_ref[0] = jnp.concatenate(pairs, axis=1)


def _attention_cache(qa, ka, v, ckt, cvt, clft):
    b, _, t, _ = qa.shape
    plen = ckt.shape[2]
    return pl.pallas_call(
        _attn_cache_kernel,
        grid=(b,),
        in_specs=[
            pl.BlockSpec((1, HEADS, t, LANES), lambda bi: (bi, 0, 0, 0)),
            pl.BlockSpec((1, HEADS, t, LANES), lambda bi: (bi, 0, 0, 0)),
            pl.BlockSpec((1, t, ATTN_W), lambda bi: (bi, 0, 0)),
            pl.BlockSpec((1, ATTN_W, plen), lambda bi: (bi, 0, 0)),
            pl.BlockSpec((1, ATTN_W, plen), lambda bi: (bi, 0, 0)),
            pl.BlockSpec((1, HEADS, plen), lambda bi: (bi, 0, 0)),
        ],
        out_specs=pl.BlockSpec((1, t, ATTN_W), lambda bi: (bi, 0, 0)),
        out_shape=jax.ShapeDtypeStruct((b, t, ATTN_W), F32),
        compiler_params=pltpu.CompilerParams(
            dimension_semantics=("arbitrary",), vmem_limit_bytes=VMEM_LIMIT),
        name="attention_cache",
    )(qa, ka, v, ckt, cvt, clft)


def _mix_kernel(x_ref, o_ref, mb_ref, wo_ref, ga_ref, gf_ref, wr_ref, br_ref,
                x2_ref, rows_ref, slot_ref, cnt_ref, carry):
    i = pl.program_id(0)
    tm = x_ref.shape[0]
    oa = _rms(o_ref[...], ga_ref[...]).astype(BF16)
    merged = jnp.concatenate([oa, mb_ref[...]], axis=1)
    x2 = x_ref[...] + _dot(merged, wo_ref[...])
    x2_ref[...] = x2
    xb = _rms(x2, gf_ref[...]).astype(BF16)
    rows_ref[:, 0:D_MODEL] = xb.astype(F32)

    lt = (_dot(xb, wr_ref[...]) + br_ref[...]).T
    row = lax.broadcasted_iota(jnp.int32, (EPG, tm), 0)
    lg = jnp.where(row < N_GROUPS, lt[0:EPG], -jnp.inf)
    eg = jnp.exp(lg - jnp.max(lg, axis=0, keepdims=True))
    pg = eg / jnp.sum(eg, axis=0, keepdims=True)
    pg_top = jnp.max(pg, axis=0, keepdims=True)
    g_top = jnp.min(jnp.where(pg == pg_top, row, EPG), axis=0, keepdims=True)

    le = lt[EPG:2 * EPG]
    for g in range(1, N_GROUPS):
        le = jnp.where(g_top == g, lt[EPG * (g + 1):EPG * (g + 2)], le)
    ee = jnp.exp(le - jnp.max(le, axis=0, keepdims=True))
    pe = ee / jnp.sum(ee, axis=0, keepdims=True)
    v1 = jnp.max(pe, axis=0, keepdims=True)
    i1 = jnp.min(jnp.where(pe == v1, row, EPG), axis=0, keepdims=True)
    pe2 = jnp.where(row == i1, -1.0, pe)
    v2 = jnp.max(pe2, axis=0, keepdims=True)
    i2 = jnp.min(jnp.where(pe2 == v2, row, EPG), axis=0, keepdims=True)
    den = v1 + v2
    wg = jnp.where(row == i1, v1 / den * pg_top, 0.0) + jnp.where(row == i2, v2 / den * pg_top, 0.0)
    rows_ref[:, D_MODEL:] = jnp.concatenate([wg, jnp.zeros((LANES - EPG, tm), F32)], axis=0).T

    @pl.when(i == 0)
    def _():
        carry[...] = jnp.zeros_like(carry)

    onehot = jnp.where(row == g_top, 1.0, 0.0)
    r2 = lax.broadcasted_iota(jnp.int32, (tm, tm), 0)
    c2 = lax.broadcasted_iota(jnp.int32, (tm, tm), 1)
    earlier = jnp.where(r2 < c2, 1.0, 0.0).astype(BF16)
    base = carry[:, 0:1]
    rank = jnp.sum(onehot * (_dot(onehot.astype(BF16), earlier) + base), axis=0, keepdims=True)
    slot_ref[i] = g_top * (1 << GROUP_SHIFT) + rank.astype(jnp.int32)
    total = base + jnp.sum(onehot, axis=1, keepdims=True)
    carry[...] = jnp.broadcast_to(total, carry.shape)
    cnt_ref[...] = carry[...].astype(jnp.int32)

    @pl.when(i == pl.num_programs(0) - 1)
    def _():
        padded = jnp.floor((total + (EXPERT_TILE - 1)) * (1.0 / EXPERT_TILE)) * EXPERT_TILE
        o1 = padded[0:1]
        o2 = o1 + padded[1:2]
        o3 = o2 + padded[2:3]
        code = slot_ref[...]
        g = code >> GROUP_SHIFT
        first = jnp.where(g == 0, 0.0, jnp.where(g == 1, o1, jnp.where(g == 2, o2, o3)))
        slot_ref[...] = first.astype(jnp.int32) + (code & ((1 << GROUP_SHIFT) - 1))


def _mix(x, o, mb, wo, ga, gf, wr, br, tm):
    n = x.shape[0]
    assert n < (1 << GROUP_SHIFT)
    full = lambda shape: pl.BlockSpec(shape, lambda i: (0,) * len(shape))
    return pl.pallas_call(
        _mix_kernel,
        grid=(n // tm,),
        in_specs=[
            pl.BlockSpec((tm, D_MODEL), lambda i: (i, 0)),
            pl.BlockSpec((tm, ATTN_W), lambda i: (i, 0)),
            pl.BlockSpec((tm, CONV_W), lambda i: (i, 0)),
            full((D_MODEL, D_MODEL)),
            full((1, ATTN_W)),
            full((1, D_MODEL)),
            full((D_MODEL, LANES)),
            full((1, LANES)),
        ],
        out_specs=[
            pl.BlockSpec((tm, D_MODEL), lambda i: (i, 0)),
            pl.BlockSpec((tm, ROW_W), lambda i: (i, 0)),
            full((n // tm, 1, tm)),
            full((EPG, LANES)),
        ],
        out_shape=[
            jax.ShapeDtypeStruct((n, D_MODEL), F32),
            jax.ShapeDtypeStruct((n, ROW_W), F32),
            jax.ShapeDtypeStruct((n // tm, 1, tm), jnp.int32),
            jax.ShapeDtypeStruct((EPG, LANES), jnp.int32),
        ],
        scratch_shapes=[pltpu.VMEM((EPG, LANES), F32)],
        compiler_params=pltpu.CompilerParams(
            dimension_semantics=("arbitrary",), vmem_limit_bytes=VMEM_LIMIT),
        name="mix",
    )(x, o, mb, wo, ga, gf, wr, br)


def _sorted_tiles(n):
    return n // EXPERT_TILE + N_GROUPS


def _group_tiles(cnt_ref):
    nt = [(cnt_ref[g] + (EXPERT_TILE - 1)) >> TILE_SHIFT for g in range(N_GROUPS)]
    s1 = nt[0]
    s2 = s1 + nt[1]
    s3 = s2 + nt[2]
    return s1, s2, s3, s3 + nt[3]


def _row_copies(n, copy_of_row):
    for r in range(n):
        copy_of_row(r).start(priority=r % 2)
    for r in range(n):
        copy_of_row(r).wait()


def _scatter_kernel(cnt_ref, rows_ref, slot_ref, xs_ref, zeros, sem, zsem):
    tm = rows_ref.shape[0]

    @pl.when(pl.program_id(0) == 0)
    def _():
        zeros[...] = jnp.zeros_like(zeros)
        s1, s2, s3, tot = _group_tiles(cnt_ref)
        ends = (s1, s2, s3, tot)

        def fill(tile):
            return pltpu.make_async_copy(zeros, xs_ref.at[pl.ds(tile * EXPERT_TILE, EXPERT_TILE), :], zsem)

        for g in range(N_GROUPS):
            partial = (cnt_ref[g] & (EXPERT_TILE - 1)) != 0

            @pl.when(partial)
            def _():
                fill(ends[g] - 1).start()
                fill(ends[g] - 1).wait()

        def fill_tail(tile, c):
            fill(tile).start()
            fill(tile).wait()
            return c

        lax.fori_loop(tot, xs_ref.shape[0] // EXPERT_TILE, fill_tail, 0)

    _row_copies(tm, lambda r: pltpu.make_async_copy(
        rows_ref.at[pl.ds(r, 1), :], xs_ref.at[pl.ds(slot_ref[0, 0, r], 1), :], sem))


def _scatter(cnt, rows, slots, tm):
    n = rows.shape[0]
    return pl.pallas_call(
        _scatter_kernel,
        grid_spec=pltpu.PrefetchScalarGridSpec(
            num_scalar_prefetch=1,
            grid=(n // tm,),
            in_specs=[
                pl.BlockSpec((tm, ROW_W), lambda i, c: (i, 0)),
                pl.BlockSpec((1, 1, tm), lambda i, c: (i, 0, 0), memory_space=pltpu.SMEM),
            ],
            out_specs=pl.BlockSpec(memory_space=pl.ANY),
            scratch_shapes=[pltpu.VMEM((EXPERT_TILE, ROW_W), F32),
                            pltpu.SemaphoreType.DMA(()), pltpu.SemaphoreType.DMA(())],
        ),
        out_shape=jax.ShapeDtypeStruct((_sorted_tiles(n) * EXPERT_TILE, ROW_W), F32),
        compiler_params=pltpu.CompilerParams(
            dimension_semantics=("arbitrary",), vmem_limit_bytes=VMEM_LIMIT),
        name="scatter",
    )(cnt, rows, slots)


def _tile_of(t, cnt_ref):
    s1, s2, s3, tot = _group_tiles(cnt_ref)
    tc = jnp.minimum(t, tot - 1)
    g = (tc >= s1).astype(jnp.int32) + (tc >= s2).astype(jnp.int32) + (tc >= s3).astype(jnp.int32)
    first = jnp.where(g == 0, 0, jnp.where(g == 1, s1, jnp.where(g == 2, s2, s3)))
    cnt = jnp.where(g == 0, cnt_ref[0], jnp.where(g == 1, cnt_ref[1], jnp.where(g == 2, cnt_ref[2], cnt_ref[3])))
    return tc, g, jnp.where(t < tot, cnt - (tc - first) * EXPERT_TILE, 0)


def _experts_kernel(cnt_ref, xs_ref, wgu_ref, wdn_ref, ys_ref):
    _, _, valid = _tile_of(pl.program_id(0), cnt_ref)

    @pl.when(valid <= 0)
    def _():
        ys_ref[...] = jnp.zeros_like(ys_ref)

    @pl.when(valid > 0)
    def _():
        xb = xs_ref[:, 0:D_MODEL].astype(BF16)
        wts = xs_ref[:, D_MODEL:]
        tot = jnp.zeros((EXPERT_TILE, D_MODEL), F32)
        for e in range(EPG):
            gu = _dot(xb, wgu_ref[0, e])
            gate = gu[:, :D_EXPERT]
            hmid = gate * jax.nn.sigmoid(gate) * gu[:, D_EXPERT:] * wts[:, e:e + 1]
            tot = tot + _dot(hmid.astype(BF16), wdn_ref[0, e])
        ys_ref[...] = tot


def _experts(cnt, xs, wgu, wdn):
    tiles = xs.shape[0] // EXPERT_TILE
    grp = lambda t, c: (_tile_of(t, c)[1], 0, 0, 0)
    return pl.pallas_call(
        _experts_kernel,
        grid_spec=pltpu.PrefetchScalarGridSpec(
            num_scalar_prefetch=1,
            grid=(tiles,),
            in_specs=[
                pl.BlockSpec((EXPERT_TILE, ROW_W), lambda t, c: (_tile_of(t, c)[0], 0)),
                pl.BlockSpec((1, EPG, D_MODEL, 2 * D_EXPERT), grp),
                pl.BlockSpec((1, EPG, D_EXPERT, D_MODEL), grp),
            ],
            out_specs=pl.BlockSpec((EXPERT_TILE, D_MODEL), lambda t, c: (t, 0)),
        ),
        out_shape=jax.ShapeDtypeStruct((tiles * EXPERT_TILE, D_MODEL), F32),
        compiler_params=pltpu.CompilerParams(
            dimension_semantics=("arbitrary",), vmem_limit_bytes=VMEM_LIMIT),
        name="experts",
    )(cnt, xs, wgu, wdn)


def _gather_kernel(x2_ref, slot_ref, gn_ref, ys_ref, y_ref, buf, sem):
    tm = x2_ref.shape[0]
    _row_copies(tm, lambda r: pltpu.make_async_copy(
        ys_ref.at[pl.ds(slot_ref[0, 0, r], 1), :], buf.at[pl.ds(r, 1), :], sem))
    y_ref[...] = _rms(x2_ref[...] + buf[...], gn_ref[...])


def _gather(x2, slots, gn, ys, tm):
    n = x2.shape[0]
    return pl.pallas_call(
        _gather_kernel,
        grid=(n // tm,),
        in_specs=[
            pl.BlockSpec((tm, D_MODEL), lambda i: (i, 0)),
            pl.BlockSpec((1, 1, tm), lambda i: (i, 0, 0), memory_space=pltpu.SMEM),
            pl.BlockSpec((1, D_MODEL), lambda i: (0, 0)),
            pl.BlockSpec(memory_space=pl.ANY),
        ],
        out_specs=pl.BlockSpec((tm, D_MODEL), lambda i: (i, 0)),
        out_shape=jax.ShapeDtypeStruct((n, D_MODEL), F32),
        scratch_shapes=[pltpu.VMEM((tm, D_MODEL), F32), pltpu.SemaphoreType.DMA(())],
        compiler_params=pltpu.CompilerParams(
            dimension_semantics=("arbitrary",), vmem_limit_bytes=VMEM_LIMIT),
        name="gather",
    )(x2, slots, gn, ys)


def _forget_lanes(f):
    z = lambda n: jnp.zeros(f.shape[:-1] + (n,), f.dtype)
    return jnp.concatenate([f, z(F_HI - HEADS), f, f, f, z(LANES - F_LO - HEADS)], axis=-1)


def _trunk(x, attend, state, w, tm_proj, tm_tok):
    b, s, _ = x.shape
    k, v, lf, qa, ka, mb, ncv, *built = _proj(
        x, state, w["gm"], w["w_proj"], w["bias_f"], w["cw"], w["gc"], tm_proj)
    if built:
        w["w_proj"] = built[0]
    o = attend(qa, ka, v)
    n = b * s
    x2, rows, slots, cnt = _mix(x.reshape(n, D_MODEL), o.reshape(n, ATTN_W), mb.reshape(n, CONV_W),
                                w["wo"], w["ga"], w["gf"], w["wr"], w["br"], tm_tok)
    cnt = cnt[:, 0]
    ys = _experts(cnt, _scatter(cnt, rows, slots, tm_tok), w["wgu"], w["wdn"])
    y = _gather(x2, slots, w["gn"], ys, tm_tok)
    return (y.reshape(b, s, D_MODEL), k.reshape(1, b, s, HEADS, HEAD_DIM), v.reshape(1, b, s, HEADS, HEAD_DIM),
            lf.transpose(0, 2, 1).reshape(1, b, s, HEADS), ncv.reshape(1, b, 2, CONV_W))


def kernel(x_prompt, x_sample, cache_k, cache_v, cache_logf, state_conv, norm_mix_g, w_in, b_forget, conv_w, norm_attn_g, norm_conv_g, w_out, norm_ffn_g, w_router_group, b_router_group, w_router_expert, b_router_expert, w_expert_gate_up, w_expert_down, norm_final_g):
    assert w_in.shape[0] == 1, "single-layer trunk"
    wr = jnp.concatenate(
        [w_router_group[0], jnp.zeros((D_MODEL, EPG - N_GROUPS), F32),
         w_router_expert[0].transpose(1, 0, 2).reshape(D_MODEL, N_GROUPS * EPG),
         jnp.zeros((D_MODEL, LANES - EPG - N_GROUPS * EPG), F32)], axis=1).astype(BF16)
    br = jnp.concatenate(
        [b_router_group[0], jnp.zeros((EPG - N_GROUPS,), F32), b_router_expert[0].reshape(-1),
         jnp.zeros((LANES - EPG - N_GROUPS * EPG,), F32)]).reshape(1, LANES)
    w = dict(
        gm=norm_mix_g[0].reshape(1, D_MODEL), w_proj=w_in[0].T, bias_f=_forget_lanes(b_forget[0]).reshape(1, LANES),
        cw=jnp.concatenate([conv_w[0], jnp.zeros((8 - conv_w.shape[1], CONV_W), F32)], axis=0),
        gc=norm_conv_g[0].reshape(1, CONV_W), wo=w_out[0].astype(BF16), ga=norm_attn_g[0].reshape(1, ATTN_W),
        gf=norm_ffn_g[0].reshape(1, D_MODEL), wr=wr, br=br,
        wgu=w_expert_gate_up[0], wdn=w_expert_down[0],
        gn=norm_final_g.reshape(1, D_MODEL))

    bp, sp, _ = x_prompt.shape
    bs, ss, _ = x_sample.shape
    plen = cache_k.shape[2]
    yp, kp, vp, lfp, cvp = _trunk(x_prompt, _attention, jnp.zeros((bp, 8, CONV_W), F32), w, 512, 512)

    ckt = cache_k[0].transpose(0, 2, 3, 1).reshape(bs, ATTN_W, plen)
    cvt = cache_v[0].transpose(0, 2, 3, 1).reshape(bs, ATTN_W, plen)
    clft = cache_logf[0].transpose(0, 2, 1)
    st = jnp.concatenate([jnp.zeros((bs, 6, CONV_W), F32), state_conv[0]], axis=1)
    attend_s = lambda qa, ka, v: _attention_cache(qa, ka, v, ckt, cvt, clft)
    ys, ks, vs, lfs, cvs = _trunk(x_sample, attend_s, st, w, ss, bs * ss)
    return (yp, ys, kp, vp, lfp, cvp, ks, vs, lfs, cvs)
```

```python
import functools
import math

import jax
import jax.numpy as jnp
from jax import lax
from jax.experimental import pallas as pl
from jax.experimental.pallas import tpu as pltpu

F32 = jnp.float32
BF16 = jnp.bfloat16

D_MODEL = 1024
HEADS = 8
HEAD_DIM = 64
ATTN_W = HEADS * HEAD_DIM
CONV_W = 512
N_GROUPS = 4
EPG = 8
D_EXPERT = 256
EPS = 1e-6
LOG2E = math.log2(math.e)
Q_SCALE = HEAD_DIM ** -0.5 * LOG2E

LANES = 128
COL_Q, COL_K, COL_V, COL_F, COL_BG, COL_CG, COL_H = 0, 512, 1024, 1536, 1664, 2176, 2688
PROJ_PAD = 3200
W_SECTIONS = ((0, COL_Q), (512, COL_K), (1024, COL_V), (1544, COL_BG), (2056, COL_CG), (2568, COL_H))
W_F_ROW = 1536
F_OUT, F_HI, F_MID, F_LO = 0, 64, 72, 80

GROUP_SHIFT = 20
ROW_W = D_MODEL + LANES
TILE_SHIFT = 8
EXPERT_TILE = 1 << TILE_SHIFT

VMEM_LIMIT = 60 * 1024 * 1024


def _dot(a, b):
    return jnp.dot(a, b, preferred_element_type=F32)


def _dot_nt(a, b):
    return lax.dot_general(a, b, (((1,), (1,)), ((), ())), preferred_element_type=F32)


def _rms(x, g):
    return x * lax.rsqrt(jnp.mean(x * x, axis=-1, keepdims=True) + EPS) * g


def _log_sigmoid(x):
    return jnp.minimum(x, 0.0) - jnp.log(1.0 + jnp.exp(-jnp.abs(x)))


def _pad_rows(x, n):
    return jnp.concatenate([x, jnp.zeros((n - x.shape[0], x.shape[1]), x.dtype)], axis=0)


def _split3(x):
    a1 = x.astype(BF16)
    r1 = x - a1.astype(F32)
    a2 = r1.astype(BF16)
    a3 = (r1 - a2.astype(F32)).astype(BF16)
    return a1, a2, a3


def _cumsum_rows(x):
    n = x.shape[0]
    r = lax.broadcasted_iota(jnp.int32, (n, n), 0)
    c = lax.broadcasted_iota(jnp.int32, (n, n), 1)
    tri = jnp.where(r >= c, 1.0, 0.0).astype(BF16)
    parts = _dot(tri, jnp.concatenate(_split3(x), axis=1))
    return parts[:, :LANES] + parts[:, LANES:2 * LANES] + parts[:, 2 * LANES:]


def _key_bias_lanes(cneg):
    lane = lax.broadcasted_iota(jnp.int32, cneg.shape, 1)
    hi = cneg.astype(BF16).astype(F32)
    r1 = cneg - hi
    mid = r1.astype(BF16).astype(F32)
    lo = r1 - mid
    out = jnp.where(lane < F_MID, hi, jnp.where(lane < F_LO, mid, lo))
    return jnp.where((lane >= F_HI) & (lane < F_LO + HEADS), out, 0.0)


def _head_block(x, h):
    blk = x[:, (h // 2) * LANES:(h // 2 + 1) * LANES]
    return pltpu.roll(blk, HEAD_DIM, 1) if h % 2 else blk


def _key_operand(k, bias_lanes, h):
    lane = lax.broadcasted_iota(jnp.int32, bias_lanes.shape, 1)
    return jnp.where(lane < HEAD_DIM, _head_block(k, h), bias_lanes).astype(BF16)


def _prepare_weight(wt_hbm, w_ref, stage, sem):
    for r0, c0 in W_SECTIONS:
        cp = pltpu.make_async_copy(wt_hbm.at[pl.ds(r0, ATTN_W), :], stage, sem)
        cp.start()
        cp.wait()
        w_ref[:, c0:c0 + ATTN_W] = stage[...].T.astype(BF16)
    cp = pltpu.make_async_copy(wt_hbm.at[pl.ds(W_F_ROW, HEADS), :], stage.at[pl.ds(0, HEADS), :], sem)
    cp.start()
    cp.wait()
    f = stage[0:HEADS, :]
    z = lambda n: jnp.zeros((n, D_MODEL), F32)
    fblk = jnp.concatenate([f, z(F_HI - HEADS), f, f, f, z(LANES - F_LO - HEADS)], axis=0)
    w_ref[:, COL_F:COL_F + LANES] = fblk.T.astype(BF16)


def _proj_kernel(prepare, x_ref, st_ref, gm_ref, w_in_ref, bf_ref, cw_ref, gc_ref,
                 k_ref, v_ref, lf_ref, qa_ref, ka_ref, mb_ref, ncv_ref, *rest):
    i = pl.program_id(1)
    nb, tm, _ = x_ref.shape
    if prepare:
        w_ref, ubuf, ccar, stage, sem = rest

        @pl.when((pl.program_id(0) == 0) & (i == 0))
        def _():
            _prepare_weight(w_in_ref, w_ref, stage, sem)
    else:
        w_ref = w_in_ref
        ubuf, ccar = rest

    @pl.when(i == 0)
    def _():
        ccar[...] = jnp.zeros_like(ccar)
        ubuf[:, 0:8, :] = st_ref[...]

    @pl.when(i > 0)
    def _():
        ubuf[:, 0:8, :] = ubuf[:, tm:tm + 8, :]

    lane = lax.broadcasted_iota(jnp.int32, (tm, LANES), 1)
    pad = (-tm) % LANES
    for bb in range(nb):
        xn = _rms(x_ref[bb], gm_ref[...]).astype(BF16)
        proj = lambda c0, n: _dot(xn, w_ref[:, c0:c0 + n])

        lfb = _log_sigmoid(proj(COL_F, LANES) + bf_ref[...])
        lf_ref[bb] = (_pad_rows(lfb, tm + pad) if pad else lfb).T[F_OUT:F_OUT + HEADS, 0:tm]
        c = _cumsum_rows(lfb) + ccar[bb, 0:1, :]
        ccar[bb] = jnp.broadcast_to(c[tm - 1:tm, :], ccar.shape[1:])
        bias_lanes = _key_bias_lanes(c * (-LOG2E))
        k = proj(COL_K, ATTN_W)
        k_ref[bb] = k
        for h in range(HEADS):
            ka_ref[bb, h] = _key_operand(k, bias_lanes, h)
        q = proj(COL_Q, ATTN_W)
        for h in range(HEADS):
            sel = jnp.where((lane == F_HI + h) | (lane == F_MID + h) | (lane == F_LO + h), 1.0, 0.0)
            qa_ref[bb, h] = jnp.where(lane < HEAD_DIM, _head_block(q, h) * Q_SCALE, sel).astype(BF16)
        v_ref[bb] = proj(COL_V, ATTN_W)

        u = proj(COL_CG, CONV_W) * proj(COL_H, CONV_W)
        ubuf[bb, 8:8 + tm, :] = u
        y = (cw_ref[0:1, :] * ubuf[bb, 6:6 + tm, :] + cw_ref[1:2, :] * ubuf[bb, 7:7 + tm, :]
             + cw_ref[2:3, :] * u)
        ob = proj(COL_BG, CONV_W) * y
        mb_ref[bb] = _rms(ob, gc_ref[...]).astype(BF16)
        ncv_ref[bb] = u[tm - 2:tm, :]


def _proj(x, state, gm, w, bias_f, cw, gc, tm, nb):
    b, s, _ = x.shape
    prepare = w.dtype == F32
    full = lambda shape: pl.BlockSpec(shape, lambda bi, i: (0,) * len(shape))
    out_specs = [
        pl.BlockSpec((nb, tm, ATTN_W), lambda bi, i: (bi, i, 0)),
        pl.BlockSpec((nb, tm, ATTN_W), lambda bi, i: (bi, i, 0)),
        pl.BlockSpec((nb, HEADS, tm), lambda bi, i: (bi, 0, i)),
        pl.BlockSpec((nb, HEADS, tm, LANES), lambda bi, i: (bi, 0, i, 0)),
        pl.BlockSpec((nb, HEADS, tm, LANES), lambda bi, i: (bi, 0, i, 0)),
        pl.BlockSpec((nb, tm, CONV_W), lambda bi, i: (bi, i, 0)),
        pl.BlockSpec((nb, 2, CONV_W), lambda bi, i: (bi, 0, 0)),
    ]
    out_shape = [
        jax.ShapeDtypeStruct((b, s, ATTN_W), F32),
        jax.ShapeDtypeStruct((b, s, ATTN_W), F32),
        jax.ShapeDtypeStruct((b, HEADS, s), F32),
        jax.ShapeDtypeStruct((b, HEADS, s, LANES), BF16),
        jax.ShapeDtypeStruct((b, HEADS, s, LANES), BF16),
        jax.ShapeDtypeStruct((b, s, CONV_W), BF16),
        jax.ShapeDtypeStruct((b, 2, CONV_W), F32),
    ]
    scratch = [pltpu.VMEM((nb, tm + 8, CONV_W), F32), pltpu.VMEM((nb, 8, LANES), F32)]
    if prepare:
        out_specs.append(full((D_MODEL, PROJ_PAD)))
        out_shape.append(jax.ShapeDtypeStruct((D_MODEL, PROJ_PAD), BF16))
        scratch += [pltpu.VMEM((ATTN_W, D_MODEL), F32), pltpu.SemaphoreType.DMA(())]
    return pl.pallas_call(
        functools.partial(_proj_kernel, prepare),
        grid=(b // nb, s // tm),
        in_specs=[
            pl.BlockSpec((nb, tm, D_MODEL), lambda bi, i: (bi, i, 0)),
            pl.BlockSpec((nb, 8, CONV_W), lambda bi, i: (bi, 0, 0)),
            full((1, D_MODEL)),
            pl.BlockSpec(memory_space=pl.ANY) if prepare else full((D_MODEL, PROJ_PAD)),
            full((1, LANES)),
            full((8, CONV_W)),
            full((1, CONV_W)),
        ],
        out_specs=out_specs,
        out_shape=out_shape,
        scratch_shapes=scratch,
        compiler_params=pltpu.CompilerParams(
            dimension_semantics=("arbitrary", "arbitrary"), vmem_limit_bytes=VMEM_LIMIT),
        name="proj",
    )(x, state, gm, w, bias_f, cw, gc)


ATT_BLK = 256


def _attn_kernel(qa_ref, ka_ref, v_ref, o_ref, sbuf):
    blk = ATT_BLK
    vt = v_ref[0].T.astype(BF16)
    r = lax.broadcasted_iota(jnp.int32, (blk, blk), 0)
    c = lax.broadcasted_iota(jnp.int32, (blk, blk), 1)
    causal = r <= c
    units = [(iq, hh) for iq in range(ka_ref.shape[2] // blk) for hh in range(2)]

    def scores(u):
        iq, hh = units[u]
        n = (iq + 1) * blk
        sbuf[u % nbuf, 0:n, :] = _dot_nt(ka_ref[0, hh, 0:n, :], qa_ref[0, hh, iq * blk:n, :])

    nbuf = sbuf.shape[0]
    for u in range(nbuf - 1):
        scores(u)
    outs = []
    for u, (iq, hh) in enumerate(units):
        if u + nbuf - 1 < len(units):
            scores(u + nbuf - 1)
        noff = iq * blk
        rows = slice(hh * HEAD_DIM, (hh + 1) * HEAD_DIM)
        s_d = jnp.where(causal, sbuf[u % nbuf, noff:noff + blk, :], -jnp.inf)
        m = jnp.max(s_d, axis=0, keepdims=True)
        if iq > 0:
            m = jnp.maximum(m, jnp.max(sbuf[u % nbuf, 0:noff, :], axis=0, keepdims=True))
        p_d = jnp.exp2(s_d - m)
        l = jnp.sum(p_d, axis=0, keepdims=True)
        acc = _dot(vt[rows, noff:noff + blk], p_d.astype(BF16))
        if iq > 0:
            p_o = jnp.exp2(sbuf[u % nbuf, 0:noff, :] - m)
            l = l + jnp.sum(p_o, axis=0, keepdims=True)
            acc = acc + _dot(vt[rows, 0:noff], p_o.astype(BF16))
        outs.append(acc / l)
        if hh == 1:
            o_ref[0, noff:noff + blk, :] = jnp.concatenate(outs, axis=0).T
            outs = []


def _attention(qa, ka, v):
    b, _, s, _ = qa.shape
    return pl.pallas_call(
        _attn_kernel,
        grid=(b, HEADS // 2),
        in_specs=[
            pl.BlockSpec((1, 2, s, LANES), lambda bi, p: (bi, p, 0, 0)),
            pl.BlockSpec((1, 2, s, LANES), lambda bi, p: (bi, p, 0, 0)),
            pl.BlockSpec((1, s, LANES), lambda bi, p: (bi, 0, p)),
        ],
        out_specs=pl.BlockSpec((1, s, LANES), lambda bi, p: (bi, 0, p)),
        out_shape=jax.ShapeDtypeStruct((b, s, ATTN_W), F32),
        scratch_shapes=[pltpu.VMEM((3, s, ATT_BLK), F32)],
        compiler_params=pltpu.CompilerParams(
            dimension_semantics=("arbitrary", "arbitrary"), vmem_limit_bytes=VMEM_LIMIT),
        name="attention",
    )(qa, ka, v)


Q_PAD = 128


def _cumsum_lanes(x):
    n = x.shape[1]
    r = lax.broadcasted_iota(jnp.int32, (n, n), 0)
    c = lax.broadcasted_iota(jnp.int32, (n, n), 1)
    tri = jnp.where(r <= c, 1.0, 0.0).astype(BF16)
    terms = jnp.concatenate([a.astype(F32) for a in _split3(x)], axis=0).astype(BF16)
    parts = _dot(terms, tri)
    rows = x.shape[0]
    return parts[0:rows] + parts[rows:2 * rows] + parts[2 * rows:]


def _attn_cache_kernel(qa_ref, ka_ref, v_ref, ckt_ref, cvt_ref, clft_ref, o_ref):
    t = qa_ref.shape[2]
    plen = ckt_ref.shape[2]
    cinc = _cumsum_lanes(clft_ref[0])
    bias = (cinc - cinc[:, plen - 1:plen]) * (-LOG2E)
    lane = lax.broadcasted_iota(jnp.int32, (t, LANES), 1)
    rq = lax.broadcasted_iota(jnp.int32, (t, Q_PAD), 0)
    ck = lax.broadcasted_iota(jnp.int32, (t, Q_PAD), 1)
    new_ok = ck <= rq

    pairs = []
    for p in range(HEADS // 2):
        rows = slice(p * LANES, (p + 1) * LANES)
        kt = ckt_ref[0, rows, :].astype(BF16)
        vt = cvt_ref[0, rows, :].astype(BF16)
        vn = _pad_rows(v_ref[0][:, rows], Q_PAD).astype(BF16)
        res = []
        for hh in range(2):
            h = 2 * p + hh
            qa = qa_ref[0, h]
            qh = jnp.where(lane < HEAD_DIM, qa.astype(F32), 0.0)
            qh = (pltpu.roll(qh, HEAD_DIM, 1) if hh else qh).astype(BF16)
            s_past = _dot(qh, kt) + bias[h:h + 1, :]
            s_new = jnp.where(new_ok, _dot_nt(qa, _pad_rows(ka_ref[0, h], Q_PAD)), -jnp.inf)
            m = jnp.maximum(jnp.max(s_past, axis=1, keepdims=True), jnp.max(s_new, axis=1, keepdims=True))
            p_past = jnp.exp2(s_past - m)
            p_new = jnp.exp2(s_new - m)
            l = jnp.sum(p_past, axis=1, keepdims=True) + jnp.sum(p_new, axis=1, keepdims=True)
            acc = _dot_nt(p_past.astype(BF16), vt) + _dot(p_new.astype(BF16), vn)
            res.append(acc / l)
        pairs.append(jnp.where(lane < HEAD_DIM, res[0], res[1]))
    o_ref[0] = jnp.concatenate(pairs, axis=1)


def _attention_cache(qa, ka, v, ckt, cvt, clft):
    b, _, t, _ = qa.shape
    plen = ckt.shape[2]
    return pl.pallas_call(
        _attn_cache_kernel,
        grid=(b,),
        in_specs=[
            pl.BlockSpec((1, HEADS, t, LANES), lambda bi: (bi, 0, 0, 0)),
            pl.BlockSpec((1, HEADS, t, LANES), lambda bi: (bi, 0, 0, 0)),
            pl.BlockSpec((1, t, ATTN_W), lambda bi: (bi, 0, 0)),
            pl.BlockSpec((1, ATTN_W, plen), lambda bi: (bi, 0, 0)),
            pl.BlockSpec((1, ATTN_W, plen), lambda bi: (bi, 0, 0)),
            pl.BlockSpec((1, HEADS, plen), lambda bi: (bi, 0, 0)),
        ],
        out_specs=pl.BlockSpec((1, t, ATTN_W), lambda bi: (bi, 0, 0)),
        out_shape=jax.ShapeDtypeStruct((b, t, ATTN_W), F32),
        compiler_params=pltpu.CompilerParams(
            dimension_semantics=("arbitrary",), vmem_limit_bytes=VMEM_LIMIT),
        name="attention_cache",
    )(qa, ka, v, ckt, cvt, clft)


def _mix_kernel(x_ref, o_ref, mb_ref, wo_ref, ga_ref, gf_ref, wr_ref, br_ref,
                x2_ref, rows_ref, slot_ref, cnt_ref, carry):
    i = pl.program_id(0)
    tm = slot_ref.shape[2]
    nsub = x_ref.shape[0] // tm
    row = lax.broadcasted_iota(jnp.int32, (EPG, tm), 0)
    r2 = lax.broadcasted_iota(jnp.int32, (tm, tm), 0)
    c2 = lax.broadcasted_iota(jnp.int32, (tm, tm), 1)
    earlier = jnp.where(r2 < c2, 1.0, 0.0).astype(BF16)

    @pl.when(i == 0)
    def _():
        carry[...] = jnp.zeros_like(carry)

    total = carry[:, 0:1]
    for j in range(nsub):
        rs = slice(j * tm, (j + 1) * tm)
        oa = _rms(o_ref[rs, :], ga_ref[...]).astype(BF16)
        merged = jnp.concatenate([oa, mb_ref[rs, :]], axis=1)
        x2 = x_ref[rs, :] + _dot(merged, wo_ref[...])
        x2_ref[rs, :] = x2
        xb = _rms(x2, gf_ref[...]).astype(BF16)
        rows_ref[rs, 0:D_MODEL] = xb.astype(F32)

        lt = (_dot(xb, wr_ref[...]) + br_ref[...]).T
        lg = jnp.where(row < N_GROUPS, lt[0:EPG], -jnp.inf)
        eg = jnp.exp(lg - jnp.max(lg, axis=0, keepdims=True))
        pg = eg / jnp.sum(eg, axis=0, keepdims=True)
        pg_top = jnp.max(pg, axis=0, keepdims=True)
        g_top = jnp.min(jnp.where(pg == pg_top, row, EPG), axis=0, keepdims=True)

        le = lt[EPG:2 * EPG]
        for g in range(1, N_GROUPS):
            le = jnp.where(g_top == g, lt[EPG * (g + 1):EPG * (g + 2)], le)
        ee = jnp.exp(le - jnp.max(le, axis=0, keepdims=True))
        pe = ee / jnp.sum(ee, axis=0, keepdims=True)
        v1 = jnp.max(pe, axis=0, keepdims=True)
        i1 = jnp.min(jnp.where(pe == v1, row, EPG), axis=0, keepdims=True)
        pe2 = jnp.where(row == i1, -1.0, pe)
        v2 = jnp.max(pe2, axis=0, keepdims=True)
        i2 = jnp.min(jnp.where(pe2 == v2, row, EPG), axis=0, keepdims=True)
        den = v1 + v2
        wg = jnp.where(row == i1, v1 / den * pg_top, 0.0) + jnp.where(row == i2, v2 / den * pg_top, 0.0)
        rows_ref[rs, D_MODEL:] = jnp.concatenate([wg, jnp.zeros((LANES - EPG, tm), F32)], axis=0).T

        onehot = jnp.where(row == g_top, 1.0, 0.0)
        rank = jnp.sum(onehot * (_dot(onehot.astype(BF16), earlier) + total), axis=0, keepdims=True)
        slot_ref[i * nsub + j] = g_top * (1 << GROUP_SHIFT) + rank.astype(jnp.int32)
        total = total + jnp.sum(onehot, axis=1, keepdims=True)

    carry[...] = jnp.broadcast_to(total, carry.shape)
    cnt_ref[...] = carry[...].astype(jnp.int32)

    @pl.when(i == pl.num_programs(0) - 1)
    def _():
        padded = jnp.floor((total + (EXPERT_TILE - 1)) * (1.0 / EXPERT_TILE)) * EXPERT_TILE
        o1 = padded[0:1]
        o2 = o1 + padded[1:2]
        o3 = o2 + padded[2:3]
        code = slot_ref[...]
        g = code >> GROUP_SHIFT
        first = jnp.where(g == 0, 0.0, jnp.where(g == 1, o1, jnp.where(g == 2, o2, o3)))
        slot_ref[...] = first.astype(jnp.int32) + (code & ((1 << GROUP_SHIFT) - 1))


def _mix(x, o, mb, wo, ga, gf, wr, br, tm, nsub):
    n = x.shape[0]
    assert n < (1 << GROUP_SHIFT)
    tb = tm * nsub
    full = lambda shape: pl.BlockSpec(shape, lambda i: (0,) * len(shape))
    return pl.pallas_call(
        _mix_kernel,
        grid=(n // tb,),
        in_specs=[
            pl.BlockSpec((tb, D_MODEL), lambda i: (i, 0)),
            pl.BlockSpec((tb, ATTN_W), lambda i: (i, 0)),
            pl.BlockSpec((tb, CONV_W), lambda i: (i, 0)),
            full((D_MODEL, D_MODEL)),
            full((1, ATTN_W)),
            full((1, D_MODEL)),
            full((D_MODEL, LANES)),
            full((1, LANES)),
        ],
        out_specs=[
            pl.BlockSpec((tb, D_MODEL), lambda i: (i, 0)),
            pl.BlockSpec((tb, ROW_W), lambda i: (i, 0)),
            full((n // tm, 1, tm)),
            full((EPG, LANES)),
        ],
        out_shape=[
            jax.ShapeDtypeStruct((n, D_MODEL), F32),
            jax.ShapeDtypeStruct((n, ROW_W), F32),
            jax.ShapeDtypeStruct((n // tm, 1, tm), jnp.int32),
            jax.ShapeDtypeStruct((EPG, LANES), jnp.int32),
        ],
        scratch_shapes=[pltpu.VMEM((EPG, LANES), F32)],
        compiler_params=pltpu.CompilerParams(
            dimension_semantics=("arbitrary",), vmem_limit_bytes=VMEM_LIMIT),
        name="mix",
    )(x, o, mb, wo, ga, gf, wr, br)


def _sorted_tiles(n):
    return n // EXPERT_TILE + N_GROUPS


def _group_tiles(cnt_ref):
    nt = [(cnt_ref[g] + (EXPERT_TILE - 1)) >> TILE_SHIFT for g in range(N_GROUPS)]
    s1 = nt[0]
    s2 = s1 + nt[1]
    s3 = s2 + nt[2]
    return s1, s2, s3, s3 + nt[3]


def _row_copies(n, copy_of_row):
    for r in range(n):
        copy_of_row(r).start(priority=r % 2)
    for r in range(n):
        copy_of_row(r).wait()


def _scatter_kernel(cnt_ref, rows_ref, slot_ref, xs_ref, zeros, sem, zsem):
    tm = rows_ref.shape[0]

    @pl.when(pl.program_id(0) == 0)
    def _():
        zeros[...] = jnp.zeros_like(zeros)
        s1, s2, s3, tot = _group_tiles(cnt_ref)
        ends = (s1, s2, s3, tot)

        def fill(tile):
            return pltpu.make_async_copy(zeros, xs_ref.at[pl.ds(tile * EXPERT_TILE, EXPERT_TILE), :], zsem)

        for g in range(N_GROUPS):
            partial = (cnt_ref[g] & (EXPERT_TILE - 1)) != 0

            @pl.when(partial)
            def _():
                fill(ends[g] - 1).start()
                fill(ends[g] - 1).wait()

        def fill_tail(tile, c):
            fill(tile).start()
            fill(tile).wait()
            return c

        lax.fori_loop(tot, xs_ref.shape[0] // EXPERT_TILE, fill_tail, 0)

    _row_copies(tm, lambda r: pltpu.make_async_copy(
        rows_ref.at[pl.ds(r, 1), :], xs_ref.at[pl.ds(slot_ref[0, 0, r], 1), :], sem))


def _scatter(cnt, rows, slots, tm):
    n = rows.shape[0]
    return pl.pallas_call(
        _scatter_kernel,
        grid_spec=pltpu.PrefetchScalarGridSpec(
            num_scalar_prefetch=1,
            grid=(n // tm,),
            in_specs=[
                pl.BlockSpec((tm, ROW_W), lambda i, c: (i, 0)),
                pl.BlockSpec((1, 1, tm), lambda i, c: (i, 0, 0), memory_space=pltpu.SMEM),
            ],
            out_specs=pl.BlockSpec(memory_space=pl.ANY),
            scratch_shapes=[pltpu.VMEM((EXPERT_TILE, ROW_W), F32),
                            pltpu.SemaphoreType.DMA(()), pltpu.SemaphoreType.DMA(())],
        ),
        out_shape=jax.ShapeDtypeStruct((_sorted_tiles(n) * EXPERT_TILE, ROW_W), F32),
        compiler_params=pltpu.CompilerParams(
            dimension_semantics=("arbitrary",), vmem_limit_bytes=VMEM_LIMIT),
        name="scatter",
    )(cnt, rows, slots)


def _tile_of(t, cnt_ref):
    s1, s2, s3, tot = _group_tiles(cnt_ref)
    tc = jnp.minimum(t, tot - 1)
    g = (tc >= s1).astype(jnp.int32) + (tc >= s2).astype(jnp.int32) + (tc >= s3).astype(jnp.int32)
    first = jnp.where(g == 0, 0, jnp.where(g == 1, s1, jnp.where(g == 2, s2, s3)))
    cnt = jnp.where(g == 0, cnt_ref[0], jnp.where(g == 1, cnt_ref[1], jnp.where(g == 2, cnt_ref[2], cnt_ref[3])))
    return tc, g, jnp.where(t < tot, cnt - (tc - first) * EXPERT_TILE, 0)


def _experts_kernel(cnt_ref, xs_ref, wgu_ref, wdn_ref, ys_ref):
    _, _, valid = _tile_of(pl.program_id(0), cnt_ref)

    @pl.when(valid <= 0)
    def _():
        ys_ref[...] = jnp.zeros_like(ys_ref)

    @pl.when(valid > 0)
    def _():
        xb = xs_ref[:, 0:D_MODEL].astype(BF16)
        wts = xs_ref[:, D_MODEL:]
        tot = jnp.zeros((EXPERT_TILE, D_MODEL), F32)
        for e in range(EPG):
            gu = _dot(xb, wgu_ref[0, e])
            gate = gu[:, :D_EXPERT]
            hmid = gate * jax.nn.sigmoid(gate) * gu[:, D_EXPERT:] * wts[:, e:e + 1]
            tot = tot + _dot(hmid.astype(BF16), wdn_ref[0, e])
        ys_ref[...] = tot


def _experts(cnt, xs, wgu, wdn):
    tiles = xs.shape[0] // EXPERT_TILE
    grp = lambda t, c: (_tile_of(t, c)[1], 0, 0, 0)
    return pl.pallas_call(
        _experts_kernel,
        grid_spec=pltpu.PrefetchScalarGridSpec(
            num_scalar_prefetch=1,
            grid=(tiles,),
            in_specs=[
                pl.BlockSpec((EXPERT_TILE, ROW_W), lambda t, c: (_tile_of(t, c)[0], 0)),
                pl.BlockSpec((1, EPG, D_MODEL, 2 * D_EXPERT), grp),
                pl.BlockSpec((1, EPG, D_EXPERT, D_MODEL), grp),
            ],
            out_specs=pl.BlockSpec((EXPERT_TILE, D_MODEL), lambda t, c: (t, 0)),
        ),
        out_shape=jax.ShapeDtypeStruct((tiles * EXPERT_TILE, D_MODEL), F32),
        compiler_params=pltpu.CompilerParams(
            dimension_semantics=("arbitrary",), vmem_limit_bytes=VMEM_LIMIT),
        name="experts",
    )(cnt, xs, wgu, wdn)


def _gather_kernel(x2_ref, slot_ref, gn_ref, ys_ref, y_ref, buf, sem):
    tm = x2_ref.shape[0]
    _row_copies(tm, lambda r: pltpu.make_async_copy(
        ys_ref.at[pl.ds(slot_ref[0, 0, r], 1), :], buf.at[pl.ds(r, 1), :], sem))
    y_ref[...] = _rms(x2_ref[...] + buf[...], gn_ref[...])


def _gather(x2, slots, gn, ys, tm):
    n = x2.shape[0]
    return pl.pallas_call(
        _gather_kernel,
        grid=(n // tm,),
        in_specs=[
            pl.BlockSpec((tm, D_MODEL), lambda i: (i, 0)),
            pl.BlockSpec((1, 1, tm), lambda i: (i, 0, 0), memory_space=pltpu.SMEM),
            pl.BlockSpec((1, D_MODEL), lambda i: (0, 0)),
            pl.BlockSpec(memory_space=pl.ANY),
        ],
        out_specs=pl.BlockSpec((tm, D_MODEL), lambda i: (i, 0)),
        out_shape=jax.ShapeDtypeStruct((n, D_MODEL), F32),
        scratch_shapes=[pltpu.VMEM((tm, D_MODEL), F32), pltpu.SemaphoreType.DMA(())],
        compiler_params=pltpu.CompilerParams(
            dimension_semantics=("arbitrary",), vmem_limit_bytes=VMEM_LIMIT),
        name="gather",
    )(x2, slots, gn, ys)


def _forget_lanes(f):
    z = lambda n: jnp.zeros(f.shape[:-1] + (n,), f.dtype)
    return jnp.concatenate([f, z(F_HI - HEADS), f, f, f, z(LANES - F_LO - HEADS)], axis=-1)


def _trunk(x, attend, state, w, tm_proj, nb_proj, tm_tok, nsub_tok):
    b, s, _ = x.shape
    k, v, lf, qa, ka, mb, ncv, *built = _proj(
        x, state, w["gm"], w["w_proj"], w["bias_f"], w["cw"], w["gc"], tm_proj, nb_proj)
    if built:
        w["w_proj"] = built[0]
    o = attend(qa, ka, v)
    n = b * s
    x2, rows, slots, cnt = _mix(x.reshape(n, D_MODEL), o.reshape(n, ATTN_W), mb.reshape(n, CONV_W),
                                w["wo"], w["ga"], w["gf"], w["wr"], w["br"], tm_tok, nsub_tok)
    cnt = cnt[:, 0]
    ys = _experts(cnt, _scatter(cnt, rows, slots, tm_tok), w["wgu"], w["wdn"])
    y = _gather(x2, slots, w["gn"], ys, tm_tok)
    return (y.reshape(b, s, D_MODEL), k.reshape(1, b, s, HEADS, HEAD_DIM), v.reshape(1, b, s, HEADS, HEAD_DIM),
            lf.transpose(0, 2, 1).reshape(1, b, s, HEADS), ncv.reshape(1, b, 2, CONV_W))


def kernel(x_prompt, x_sample, cache_k, cache_v, cache_logf, state_conv, norm_mix_g, w_in, b_forget, conv_w, norm_attn_g, norm_conv_g, w_out, norm_ffn_g, w_router_group, b_router_group, w_router_expert, b_router_expert, w_expert_gate_up, w_expert_down, norm_final_g):
    assert w_in.shape[0] == 1, "single-layer trunk"
    wr = jnp.concatenate(
        [w_router_group[0], jnp.zeros((D_MODEL, EPG - N_GROUPS), F32),
         w_router_expert[0].transpose(1, 0, 2).reshape(D_MODEL, N_GROUPS * EPG),
         jnp.zeros((D_MODEL, LANES - EPG - N_GROUPS * EPG), F32)], axis=1).astype(BF16)
    br = jnp.concatenate(
        [b_router_group[0], jnp.zeros((EPG - N_GROUPS,), F32), b_router_expert[0].reshape(-1),
         jnp.zeros((LANES - EPG - N_GROUPS * EPG,), F32)]).reshape(1, LANES)
    w = dict(
        gm=norm_mix_g[0].reshape(1, D_MODEL), w_proj=w_in[0].T, bias_f=_forget_lanes(b_forget[0]).reshape(1, LANES),
        cw=jnp.concatenate([conv_w[0], jnp.zeros((8 - conv_w.shape[1], CONV_W), F32)], axis=0),
        gc=norm_conv_g[0].reshape(1, CONV_W), wo=w_out[0].astype(BF16), ga=norm_attn_g[0].reshape(1, ATTN_W),
        gf=norm_ffn_g[0].reshape(1, D_MODEL), wr=wr, br=br,
        wgu=w_expert_gate_up[0], wdn=w_expert_down[0],
        gn=norm_final_g.reshape(1, D_MODEL))

    bp, sp, _ = x_prompt.shape
    bs, ss, _ = x_sample.shape
    plen = cache_k.shape[2]
    yp, kp, vp, lfp, cvp = _trunk(x_prompt, _attention, jnp.zeros((bp, 8, CONV_W), F32), w, 512, 2, 512, 2)

    ckt = cache_k[0].transpose(0, 2, 3, 1).reshape(bs, ATTN_W, plen)
    cvt = cache_v[0].transpose(0, 2, 3, 1).reshape(bs, ATTN_W, plen)
    clft = cache_logf[0].transpose(0, 2, 1)
    st = jnp.concatenate([jnp.zeros((bs, 6, CONV_W), F32), state_conv[0]], axis=1)
    attend_s = lambda qa, ka, v: _attention_cache(qa, ka, v, ckt, cvt, clft)
    ys, ks, vs, lfs, cvs = _trunk(x_sample, attend_s, st, w, ss, 2, bs * ss, 1)
    return (yp, ys, kp, vp, lfp, cvp, ks, vs, lfs, cvs)
```

```python
import functools
import math

import jax
import jax.numpy as jnp
from jax import lax
from jax.experimental import pallas as pl
from jax.experimental.pallas import tpu as pltpu

F32 = jnp.float32
BF16 = jnp.bfloat16

D_MODEL = 1024
HEADS = 8
HEAD_DIM = 64
ATTN_W = HEADS * HEAD_DIM
CONV_W = 512
N_GROUPS = 4
EPG = 8
D_EXPERT = 256
EPS = 1e-6
LOG2E = math.log2(math.e)
Q_SCALE = HEAD_DIM ** -0.5 * LOG2E

LANES = 128
COL_Q, COL_K, COL_V, COL_F, COL_BG, COL_CG, COL_H = 0, 512, 1024, 1536, 1664, 2176, 2688
PROJ_PAD = 3200
W_SECTIONS = ((0, COL_Q), (512, COL_K), (1024, COL_V), (1544, COL_BG), (2056, COL_CG), (2568, COL_H))
W_F_ROW = 1536
F_OUT, F_HI, F_MID, F_LO = 0, 64, 72, 80

GROUP_SHIFT = 20
ROW_W = 2 * D_MODEL + LANES
META_ID = EPG
TILE_SHIFT = 8
EXPERT_TILE = 1 << TILE_SHIFT

VMEM_LIMIT = 60 * 1024 * 1024


def _dot(a, b):
    return jnp.dot(a, b, preferred_element_type=F32)


def _dot_nt(a, b):
    return lax.dot_general(a, b, (((1,), (1,)), ((), ())), preferred_element_type=F32)


def _rms(x, g):
    return x * lax.rsqrt(jnp.mean(x * x, axis=-1, keepdims=True) + EPS) * g


def _log_sigmoid(x):
    return jnp.minimum(x, 0.0) - jnp.log(1.0 + jnp.exp(-jnp.abs(x)))


def _pad_rows(x, n):
    return jnp.concatenate([x, jnp.zeros((n - x.shape[0], x.shape[1]), x.dtype)], axis=0)


def _split3(x):
    a1 = x.astype(BF16)
    r1 = x - a1.astype(F32)
    a2 = r1.astype(BF16)
    a3 = (r1 - a2.astype(F32)).astype(BF16)
    return a1, a2, a3


def _cumsum_rows(x):
    n = x.shape[0]
    r = lax.broadcasted_iota(jnp.int32, (n, n), 0)
    c = lax.broadcasted_iota(jnp.int32, (n, n), 1)
    tri = jnp.where(r >= c, 1.0, 0.0).astype(BF16)
    parts = _dot(tri, jnp.concatenate(_split3(x), axis=1))
    return parts[:, :LANES] + parts[:, LANES:2 * LANES] + parts[:, 2 * LANES:]


def _key_bias_lanes(cneg):
    lane = lax.broadcasted_iota(jnp.int32, cneg.shape, 1)
    hi = cneg.astype(BF16).astype(F32)
    r1 = cneg - hi
    mid = r1.astype(BF16).astype(F32)
    lo = r1 - mid
    out = jnp.where(lane < F_MID, hi, jnp.where(lane < F_LO, mid, lo))
    return jnp.where((lane >= F_HI) & (lane < F_LO + HEADS), out, 0.0)


def _head_block(x, h):
    blk = x[:, (h // 2) * LANES:(h // 2 + 1) * LANES]
    return pltpu.roll(blk, HEAD_DIM, 1) if h % 2 else blk


def _key_operand(k, bias_lanes, h):
    lane = lax.broadcasted_iota(jnp.int32, bias_lanes.shape, 1)
    return jnp.where(lane < HEAD_DIM, _head_block(k, h), bias_lanes).astype(BF16)


def _prepare_weight(wt_hbm, w_ref, stage, sem):
    for r0, c0 in W_SECTIONS:
        cp = pltpu.make_async_copy(wt_hbm.at[pl.ds(r0, ATTN_W), :], stage, sem)
        cp.start()
        cp.wait()
        w_ref[:, c0:c0 + ATTN_W] = stage[...].T.astype(BF16)
    cp = pltpu.make_async_copy(wt_hbm.at[pl.ds(W_F_ROW, HEADS), :], stage.at[pl.ds(0, HEADS), :], sem)
    cp.start()
    cp.wait()
    f = stage[0:HEADS, :]
    z = lambda n: jnp.zeros((n, D_MODEL), F32)
    fblk = jnp.concatenate([f, z(F_HI - HEADS), f, f, f, z(LANES - F_LO - HEADS)], axis=0)
    w_ref[:, COL_F:COL_F + LANES] = fblk.T.astype(BF16)


def _proj_kernel(prepare, x_ref, st_ref, gm_ref, w_in_ref, bf_ref, cw_ref, gc_ref,
                 k_ref, v_ref, lf_ref, qa_ref, ka_ref, mb_ref, ncv_ref, *rest):
    i = pl.program_id(1)
    nb, tm, _ = x_ref.shape
    if prepare:
        w_ref, ubuf, ccar, stage, sem = rest

        @pl.when((pl.program_id(0) == 0) & (i == 0))
        def _():
            _prepare_weight(w_in_ref, w_ref, stage, sem)
    else:
        w_ref = w_in_ref
        ubuf, ccar = rest

    @pl.when(i == 0)
    def _():
        ccar[...] = jnp.zeros_like(ccar)
        ubuf[:, 0:8, :] = st_ref[...]

    @pl.when(i > 0)
    def _():
        ubuf[:, 0:8, :] = ubuf[:, tm:tm + 8, :]

    lane = lax.broadcasted_iota(jnp.int32, (tm, LANES), 1)
    pad = (-tm) % LANES
    for bb in range(nb):
        xn = _rms(x_ref[bb], gm_ref[...]).astype(BF16)
        proj = lambda c0, n: _dot(xn, w_ref[:, c0:c0 + n])

        lfb = _log_sigmoid(proj(COL_F, LANES) + bf_ref[...])
        lf_ref[bb] = (_pad_rows(lfb, tm + pad) if pad else lfb).T[F_OUT:F_OUT + HEADS, 0:tm]
        c = _cumsum_rows(lfb) + ccar[bb, 0:1, :]
        ccar[bb] = jnp.broadcast_to(c[tm - 1:tm, :], ccar.shape[1:])
        bias_lanes = _key_bias_lanes(c * (-LOG2E))
        k = proj(COL_K, ATTN_W)
        k_ref[bb] = k
        for h in range(HEADS):
            ka_ref[bb, h] = _key_operand(k, bias_lanes, h)
        q = proj(COL_Q, ATTN_W)
        for h in range(HEADS):
            sel = jnp.where((lane == F_HI + h) | (lane == F_MID + h) | (lane == F_LO + h), 1.0, 0.0)
            qa_ref[bb, h] = jnp.where(lane < HEAD_DIM, _head_block(q, h) * Q_SCALE, sel).astype(BF16)
        v_ref[bb] = proj(COL_V, ATTN_W)

        u = proj(COL_CG, CONV_W) * proj(COL_H, CONV_W)
        ubuf[bb, 8:8 + tm, :] = u
        y = (cw_ref[0:1, :] * ubuf[bb, 6:6 + tm, :] + cw_ref[1:2, :] * ubuf[bb, 7:7 + tm, :]
             + cw_ref[2:3, :] * u)
        ob = proj(COL_BG, CONV_W) * y
        mb_ref[bb] = _rms(ob, gc_ref[...]).astype(BF16)
        ncv_ref[bb] = u[tm - 2:tm, :]


def _proj(x, state, gm, w, bias_f, cw, gc, tm, nb):
    b, s, _ = x.shape
    prepare = w.dtype == F32
    full = lambda shape: pl.BlockSpec(shape, lambda bi, i: (0,) * len(shape))
    out_specs = [
        pl.BlockSpec((nb, tm, ATTN_W), lambda bi, i: (bi, i, 0)),
        pl.BlockSpec((nb, tm, ATTN_W), lambda bi, i: (bi, i, 0)),
        pl.BlockSpec((nb, HEADS, tm), lambda bi, i: (bi, 0, i)),
        pl.BlockSpec((nb, HEADS, tm, LANES), lambda bi, i: (bi, 0, i, 0)),
        pl.BlockSpec((nb, HEADS, tm, LANES), lambda bi, i: (bi, 0, i, 0)),
        pl.BlockSpec((nb, tm, CONV_W), lambda bi, i: (bi, i, 0)),
        pl.BlockSpec((nb, 2, CONV_W), lambda bi, i: (bi, 0, 0)),
    ]
    out_shape = [
        jax.ShapeDtypeStruct((b, s, ATTN_W), F32),
        jax.ShapeDtypeStruct((b, s, ATTN_W), F32),
        jax.ShapeDtypeStruct((b, HEADS, s), F32),
        jax.ShapeDtypeStruct((b, HEADS, s, LANES), BF16),
        jax.ShapeDtypeStruct((b, HEADS, s, LANES), BF16),
        jax.ShapeDtypeStruct((b, s, CONV_W), BF16),
        jax.ShapeDtypeStruct((b, 2, CONV_W), F32),
    ]
    scratch = [pltpu.VMEM((nb, tm + 8, CONV_W), F32), pltpu.VMEM((nb, 8, LANES), F32)]
    if prepare:
        out_specs.append(full((D_MODEL, PROJ_PAD)))
        out_shape.append(jax.ShapeDtypeStruct((D_MODEL, PROJ_PAD), BF16))
        scratch += [pltpu.VMEM((ATTN_W, D_MODEL), F32), pltpu.SemaphoreType.DMA(())]
    return pl.pallas_call(
        functools.partial(_proj_kernel, prepare),
        grid=(b // nb, s // tm),
        in_specs=[
            pl.BlockSpec((nb, tm, D_MODEL), lambda bi, i: (bi, i, 0)),
            pl.BlockSpec((nb, 8, CONV_W), lambda bi, i: (bi, 0, 0)),
            full((1, D_MODEL)),
            pl.BlockSpec(memory_space=pl.ANY) if prepare else full((D_MODEL, PROJ_PAD)),
            full((1, LANES)),
            full((8, CONV_W)),
            full((1, CONV_W)),
        ],
        out_specs=out_specs,
        out_shape=out_shape,
        scratch_shapes=scratch,
        compiler_params=pltpu.CompilerParams(
            dimension_semantics=("arbitrary", "arbitrary"), vmem_limit_bytes=VMEM_LIMIT),
        name="proj",
    )(x, state, gm, w, bias_f, cw, gc)


ATT_BLK = 256


def _attn_kernel(qa_ref, ka_ref, v_ref, o_ref, sbuf):
    blk = ATT_BLK
    vt = v_ref[0].T.astype(BF16)
    r = lax.broadcasted_iota(jnp.int32, (blk, blk), 0)
    c = lax.broadcasted_iota(jnp.int32, (blk, blk), 1)
    causal = r <= c
    units = [(iq, hh) for iq in range(ka_ref.shape[2] // blk) for hh in range(2)]

    def scores(u):
        iq, hh = units[u]
        n = (iq + 1) * blk
        sbuf[u % nbuf, 0:n, :] = _dot_nt(ka_ref[0, hh, 0:n, :], qa_ref[0, hh, iq * blk:n, :])

    nbuf = sbuf.shape[0]
    for u in range(nbuf - 1):
        scores(u)
    outs = []
    for u, (iq, hh) in enumerate(units):
        if u + nbuf - 1 < len(units):
            scores(u + nbuf - 1)
        noff = iq * blk
        rows = slice(hh * HEAD_DIM, (hh + 1) * HEAD_DIM)
        s_d = jnp.where(causal, sbuf[u % nbuf, noff:noff + blk, :], -jnp.inf)
        m = jnp.max(s_d, axis=0, keepdims=True)
        if iq > 0:
            m = jnp.maximum(m, jnp.max(sbuf[u % nbuf, 0:noff, :], axis=0, keepdims=True))
        p_d = jnp.exp2(s_d - m)
        l = jnp.sum(p_d, axis=0, keepdims=True)
        acc = _dot(vt[rows, noff:noff + blk], p_d.astype(BF16))
        if iq > 0:
            p_o = jnp.exp2(sbuf[u % nbuf, 0:noff, :] - m)
            l = l + jnp.sum(p_o, axis=0, keepdims=True)
            acc = acc + _dot(vt[rows, 0:noff], p_o.astype(BF16))
        outs.append(acc / l)
        if hh == 1:
            o_ref[0, noff:noff + blk, :] = jnp.concatenate(outs, axis=0).T
            outs = []


def _attention(qa, ka, v):
    b, _, s, _ = qa.shape
    return pl.pallas_call(
        _attn_kernel,
        grid=(b, HEADS // 2),
        in_specs=[
            pl.BlockSpec((1, 2, s, LANES), lambda bi, p: (bi, p, 0, 0)),
            pl.BlockSpec((1, 2, s, LANES), lambda bi, p: (bi, p, 0, 0)),
            pl.BlockSpec((1, s, LANES), lambda bi, p: (bi, 0, p)),
        ],
        out_specs=pl.BlockSpec((1, s, LANES), lambda bi, p: (bi, 0, p)),
        out_shape=jax.ShapeDtypeStruct((b, s, ATTN_W), F32),
        scratch_shapes=[pltpu.VMEM((3, s, ATT_BLK), F32)],
        compiler_params=pltpu.CompilerParams(
            dimension_semantics=("arbitrary", "arbitrary"), vmem_limit_bytes=VMEM_LIMIT),
        name="attention",
    )(qa, ka, v)


Q_PAD = 128


def _cumsum_lanes(x):
    n = x.shape[1]
    r = lax.broadcasted_iota(jnp.int32, (n, n), 0)
    c = lax.broadcasted_iota(jnp.int32, (n, n), 1)
    tri = jnp.where(r <= c, 1.0, 0.0).astype(BF16)
    terms = jnp.concatenate([a.astype(F32) for a in _split3(x)], axis=0).astype(BF16)
    parts = _dot(terms, tri)
    rows = x.shape[0]
    return parts[0:rows] + parts[rows:2 * rows] + parts[2 * rows:]


def _attn_cache_kernel(qa_ref, ka_ref, v_ref, ckt_ref, cvt_ref, clft_ref, o_ref):
    t = qa_ref.shape[2]
    plen = ckt_ref.shape[2]
    cinc = _cumsum_lanes(clft_ref[0])
    bias = (cinc - cinc[:, plen - 1:plen]) * (-LOG2E)
    lane = lax.broadcasted_iota(jnp.int32, (t, LANES), 1)
    rq = lax.broadcasted_iota(jnp.int32, (t, Q_PAD), 0)
    ck = lax.broadcasted_iota(jnp.int32, (t, Q_PAD), 1)
    new_ok = ck <= rq

    pairs = []
    for p in range(HEADS // 2):
        rows = slice(p * LANES, (p + 1) * LANES)
        kt = ckt_ref[0, rows, :].astype(BF16)
        vt = cvt_ref[0, rows, :].astype(BF16)
        vn = _pad_rows(v_ref[0][:, rows], Q_PAD).astype(BF16)
        res = []
        for hh in range(2):
            h = 2 * p + hh
            qa = qa_ref[0, h]
            qh = jnp.where(lane < HEAD_DIM, qa.astype(F32), 0.0)
            qh = (pltpu.roll(qh, HEAD_DIM, 1) if hh else qh).astype(BF16)
            s_past = _dot(qh, kt) + bias[h:h + 1, :]
            s_new = jnp.where(new_ok, _dot_nt(qa, _pad_rows(ka_ref[0, h], Q_PAD)), -jnp.inf)
            m = jnp.maximum(jnp.max(s_past, axis=1, keepdims=True), jnp.max(s_new, axis=1, keepdims=True))
            p_past = jnp.exp2(s_past - m)
            p_new = jnp.exp2(s_new - m)
            l = jnp.sum(p_past, axis=1, keepdims=True) + jnp.sum(p_new, axis=1, keepdims=True)
            acc = _dot_nt(p_past.astype(BF16), vt) + _dot(p_new.astype(BF16), vn)
            res.append(acc / l)
        pairs.append(jnp.where(lane < HEAD_DIM, res[0], res[1]))
    o_ref[0] = jnp.concatenate(pairs, axis=1)


def _attention_cache(qa, ka, v, ckt, cvt, clft):
    b, _, t, _ = qa.shape
    plen = ckt.shape[2]
    return pl.pallas_call(
        _attn_cache_kernel,
        grid=(b,),
        in_specs=[
            pl.BlockSpec((1, HEADS, t, LANES), lambda bi: (bi, 0, 0, 0)),
            pl.BlockSpec((1, HEADS, t, LANES), lambda bi: (bi, 0, 0, 0)),
            pl.BlockSpec((1, t, ATTN_W), lambda bi: (bi, 0, 0)),
            pl.BlockSpec((1, ATTN_W, plen), lambda bi: (bi, 0, 0)),
            pl.BlockSpec((1, ATTN_W, plen), lambda bi: (bi, 0, 0)),
            pl.BlockSpec((1, HEADS, plen), lambda bi: (bi, 0, 0)),
        ],
        out_specs=pl.BlockSpec((1, t, ATTN_W), lambda bi: (bi, 0, 0)),
        out_shape=jax.ShapeDtypeStruct((b, t, ATTN_W), F32),
        compiler_params=pltpu.CompilerParams(
            dimension_semantics=("arbitrary",), vmem_limit_bytes=VMEM_LIMIT),
        name="attention_cache",
    )(qa, ka, v, ckt, cvt, clft)


def _mix_kernel(x_ref, o_ref, mb_ref, wo_ref, ga_ref, gf_ref, wr_ref, br_ref,
                rows_ref, slot_ref, cnt_ref, carry):
    i = pl.program_id(0)
    tm = slot_ref.shape[2]
    nsub = x_ref.shape[0] // tm
    row = lax.broadcasted_iota(jnp.int32, (EPG, tm), 0)
    row2 = lax.broadcasted_iota(jnp.int32, (2 * EPG, tm), 0)
    tok = lax.broadcasted_iota(jnp.int32, (2 * EPG, tm), 1)
    r2 = lax.broadcasted_iota(jnp.int32, (tm, tm), 0)
    c2 = lax.broadcasted_iota(jnp.int32, (tm, tm), 1)
    earlier = jnp.where(r2 < c2, 1.0, 0.0).astype(BF16)

    @pl.when(i == 0)
    def _():
        carry[...] = jnp.zeros_like(carry)

    total = carry[:, 0:1]
    for j in range(nsub):
        rs = slice(j * tm, (j + 1) * tm)
        oa = _rms(o_ref[rs, :], ga_ref[...]).astype(BF16)
        merged = jnp.concatenate([oa, mb_ref[rs, :]], axis=1)
        x2 = x_ref[rs, :] + _dot(merged, wo_ref[...])
        rows_ref[rs, D_MODEL:2 * D_MODEL] = x2
        xb = _rms(x2, gf_ref[...]).astype(BF16)
        rows_ref[rs, 0:D_MODEL] = xb.astype(F32)

        lt = (_dot(xb, wr_ref[...]) + br_ref[...]).T
        lg = jnp.where(row < N_GROUPS, lt[0:EPG], -jnp.inf)
        eg = jnp.exp(lg - jnp.max(lg, axis=0, keepdims=True))
        pg = eg / jnp.sum(eg, axis=0, keepdims=True)
        pg_top = jnp.max(pg, axis=0, keepdims=True)
        g_top = jnp.min(jnp.where(pg == pg_top, row, EPG), axis=0, keepdims=True)

        le = lt[EPG:2 * EPG]
        for g in range(1, N_GROUPS):
            le = jnp.where(g_top == g, lt[EPG * (g + 1):EPG * (g + 2)], le)
        ee = jnp.exp(le - jnp.max(le, axis=0, keepdims=True))
        pe = ee / jnp.sum(ee, axis=0, keepdims=True)
        v1 = jnp.max(pe, axis=0, keepdims=True)
        i1 = jnp.min(jnp.where(pe == v1, row, EPG), axis=0, keepdims=True)
        pe2 = jnp.where(row == i1, -1.0, pe)
        v2 = jnp.max(pe2, axis=0, keepdims=True)
        i2 = jnp.min(jnp.where(pe2 == v2, row, EPG), axis=0, keepdims=True)
        den = v1 + v2
        wg = jnp.where(row == i1, v1 / den * pg_top, 0.0) + jnp.where(row == i2, v2 / den * pg_top, 0.0)
        meta = jnp.where(row2 == META_ID, ((i * nsub + j) * tm + tok).astype(F32),
                         jnp.concatenate([wg, jnp.zeros((EPG, tm), F32)], axis=0))
        rows_ref[rs, 2 * D_MODEL:] = jnp.concatenate([meta, jnp.zeros((LANES - 2 * EPG, tm), F32)], axis=0).T

        onehot = jnp.where(row == g_top, 1.0, 0.0)
        rank = jnp.sum(onehot * (_dot(onehot.astype(BF16), earlier) + total), axis=0, keepdims=True)
        slot_ref[i * nsub + j] = g_top * (1 << GROUP_SHIFT) + rank.astype(jnp.int32)
        total = total + jnp.sum(onehot, axis=1, keepdims=True)

    carry[...] = jnp.broadcast_to(total, carry.shape)
    cnt_ref[...] = carry[...].astype(jnp.int32)

    @pl.when(i == pl.num_programs(0) - 1)
    def _():
        padded = jnp.floor((total + (EXPERT_TILE - 1)) * (1.0 / EXPERT_TILE)) * EXPERT_TILE
        o1 = padded[0:1]
        o2 = o1 + padded[1:2]
        o3 = o2 + padded[2:3]
        code = slot_ref[...]
        g = code >> GROUP_SHIFT
        first = jnp.where(g == 0, 0.0, jnp.where(g == 1, o1, jnp.where(g == 2, o2, o3)))
        slot_ref[...] = first.astype(jnp.int32) + (code & ((1 << GROUP_SHIFT) - 1))


def _mix(x, o, mb, wo, ga, gf, wr, br, tm, nsub):
    n = x.shape[0]
    assert n < (1 << GROUP_SHIFT)
    tb = tm * nsub
    full = lambda shape: pl.BlockSpec(shape, lambda i: (0,) * len(shape))
    return pl.pallas_call(
        _mix_kernel,
        grid=(n // tb,),
        in_specs=[
            pl.BlockSpec((tb, D_MODEL), lambda i: (i, 0)),
            pl.BlockSpec((tb, ATTN_W), lambda i: (i, 0)),
            pl.BlockSpec((tb, CONV_W), lambda i: (i, 0)),
            full((D_MODEL, D_MODEL)),
            full((1, ATTN_W)),
            full((1, D_MODEL)),
            full((D_MODEL, LANES)),
            full((1, LANES)),
        ],
        out_specs=[
            pl.BlockSpec((tb, ROW_W), lambda i: (i, 0)),
            full((n // tm, 1, tm)),
            full((EPG, LANES)),
        ],
        out_shape=[
            jax.ShapeDtypeStruct((n, ROW_W), F32),
            jax.ShapeDtypeStruct((n // tm, 1, tm), jnp.int32),
            jax.ShapeDtypeStruct((EPG, LANES), jnp.int32),
        ],
        scratch_shapes=[pltpu.VMEM((EPG, LANES), F32)],
        compiler_params=pltpu.CompilerParams(
            dimension_semantics=("arbitrary",), vmem_limit_bytes=VMEM_LIMIT),
        name="mix",
    )(x, o, mb, wo, ga, gf, wr, br)


def _sorted_tiles(n):
    return n // EXPERT_TILE + N_GROUPS


def _group_tiles(cnt_ref):
    nt = [(cnt_ref[g] + (EXPERT_TILE - 1)) >> TILE_SHIFT for g in range(N_GROUPS)]
    s1 = nt[0]
    s2 = s1 + nt[1]
    s3 = s2 + nt[2]
    return s1, s2, s3, s3 + nt[3]


def _scatter_kernel(cnt_ref, rows_ref, slot_ref, xs_ref, zeros, sem, zsem):
    tm = rows_ref.shape[0]

    @pl.when(pl.program_id(0) == 0)
    def _():
        zeros[...] = jnp.zeros_like(zeros)
        s1, s2, s3, tot = _group_tiles(cnt_ref)
        ends = (s1, s2, s3, tot)

        def fill(tile):
            return pltpu.make_async_copy(zeros, xs_ref.at[pl.ds(tile * EXPERT_TILE, EXPERT_TILE), :], zsem)

        for g in range(N_GROUPS):
            partial = (cnt_ref[g] & (EXPERT_TILE - 1)) != 0

            @pl.when(partial)
            def _():
                fill(ends[g] - 1).start()
                fill(ends[g] - 1).wait()

        def fill_tail(tile, c):
            fill(tile).start()
            fill(tile).wait()
            return c

        lax.fori_loop(tot, xs_ref.shape[0] // EXPERT_TILE, fill_tail, 0)

    copy_of_row = lambda r: pltpu.make_async_copy(
        rows_ref.at[pl.ds(r, 1), :], xs_ref.at[pl.ds(slot_ref[0, 0, r], 1), :], sem)
    for r in range(tm):
        copy_of_row(r).start(priority=r % 2)
    for r in range(tm):
        copy_of_row(r).wait()


def _scatter(cnt, rows, slots, tm):
    n = rows.shape[0]
    return pl.pallas_call(
        _scatter_kernel,
        grid_spec=pltpu.PrefetchScalarGridSpec(
            num_scalar_prefetch=1,
            grid=(n // tm,),
            in_specs=[
                pl.BlockSpec((tm, ROW_W), lambda i, c: (i, 0)),
                pl.BlockSpec((1, 1, tm), lambda i, c: (i, 0, 0), memory_space=pltpu.SMEM),
            ],
            out_specs=pl.BlockSpec(memory_space=pl.ANY),
            scratch_shapes=[pltpu.VMEM((EXPERT_TILE, ROW_W), F32),
                            pltpu.SemaphoreType.DMA(()), pltpu.SemaphoreType.DMA(())],
        ),
        out_shape=jax.ShapeDtypeStruct((_sorted_tiles(n) * EXPERT_TILE, ROW_W), F32),
        compiler_params=pltpu.CompilerParams(
            dimension_semantics=("arbitrary",), vmem_limit_bytes=VMEM_LIMIT),
        name="scatter",
    )(cnt, rows, slots)


def _tile_of(t, cnt_ref):
    s1, s2, s3, tot = _group_tiles(cnt_ref)
    tc = jnp.minimum(t, tot - 1)
    g = (tc >= s1).astype(jnp.int32) + (tc >= s2).astype(jnp.int32) + (tc >= s3).astype(jnp.int32)
    first = jnp.where(g == 0, 0, jnp.where(g == 1, s1, jnp.where(g == 2, s2, s3)))
    cnt = jnp.where(g == 0, cnt_ref[0], jnp.where(g == 1, cnt_ref[1], jnp.where(g == 2, cnt_ref[2], cnt_ref[3])))
    return tc, g, jnp.where(t < tot, cnt - (tc - first) * EXPERT_TILE, 0)


def _experts_kernel(cnt_ref, xs_ref, wgu_ref, wdn_ref, gn_ref, y_hbm, yb0, yb1, idv, ids, sem, isem):
    t = pl.program_id(0)
    _, _, cur_valid = _tile_of(t, cnt_ref)
    prev_valid = jnp.where(t > 0, jnp.clip(_tile_of(t - 1, cnt_ref)[2], 0, EXPERT_TILE), 0)
    fast = (prev_valid == EXPERT_TILE) & (cur_valid > 0)

    def compute(ybuf, par):
        idv[par:par + 1, :] = xs_ref[:, 2 * D_MODEL:].T[META_ID:META_ID + 1, :].astype(jnp.int32)
        pltpu.make_async_copy(idv.at[par], ids.at[par], isem).start()
        xb = xs_ref[:, 0:D_MODEL].astype(BF16)
        wts = xs_ref[:, 2 * D_MODEL:]
        tot = xs_ref[:, D_MODEL:2 * D_MODEL]
        for e in range(EPG):
            gu = _dot(xb, wgu_ref[0, e])
            gate = gu[:, :D_EXPERT]
            hmid = gate * jax.nn.sigmoid(gate) * gu[:, D_EXPERT:] * wts[:, e:e + 1]
            tot = tot + _dot(hmid.astype(BF16), wdn_ref[0, e])
        ybuf[...] = _rms(tot, gn_ref[...])

    def ids_wait(par):
        pltpu.make_async_copy(idv.at[par], ids.at[par], isem).wait()

    def row_copy(ybuf, par, r):
        return pltpu.make_async_copy(ybuf.at[pl.ds(r, 1), :], y_hbm.at[pl.ds(ids[par, r], 1), :], sem)

    for par, (ycur, yprev) in enumerate(((yb0, yb1), (yb1, yb0))):
        mine = (t % 2) == par

        @pl.when(mine & fast)
        def _():
            ids_wait(1 - par)
            for r in range(EXPERT_TILE):
                row_copy(yprev, 1 - par, r).start(priority=r % 2)
            compute(ycur, par)
            for r in range(EXPERT_TILE):
                row_copy(yprev, 1 - par, r).wait()

        @pl.when(mine & jnp.logical_not(fast))
        def _():
            @pl.when(prev_valid > 0)
            def _():
                ids_wait(1 - par)

                def start(r, c):
                    row_copy(yprev, 1 - par, r).start()
                    return c
                lax.fori_loop(0, prev_valid, start, 0)

            @pl.when(cur_valid > 0)
            def _():
                compute(ycur, par)

            def wait(r, c):
                row_copy(yprev, 1 - par, r).wait()
                return c
            lax.fori_loop(0, prev_valid, wait, 0)


def _experts(cnt, xs, wgu, wdn, gn, n):
    tiles = xs.shape[0] // EXPERT_TILE
    grp = lambda t, c: (_tile_of(t, c)[1], 0, 0, 0)
    return pl.pallas_call(
        _experts_kernel,
        grid_spec=pltpu.PrefetchScalarGridSpec(
            num_scalar_prefetch=1,
            grid=(tiles + 1,),
            in_specs=[
                pl.BlockSpec((EXPERT_TILE, ROW_W), lambda t, c: (_tile_of(t, c)[0], 0)),
                pl.BlockSpec((1, EPG, D_MODEL, 2 * D_EXPERT), grp),
                pl.BlockSpec((1, EPG, D_EXPERT, D_MODEL), grp),
                pl.BlockSpec((1, D_MODEL), lambda t, c: (0, 0)),
            ],
            out_specs=pl.BlockSpec(memory_space=pl.ANY),
            scratch_shapes=[pltpu.VMEM((EXPERT_TILE, D_MODEL), F32), pltpu.VMEM((EXPERT_TILE, D_MODEL), F32),
                            pltpu.VMEM((8, EXPERT_TILE), jnp.int32), pltpu.SMEM((2, EXPERT_TILE), jnp.int32),
                            pltpu.SemaphoreType.DMA(()), pltpu.SemaphoreType.DMA(())],
        ),
        out_shape=jax.ShapeDtypeStruct((n, D_MODEL), F32),
        compiler_params=pltpu.CompilerParams(
            dimension_semantics=("arbitrary",), vmem_limit_bytes=VMEM_LIMIT),
        name="experts",
    )(cnt, xs, wgu, wdn, gn)


def _forget_lanes(f):
    z = lambda n: jnp.zeros(f.shape[:-1] + (n,), f.dtype)
    return jnp.concatenate([f, z(F_HI - HEADS), f, f, f, z(LANES - F_LO - HEADS)], axis=-1)


def _trunk(x, attend, state, w, tm_proj, nb_proj, tm_tok, nsub_tok):
    b, s, _ = x.shape
    k, v, lf, qa, ka, mb, ncv, *built = _proj(
        x, state, w["gm"], w["w_proj"], w["bias_f"], w["cw"], w["gc"], tm_proj, nb_proj)
    if built:
        w["w_proj"] = built[0]
    o = attend(qa, ka, v)
    n = b * s
    rows, slots, cnt = _mix(x.reshape(n, D_MODEL), o.reshape(n, ATTN_W), mb.reshape(n, CONV_W),
                            w["wo"], w["ga"], w["gf"], w["wr"], w["br"], tm_tok, nsub_tok)
    cnt = cnt[:, 0]
    y = _experts(cnt, _scatter(cnt, rows, slots, tm_tok), w["wgu"], w["wdn"], w["gn"], n)
    return (y.reshape(b, s, D_MODEL), k.reshape(1, b, s, HEADS, HEAD_DIM), v.reshape(1, b, s, HEADS, HEAD_DIM),
            lf.transpose(0, 2, 1).reshape(1, b, s, HEADS), ncv.reshape(1, b, 2, CONV_W))


def kernel(x_prompt, x_sample, cache_k, cache_v, cache_logf, state_conv, norm_mix_g, w_in, b_forget, conv_w, norm_attn_g, norm_conv_g, w_out, norm_ffn_g, w_router_group, b_router_group, w_router_expert, b_router_expert, w_expert_gate_up, w_expert_down, norm_final_g):
    assert w_in.shape[0] == 1, "single-layer trunk"
    wr = jnp.concatenate(
        [w_router_group[0], jnp.zeros((D_MODEL, EPG - N_GROUPS), F32),
         w_router_expert[0].transpose(1, 0, 2).reshape(D_MODEL, N_GROUPS * EPG),
         jnp.zeros((D_MODEL, LANES - EPG - N_GROUPS * EPG), F32)], axis=1).astype(BF16)
    br = jnp.concatenate(
        [b_router_group[0], jnp.zeros((EPG - N_GROUPS,), F32), b_router_expert[0].reshape(-1),
         jnp.zeros((LANES - EPG - N_GROUPS * EPG,), F32)]).reshape(1, LANES)
    w = dict(
        gm=norm_mix_g[0].reshape(1, D_MODEL), w_proj=w_in[0].T, bias_f=_forget_lanes(b_forget[0]).reshape(1, LANES),
        cw=jnp.concatenate([conv_w[0], jnp.zeros((8 - conv_w.shape[1], CONV_W), F32)], axis=0),
        gc=norm_conv_g[0].reshape(1, CONV_W), wo=w_out[0].astype(BF16), ga=norm_attn_g[0].reshape(1, ATTN_W),
        gf=norm_ffn_g[0].reshape(1, D_MODEL), wr=wr, br=br,
        wgu=w_expert_gate_up[0], wdn=w_expert_down[0],
        gn=norm_final_g.reshape(1, D_MODEL))

    bp, sp, _ = x_prompt.shape
    bs, ss, _ = x_sample.shape
    plen = cache_k.shape[2]
    yp, kp, vp, lfp, cvp = _trunk(x_prompt, _attention, jnp.zeros((bp, 8, CONV_W), F32), w, 512, 2, 512, 2)

    ckt = cache_k[0].transpose(0, 2, 3, 1).reshape(bs, ATTN_W, plen)
    cvt = cache_v[0].transpose(0, 2, 3, 1).reshape(bs, ATTN_W, plen)
    clft = cache_logf[0].transpose(0, 2, 1)
    st = jnp.concatenate([jnp.zeros((bs, 6, CONV_W), F32), state_conv[0]], axis=1)
    attend_s = lambda qa, ka, v: _attention_cache(qa, ka, v, ckt, cvt, clft)
    ys, ks, vs, lfs, cvs = _trunk(x_sample, attend_s, st, w, ss, 2, bs * ss, 1)
    return (yp, ys, kp, vp, lfp, cvp, ks, vs, lfs, cvs)
```

```python
import functools
import math

import jax
import jax.numpy as jnp
from jax import lax
from jax.experimental import pallas as pl
from jax.experimental.pallas import tpu as pltpu

F32 = jnp.float32
BF16 = jnp.bfloat16

D_MODEL = 1024
HEADS = 8
HEAD_DIM = 64
ATTN_W = HEADS * HEAD_DIM
CONV_W = 512
N_GROUPS = 4
EPG = 8
D_EXPERT = 256
EPS = 1e-6
LOG2E = math.log2(math.e)
Q_SCALE = HEAD_DIM ** -0.5 * LOG2E

LANES = 128
COL_Q, COL_K, COL_V, COL_F, COL_BG, COL_CG, COL_H = 0, 512, 1024, 1536, 1664, 2176, 2688
PROJ_PAD = 3200
W_SECTIONS = ((0, COL_Q), (512, COL_K), (1024, COL_V), (1544, COL_BG), (2056, COL_CG), (2568, COL_H))
W_F_ROW = 1536
F_OUT, F_HI, F_MID, F_LO = 0, 64, 72, 80

GROUP_SHIFT = 20
ROW_W = 2 * D_MODEL + LANES
META_ID = EPG
TILE_SHIFT = 8
EXPERT_TILE = 1 << TILE_SHIFT

VMEM_LIMIT = 60 * 1024 * 1024


def _dot(a, b):
    return jnp.dot(a, b, preferred_element_type=F32)


def _dot_nt(a, b):
    return lax.dot_general(a, b, (((1,), (1,)), ((), ())), preferred_element_type=F32)


def _rms(x, g):
    return x * lax.rsqrt(jnp.mean(x * x, axis=-1, keepdims=True) + EPS) * g


def _log_sigmoid(x):
    return jnp.minimum(x, 0.0) - jnp.log(1.0 + jnp.exp(-jnp.abs(x)))


def _pad_rows(x, n):
    return jnp.concatenate([x, jnp.zeros((n - x.shape[0], x.shape[1]), x.dtype)], axis=0)


def _split3(x):
    a1 = x.astype(BF16)
    r1 = x - a1.astype(F32)
    a2 = r1.astype(BF16)
    a3 = (r1 - a2.astype(F32)).astype(BF16)
    return a1, a2, a3


def _cumsum_rows(x):
    n = x.shape[0]
    r = lax.broadcasted_iota(jnp.int32, (n, n), 0)
    c = lax.broadcasted_iota(jnp.int32, (n, n), 1)
    tri = jnp.where(r >= c, 1.0, 0.0).astype(BF16)
    parts = _dot(tri, jnp.concatenate(_split3(x), axis=1))
    return parts[:, :LANES] + parts[:, LANES:2 * LANES] + parts[:, 2 * LANES:]


def _key_bias_lanes(cneg):
    lane = lax.broadcasted_iota(jnp.int32, cneg.shape, 1)
    hi = cneg.astype(BF16).astype(F32)
    r1 = cneg - hi
    mid = r1.astype(BF16).astype(F32)
    lo = r1 - mid
    out = jnp.where(lane < F_MID, hi, jnp.where(lane < F_LO, mid, lo))
    return jnp.where((lane >= F_HI) & (lane < F_LO + HEADS), out, 0.0)


def _head_block(x, h):
    blk = x[:, (h // 2) * LANES:(h // 2 + 1) * LANES]
    return pltpu.roll(blk, HEAD_DIM, 1) if h % 2 else blk


def _key_operand(k, bias_lanes, h):
    lane = lax.broadcasted_iota(jnp.int32, bias_lanes.shape, 1)
    return jnp.where(lane < HEAD_DIM, _head_block(k, h), bias_lanes).astype(BF16)


def _prepare_weight(wt_hbm, w_ref, stage, sem):
    for r0, c0 in W_SECTIONS:
        cp = pltpu.make_async_copy(wt_hbm.at[pl.ds(r0, ATTN_W), :], stage, sem)
        cp.start()
        cp.wait()
        w_ref[:, c0:c0 + ATTN_W] = stage[...].T.astype(BF16)
    cp = pltpu.make_async_copy(wt_hbm.at[pl.ds(W_F_ROW, HEADS), :], stage.at[pl.ds(0, HEADS), :], sem)
    cp.start()
    cp.wait()
    f = stage[0:HEADS, :]
    z = lambda n: jnp.zeros((n, D_MODEL), F32)
    fblk = jnp.concatenate([f, z(F_HI - HEADS), f, f, f, z(LANES - F_LO - HEADS)], axis=0)
    w_ref[:, COL_F:COL_F + LANES] = fblk.T.astype(BF16)


def _proj_kernel(prepare, x_ref, st_ref, gm_ref, w_in_ref, bf_ref, cw_ref, gc_ref,
                 k_ref, v_ref, lf_ref, qa_ref, ka_ref, mb_ref, ncv_ref, *rest):
    i = pl.program_id(1)
    nb, tm, _ = x_ref.shape
    if prepare:
        w_ref, ubuf, ccar, stage, sem = rest

        @pl.when((pl.program_id(0) == 0) & (i == 0))
        def _():
            _prepare_weight(w_in_ref, w_ref, stage, sem)
    else:
        w_ref = w_in_ref
        ubuf, ccar = rest

    @pl.when(i == 0)
    def _():
        ccar[...] = jnp.zeros_like(ccar)
        ubuf[:, 0:8, :] = st_ref[...]

    @pl.when(i > 0)
    def _():
        ubuf[:, 0:8, :] = ubuf[:, tm:tm + 8, :]

    lane = lax.broadcasted_iota(jnp.int32, (tm, LANES), 1)
    pad = (-tm) % LANES
    for bb in range(nb):
        xn = _rms(x_ref[bb], gm_ref[...]).astype(BF16)
        proj = lambda c0, n: _dot(xn, w_ref[:, c0:c0 + n])

        lfb = _log_sigmoid(proj(COL_F, LANES) + bf_ref[...])
        lf_ref[bb] = (_pad_rows(lfb, tm + pad) if pad else lfb).T[F_OUT:F_OUT + HEADS, 0:tm]
        c = _cumsum_rows(lfb) + ccar[bb, 0:1, :]
        ccar[bb] = jnp.broadcast_to(c[tm - 1:tm, :], ccar.shape[1:])
        bias_lanes = _key_bias_lanes(c * (-LOG2E))
        k = proj(COL_K, ATTN_W)
        k_ref[bb] = k
        for h in range(HEADS):
            ka_ref[bb, h] = _key_operand(k, bias_lanes, h)
        q = proj(COL_Q, ATTN_W)
        for h in range(HEADS):
            sel = jnp.where((lane == F_HI + h) | (lane == F_MID + h) | (lane == F_LO + h), 1.0, 0.0)
            qa_ref[bb, h] = jnp.where(lane < HEAD_DIM, _head_block(q, h) * Q_SCALE, sel).astype(BF16)
        v_ref[bb] = proj(COL_V, ATTN_W)

        u = proj(COL_CG, CONV_W) * proj(COL_H, CONV_W)
        ubuf[bb, 8:8 + tm, :] = u
        y = (cw_ref[0:1, :] * ubuf[bb, 6:6 + tm, :] + cw_ref[1:2, :] * ubuf[bb, 7:7 + tm, :]
             + cw_ref[2:3, :] * u)
        ob = proj(COL_BG, CONV_W) * y
        mb_ref[bb] = _rms(ob, gc_ref[...]).astype(BF16)
        ncv_ref[bb] = u[tm - 2:tm, :]


def _proj(x, state, gm, w, bias_f, cw, gc, tm, nb):
    b, s, _ = x.shape
    prepare = w.dtype == F32
    full = lambda shape: pl.BlockSpec(shape, lambda bi, i: (0,) * len(shape))
    out_specs = [
        pl.BlockSpec((nb, tm, ATTN_W), lambda bi, i: (bi, i, 0)),
        pl.BlockSpec((nb, tm, ATTN_W), lambda bi, i: (bi, i, 0)),
        pl.BlockSpec((nb, HEADS, tm), lambda bi, i: (bi, 0, i)),
        pl.BlockSpec((nb, HEADS, tm, LANES), lambda bi, i: (bi, 0, i, 0)),
        pl.BlockSpec((nb, HEADS, tm, LANES), lambda bi, i: (bi, 0, i, 0)),
        pl.BlockSpec((nb, tm, CONV_W), lambda bi, i: (bi, i, 0)),
        pl.BlockSpec((nb, 2, CONV_W), lambda bi, i: (bi, 0, 0)),
    ]
    out_shape = [
        jax.ShapeDtypeStruct((b, s, ATTN_W), F32),
        jax.ShapeDtypeStruct((b, s, ATTN_W), F32),
        jax.ShapeDtypeStruct((b, HEADS, s), F32),
        jax.ShapeDtypeStruct((b, HEADS, s, LANES), BF16),
        jax.ShapeDtypeStruct((b, HEADS, s, LANES), BF16),
        jax.ShapeDtypeStruct((b, s, CONV_W), BF16),
        jax.ShapeDtypeStruct((b, 2, CONV_W), F32),
    ]
    scratch = [pltpu.VMEM((nb, tm + 8, CONV_W), F32), pltpu.VMEM((nb, 8, LANES), F32)]
    if prepare:
        out_specs.append(full((D_MODEL, PROJ_PAD)))
        out_shape.append(jax.ShapeDtypeStruct((D_MODEL, PROJ_PAD), BF16))
        scratch += [pltpu.VMEM((ATTN_W, D_MODEL), F32), pltpu.SemaphoreType.DMA(())]
    return pl.pallas_call(
        functools.partial(_proj_kernel, prepare),
        grid=(b // nb, s // tm),
        in_specs=[
            pl.BlockSpec((nb, tm, D_MODEL), lambda bi, i: (bi, i, 0)),
            pl.BlockSpec((nb, 8, CONV_W), lambda bi, i: (bi, 0, 0)),
            full((1, D_MODEL)),
            pl.BlockSpec(memory_space=pl.ANY) if prepare else full((D_MODEL, PROJ_PAD)),
            full((1, LANES)),
            full((8, CONV_W)),
            full((1, CONV_W)),
        ],
        out_specs=out_specs,
        out_shape=out_shape,
        scratch_shapes=scratch,
        compiler_params=pltpu.CompilerParams(
            dimension_semantics=("arbitrary", "arbitrary"), vmem_limit_bytes=VMEM_LIMIT),
        name="proj",
    )(x, state, gm, w, bias_f, cw, gc)


ATT_BLK = 256


def _attn_kernel(qa_ref, ka_ref, v_ref, o_ref, sbuf):
    blk = ATT_BLK
    vt = v_ref[0].T.astype(BF16)
    r = lax.broadcasted_iota(jnp.int32, (blk, blk), 0)
    c = lax.broadcasted_iota(jnp.int32, (blk, blk), 1)
    causal = r <= c
    units = [(iq, hh) for iq in range(ka_ref.shape[2] // blk) for hh in range(2)]

    def scores(u):
        iq, hh = units[u]
        n = (iq + 1) * blk
        sbuf[u % nbuf, 0:n, :] = _dot_nt(ka_ref[0, hh, 0:n, :], qa_ref[0, hh, iq * blk:n, :])

    nbuf = sbuf.shape[0]
    for u in range(nbuf - 1):
        scores(u)
    outs = []
    for u, (iq, hh) in enumerate(units):
        if u + nbuf - 1 < len(units):
            scores(u + nbuf - 1)
        noff = iq * blk
        rows = slice(hh * HEAD_DIM, (hh + 1) * HEAD_DIM)
        s_d = jnp.where(causal, sbuf[u % nbuf, noff:noff + blk, :], -jnp.inf)
        m = jnp.max(s_d, axis=0, keepdims=True)
        if iq > 0:
            m = jnp.maximum(m, jnp.max(sbuf[u % nbuf, 0:noff, :], axis=0, keepdims=True))
        p_d = jnp.exp2(s_d - m)
        l = jnp.sum(p_d, axis=0, keepdims=True)
        acc = _dot(vt[rows, noff:noff + blk], p_d.astype(BF16))
        if iq > 0:
            p_o = jnp.exp2(sbuf[u % nbuf, 0:noff, :] - m)
            l = l + jnp.sum(p_o, axis=0, keepdims=True)
            acc = acc + _dot(vt[rows, 0:noff], p_o.astype(BF16))
        outs.append(acc / l)
        if hh == 1:
            o_ref[0, noff:noff + blk, :] = jnp.concatenate(outs, axis=0).T
            outs = []


def _attention(qa, ka, v):
    b, _, s, _ = qa.shape
    return pl.pallas_call(
        _attn_kernel,
        grid=(b, HEADS // 2),
        in_specs=[
            pl.BlockSpec((1, 2, s, LANES), lambda bi, p: (bi, p, 0, 0)),
            pl.BlockSpec((1, 2, s, LANES), lambda bi, p: (bi, p, 0, 0)),
            pl.BlockSpec((1, s, LANES), lambda bi, p: (bi, 0, p)),
        ],
        out_specs=pl.BlockSpec((1, s, LANES), lambda bi, p: (bi, 0, p)),
        out_shape=jax.ShapeDtypeStruct((b, s, ATTN_W), F32),
        scratch_shapes=[pltpu.VMEM((3, s, ATT_BLK), F32)],
        compiler_params=pltpu.CompilerParams(
            dimension_semantics=("arbitrary", "arbitrary"), vmem_limit_bytes=VMEM_LIMIT),
        name="attention",
    )(qa, ka, v)


Q_PAD = 128


def _cumsum_lanes(x):
    n = x.shape[1]
    r = lax.broadcasted_iota(jnp.int32, (n, n), 0)
    c = lax.broadcasted_iota(jnp.int32, (n, n), 1)
    tri = jnp.where(r <= c, 1.0, 0.0).astype(BF16)
    terms = jnp.concatenate([a.astype(F32) for a in _split3(x)], axis=0).astype(BF16)
    parts = _dot(terms, tri)
    rows = x.shape[0]
    return parts[0:rows] + parts[rows:2 * rows] + parts[2 * rows:]


def _attn_cache_kernel(qa_ref, ka_ref, v_ref, ckt_ref, cvt_ref, clft_ref, o_ref):
    t = qa_ref.shape[2]
    plen = ckt_ref.shape[2]
    cinc = _cumsum_lanes(clft_ref[0])
    bias = (cinc - cinc[:, plen - 1:plen]) * (-LOG2E)
    lane = lax.broadcasted_iota(jnp.int32, (t, LANES), 1)
    rq = lax.broadcasted_iota(jnp.int32, (t, Q_PAD), 0)
    ck = lax.broadcasted_iota(jnp.int32, (t, Q_PAD), 1)
    new_ok = ck <= rq

    pairs = []
    for p in range(HEADS // 2):
        rows = slice(p * LANES, (p + 1) * LANES)
        kt = ckt_ref[0, rows, :].astype(BF16)
        vt = cvt_ref[0, rows, :].astype(BF16)
        vn = _pad_rows(v_ref[0][:, rows], Q_PAD).astype(BF16)
        res = []
        for hh in range(2):
            h = 2 * p + hh
            qa = qa_ref[0, h]
            qh = jnp.where(lane < HEAD_DIM, qa.astype(F32), 0.0)
            qh = (pltpu.roll(qh, HEAD_DIM, 1) if hh else qh).astype(BF16)
            s_past = _dot(qh, kt) + bias[h:h + 1, :]
            s_new = jnp.where(new_ok, _dot_nt(qa, _pad_rows(ka_ref[0, h], Q_PAD)), -jnp.inf)
            m = jnp.maximum(jnp.max(s_past, axis=1, keepdims=True), jnp.max(s_new, axis=1, keepdims=True))
            p_past = jnp.exp2(s_past - m)
            p_new = jnp.exp2(s_new - m)
            l = jnp.sum(p_past, axis=1, keepdims=True) + jnp.sum(p_new, axis=1, keepdims=True)
            acc = _dot_nt(p_past.astype(BF16), vt) + _dot(p_new.astype(BF16), vn)
            res.append(acc / l)
        pairs.append(jnp.where(lane < HEAD_DIM, res[0], res[1]))
    o_ref[0] = jnp.concatenate(pairs, axis=1)


def _attention_cache(qa, ka, v, ckt, cvt, clft):
    b, _, t, _ = qa.shape
    plen = ckt.shape[2]
    return pl.pallas_call(
        _attn_cache_kernel,
        grid=(b,),
        in_specs=[
            pl.BlockSpec((1, HEADS, t, LANES), lambda bi: (bi, 0, 0, 0)),
            pl.BlockSpec((1, HEADS, t, LANES), lambda bi: (bi, 0, 0, 0)),
            pl.BlockSpec((1, t, ATTN_W), lambda bi: (bi, 0, 0)),
            pl.BlockSpec((1, ATTN_W, plen), lambda bi: (bi, 0, 0)),
            pl.BlockSpec((1, ATTN_W, plen), lambda bi: (bi, 0, 0)),
            pl.BlockSpec((1, HEADS, plen), lambda bi: (bi, 0, 0)),
        ],
        out_specs=pl.BlockSpec((1, t, ATTN_W), lambda bi: (bi, 0, 0)),
        out_shape=jax.ShapeDtypeStruct((b, t, ATTN_W), F32),
        compiler_params=pltpu.CompilerParams(
            dimension_semantics=("arbitrary",), vmem_limit_bytes=VMEM_LIMIT),
        name="attention_cache",
    )(qa, ka, v, ckt, cvt, clft)


def _mix_kernel(x_ref, o_ref, mb_ref, wo_ref, ga_ref, gf_ref, wr_ref, br_ref,
                rows_ref, slot_ref, cnt_ref, carry):
    i = pl.program_id(0)
    tm = slot_ref.shape[2]
    nsub = x_ref.shape[0] // tm
    row = lax.broadcasted_iota(jnp.int32, (EPG, tm), 0)
    row2 = lax.broadcasted_iota(jnp.int32, (2 * EPG, tm), 0)
    tok = lax.broadcasted_iota(jnp.int32, (2 * EPG, tm), 1)
    r2 = lax.broadcasted_iota(jnp.int32, (tm, tm), 0)
    c2 = lax.broadcasted_iota(jnp.int32, (tm, tm), 1)
    earlier = jnp.where(r2 < c2, 1.0, 0.0).astype(BF16)

    @pl.when(i == 0)
    def _():
        carry[...] = jnp.zeros_like(carry)

    total = carry[:, 0:1]
    for j in range(nsub):
        rs = slice(j * tm, (j + 1) * tm)
        oa = _rms(o_ref[rs, :], ga_ref[...]).astype(BF16)
        merged = jnp.concatenate([oa, mb_ref[rs, :]], axis=1)
        x2 = x_ref[rs, :] + _dot(merged, wo_ref[...])
        rows_ref[rs, D_MODEL:2 * D_MODEL] = x2
        xb = _rms(x2, gf_ref[...]).astype(BF16)
        rows_ref[rs, 0:D_MODEL] = xb.astype(F32)

        lt = (_dot(xb, wr_ref[...]) + br_ref[...]).T
        lg = jnp.where(row < N_GROUPS, lt[0:EPG], -jnp.inf)
        eg = jnp.exp(lg - jnp.max(lg, axis=0, keepdims=True))
        pg = eg / jnp.sum(eg, axis=0, keepdims=True)
        pg_top = jnp.max(pg, axis=0, keepdims=True)
        g_top = jnp.min(jnp.where(pg == pg_top, row, EPG), axis=0, keepdims=True)

        le = lt[EPG:2 * EPG]
        for g in range(1, N_GROUPS):
            le = jnp.where(g_top == g, lt[EPG * (g + 1):EPG * (g + 2)], le)
        ee = jnp.exp(le - jnp.max(le, axis=0, keepdims=True))
        pe = ee / jnp.sum(ee, axis=0, keepdims=True)
        v1 = jnp.max(pe, axis=0, keepdims=True)
        i1 = jnp.min(jnp.where(pe == v1, row, EPG), axis=0, keepdims=True)
        pe2 = jnp.where(row == i1, -1.0, pe)
        v2 = jnp.max(pe2, axis=0, keepdims=True)
        i2 = jnp.min(jnp.where(pe2 == v2, row, EPG), axis=0, keepdims=True)
        den = v1 + v2
        wg = jnp.where(row == i1, v1 / den * pg_top, 0.0) + jnp.where(row == i2, v2 / den * pg_top, 0.0)
        meta = jnp.where(row2 == META_ID, ((i * nsub + j) * tm + tok).astype(F32),
                         jnp.concatenate([wg, jnp.zeros((EPG, tm), F32)], axis=0))
        rows_ref[rs, 2 * D_MODEL:] = jnp.concatenate([meta, jnp.zeros((LANES - 2 * EPG, tm), F32)], axis=0).T

        onehot = jnp.where(row == g_top, 1.0, 0.0)
        rank = jnp.sum(onehot * (_dot(onehot.astype(BF16), earlier) + total), axis=0, keepdims=True)
        slot_ref[i * nsub + j] = g_top * (1 << GROUP_SHIFT) + rank.astype(jnp.int32)
        total = total + jnp.sum(onehot, axis=1, keepdims=True)

    carry[...] = jnp.broadcast_to(total, carry.shape)
    cnt_ref[...] = carry[...].astype(jnp.int32)

    @pl.when(i == pl.num_programs(0) - 1)
    def _():
        padded = jnp.floor((total + (EXPERT_TILE - 1)) * (1.0 / EXPERT_TILE)) * EXPERT_TILE
        o1 = padded[0:1]
        o2 = o1 + padded[1:2]
        o3 = o2 + padded[2:3]
        code = slot_ref[...]
        g = code >> GROUP_SHIFT
        first = jnp.where(g == 0, 0.0, jnp.where(g == 1, o1, jnp.where(g == 2, o2, o3)))
        slot_ref[...] = first.astype(jnp.int32) + (code & ((1 << GROUP_SHIFT) - 1))


def _mix(x, o, mb, wo, ga, gf, wr, br, tm, nsub):
    n = x.shape[0]
    assert n < (1 << GROUP_SHIFT)
    tb = tm * nsub
    full = lambda shape: pl.BlockSpec(shape, lambda i: (0,) * len(shape))
    return pl.pallas_call(
        _mix_kernel,
        grid=(n // tb,),
        in_specs=[
            pl.BlockSpec((tb, D_MODEL), lambda i: (i, 0)),
            pl.BlockSpec((tb, ATTN_W), lambda i: (i, 0)),
            pl.BlockSpec((tb, CONV_W), lambda i: (i, 0)),
            full((D_MODEL, D_MODEL)),
            full((1, ATTN_W)),
            full((1, D_MODEL)),
            full((D_MODEL, LANES)),
            full((1, LANES)),
        ],
        out_specs=[
            pl.BlockSpec((tb, ROW_W), lambda i: (i, 0)),
            full((n // tm, 1, tm)),
            full((EPG, LANES)),
        ],
        out_shape=[
            jax.ShapeDtypeStruct((n, ROW_W), F32),
            jax.ShapeDtypeStruct((n // tm, 1, tm), jnp.int32),
            jax.ShapeDtypeStruct((EPG, LANES), jnp.int32),
        ],
        scratch_shapes=[pltpu.VMEM((EPG, LANES), F32)],
        compiler_params=pltpu.CompilerParams(
            dimension_semantics=("arbitrary",), vmem_limit_bytes=VMEM_LIMIT),
        name="mix",
    )(x, o, mb, wo, ga, gf, wr, br)


def _sorted_tiles(n):
    return n // EXPERT_TILE + N_GROUPS


def _group_tiles(cnt_ref):
    nt = [(cnt_ref[g] + (EXPERT_TILE - 1)) >> TILE_SHIFT for g in range(N_GROUPS)]
    s1 = nt[0]
    s2 = s1 + nt[1]
    s3 = s2 + nt[2]
    return s1, s2, s3, s3 + nt[3]


def _scatter_kernel(cnt_ref, rows_ref, slot_ref, xs_ref, zeros, sem, zsem):
    tm = rows_ref.shape[0]

    @pl.when(pl.program_id(0) == 0)
    def _():
        zeros[...] = jnp.zeros_like(zeros)
        s1, s2, s3, tot = _group_tiles(cnt_ref)
        ends = (s1, s2, s3, tot)

        def fill(tile):
            return pltpu.make_async_copy(zeros, xs_ref.at[pl.ds(tile * EXPERT_TILE, EXPERT_TILE), :], zsem)

        for g in range(N_GROUPS):
            partial = (cnt_ref[g] & (EXPERT_TILE - 1)) != 0

            @pl.when(partial)
            def _():
                fill(ends[g] - 1).start()
                fill(ends[g] - 1).wait()

        def fill_tail(tile, c):
            fill(tile).start()
            fill(tile).wait()
            return c

        lax.fori_loop(tot, xs_ref.shape[0] // EXPERT_TILE, fill_tail, 0)

    copy_of_row = lambda r: pltpu.make_async_copy(
        rows_ref.at[pl.ds(r, 1), :], xs_ref.at[pl.ds(slot_ref[0, 0, r], 1), :], sem)
    for r in range(tm):
        copy_of_row(r).start(priority=r % 2)
    for r in range(tm):
        copy_of_row(r).wait()


def _scatter(cnt, rows, slots, tm):
    n = rows.shape[0]
    return pl.pallas_call(
        _scatter_kernel,
        grid_spec=pltpu.PrefetchScalarGridSpec(
            num_scalar_prefetch=1,
            grid=(n // tm,),
            in_specs=[
                pl.BlockSpec((tm, ROW_W), lambda i, c: (i, 0)),
                pl.BlockSpec((1, 1, tm), lambda i, c: (i, 0, 0), memory_space=pltpu.SMEM),
            ],
            out_specs=pl.BlockSpec(memory_space=pl.ANY),
            scratch_shapes=[pltpu.VMEM((EXPERT_TILE, ROW_W), F32),
                            pltpu.SemaphoreType.DMA(()), pltpu.SemaphoreType.DMA(())],
        ),
        out_shape=jax.ShapeDtypeStruct((_sorted_tiles(n) * EXPERT_TILE, ROW_W), F32),
        compiler_params=pltpu.CompilerParams(
            dimension_semantics=("arbitrary",), vmem_limit_bytes=VMEM_LIMIT),
        name="scatter",
    )(cnt, rows, slots)


def _tile_of(t, cnt_ref):
    s1, s2, s3, tot = _group_tiles(cnt_ref)
    tc = jnp.minimum(t, tot - 1)
    g = (tc >= s1).astype(jnp.int32) + (tc >= s2).astype(jnp.int32) + (tc >= s3).astype(jnp.int32)
    first = jnp.where(g == 0, 0, jnp.where(g == 1, s1, jnp.where(g == 2, s2, s3)))
    cnt = jnp.where(g == 0, cnt_ref[0], jnp.where(g == 1, cnt_ref[1], jnp.where(g == 2, cnt_ref[2], cnt_ref[3])))
    return tc, g, jnp.where(t < tot, cnt - (tc - first) * EXPERT_TILE, 0)


def _experts_kernel(cnt_ref, xs_ref, wgu_ref, wdn_ref, gn_ref, y_hbm, yb0, yb1, idv, ids, pend, sem, isem):
    t = pl.program_id(0)
    _, _, cur_valid = _tile_of(t, cnt_ref)
    prev_valid = jnp.where(t > 0, jnp.clip(_tile_of(t - 1, cnt_ref)[2], 0, EXPERT_TILE), 0)
    fast = (prev_valid == EXPERT_TILE) & (cur_valid > 0)

    @pl.when(t == 0)
    def _():
        pend[0] = 0

    def drain(k):
        one = pltpu.make_async_copy(yb0.at[pl.ds(0, 1), :], y_hbm.at[pl.ds(0, 1), :], sem.at[k])

        @pl.when(pend[0] == EXPERT_TILE)
        def _():
            for _ in range(EXPERT_TILE):
                one.wait()

        @pl.when(pend[0] != EXPERT_TILE)
        def _():
            def body(r, c):
                one.wait()
                return c
            lax.fori_loop(0, pend[0], body, 0)

    def compute(ybuf, par):
        idv[par:par + 1, :] = xs_ref[:, 2 * D_MODEL:].T[META_ID:META_ID + 1, :].astype(jnp.int32)
        pltpu.make_async_copy(idv.at[par], ids.at[par], isem).start()
        xb = xs_ref[:, 0:D_MODEL].astype(BF16)
        wts = xs_ref[:, 2 * D_MODEL:]
        tot = xs_ref[:, D_MODEL:2 * D_MODEL]
        for e in range(EPG):
            gu = _dot(xb, wgu_ref[0, e])
            gate = gu[:, :D_EXPERT]
            hmid = gate * jax.nn.sigmoid(gate) * gu[:, D_EXPERT:] * wts[:, e:e + 1]
            tot = tot + _dot(hmid.astype(BF16), wdn_ref[0, e])
        y = _rms(tot, gn_ref[...])
        drain(1 - par)
        ybuf[...] = y

    def ids_wait(par):
        pltpu.make_async_copy(idv.at[par], ids.at[par], isem).wait()

    def row_copy(ybuf, par, r):
        return pltpu.make_async_copy(ybuf.at[pl.ds(r, 1), :], y_hbm.at[pl.ds(ids[1 - par, r], 1), :], sem.at[par])

    for par, (ycur, yprev) in enumerate(((yb0, yb1), (yb1, yb0))):
        @pl.when(((t % 2) == par) & fast)
        def _():
            ids_wait(1 - par)
            for r in range(EXPERT_TILE):
                row_copy(yprev, par, r).start(priority=r % 2)
            compute(ycur, par)

        @pl.when(((t % 2) == par) & jnp.logical_not(fast))
        def _():
            @pl.when(prev_valid > 0)
            def _():
                ids_wait(1 - par)

                def start(r, c):
                    row_copy(yprev, par, r).start()
                    return c
                lax.fori_loop(0, prev_valid, start, 0)

            @pl.when(cur_valid > 0)
            def _():
                compute(ycur, par)

            @pl.when(cur_valid <= 0)
            def _():
                drain(1 - par)

        @pl.when((t % 2) == par)
        def _():
            pend[0] = prev_valid

            @pl.when(t == pl.num_programs(0) - 1)
            def _():
                drain(par)


def _experts(cnt, xs, wgu, wdn, gn, n):
    tiles = xs.shape[0] // EXPERT_TILE
    grp = lambda t, c: (_tile_of(t, c)[1], 0, 0, 0)
    return pl.pallas_call(
        _experts_kernel,
        grid_spec=pltpu.PrefetchScalarGridSpec(
            num_scalar_prefetch=1,
            grid=(tiles + 1,),
            in_specs=[
                pl.BlockSpec((EXPERT_TILE, ROW_W), lambda t, c: (_tile_of(t, c)[0], 0)),
                pl.BlockSpec((1, EPG, D_MODEL, 2 * D_EXPERT), grp),
                pl.BlockSpec((1, EPG, D_EXPERT, D_MODEL), grp),
                pl.BlockSpec((1, D_MODEL), lambda t, c: (0, 0)),
            ],
            out_specs=pl.BlockSpec(memory_space=pl.ANY),
            scratch_shapes=[pltpu.VMEM((EXPERT_TILE, D_MODEL), F32), pltpu.VMEM((EXPERT_TILE, D_MODEL), F32),
                            pltpu.VMEM((8, EXPERT_TILE), jnp.int32), pltpu.SMEM((2, EXPERT_TILE), jnp.int32),
                            pltpu.SMEM((1,), jnp.int32), pltpu.SemaphoreType.DMA((2,)), pltpu.SemaphoreType.DMA(())],
        ),
        out_shape=jax.ShapeDtypeStruct((n, D_MODEL), F32),
        compiler_params=pltpu.CompilerParams(
            dimension_semantics=("arbitrary",), vmem_limit_bytes=VMEM_LIMIT),
        name="experts",
    )(cnt, xs, wgu, wdn, gn)


def _forget_lanes(f):
    z = lambda n: jnp.zeros(f.shape[:-1] + (n,), f.dtype)
    return jnp.concatenate([f, z(F_HI - HEADS), f, f, f, z(LANES - F_LO - HEADS)], axis=-1)


def _trunk(x, attend, state, w, tm_proj, nb_proj, tm_tok, nsub_tok):
    b, s, _ = x.shape
    k, v, lf, qa, ka, mb, ncv, *built = _proj(
        x, state, w["gm"], w["w_proj"], w["bias_f"], w["cw"], w["gc"], tm_proj, nb_proj)
    if built:
        w["w_proj"] = built[0]
    o = attend(qa, ka, v)
    n = b * s
    rows, slots, cnt = _mix(x.reshape(n, D_MODEL), o.reshape(n, ATTN_W), mb.reshape(n, CONV_W),
                            w["wo"], w["ga"], w["gf"], w["wr"], w["br"], tm_tok, nsub_tok)
    cnt = cnt[:, 0]
    y = _experts(cnt, _scatter(cnt, rows, slots, tm_tok), w["wgu"], w["wdn"], w["gn"], n)
    return (y.reshape(b, s, D_MODEL), k.reshape(1, b, s, HEADS, HEAD_DIM), v.reshape(1, b, s, HEADS, HEAD_DIM),
            lf.transpose(0, 2, 1).reshape(1, b, s, HEADS), ncv.reshape(1, b, 2, CONV_W))


def kernel(x_prompt, x_sample, cache_k, cache_v, cache_logf, state_conv, norm_mix_g, w_in, b_forget, conv_w, norm_attn_g, norm_conv_g, w_out, norm_ffn_g, w_router_group, b_router_group, w_router_expert, b_router_expert, w_expert_gate_up, w_expert_down, norm_final_g):
    assert w_in.shape[0] == 1, "single-layer trunk"
    wr = jnp.concatenate(
        [w_router_group[0], jnp.zeros((D_MODEL, EPG - N_GROUPS), F32),
         w_router_expert[0].transpose(1, 0, 2).reshape(D_MODEL, N_GROUPS * EPG),
         jnp.zeros((D_MODEL, LANES - EPG - N_GROUPS * EPG), F32)], axis=1).astype(BF16)
    br = jnp.concatenate(
        [b_router_group[0], jnp.zeros((EPG - N_GROUPS,), F32), b_router_expert[0].reshape(-1),
         jnp.zeros((LANES - EPG - N_GROUPS * EPG,), F32)]).reshape(1, LANES)
    w = dict(
        gm=norm_mix_g[0].reshape(1, D_MODEL), w_proj=w_in[0].T, bias_f=_forget_lanes(b_forget[0]).reshape(1, LANES),
        cw=jnp.concatenate([conv_w[0], jnp.zeros((8 - conv_w.shape[1], CONV_W), F32)], axis=0),
        gc=norm_conv_g[0].reshape(1, CONV_W), wo=w_out[0].astype(BF16), ga=norm_attn_g[0].reshape(1, ATTN_W),
        gf=norm_ffn_g[0].reshape(1, D_MODEL), wr=wr, br=br,
        wgu=w_expert_gate_up[0], wdn=w_expert_down[0],
        gn=norm_final_g.reshape(1, D_MODEL))

    bp, sp, _ = x_prompt.shape
    bs, ss, _ = x_sample.shape
    plen = cache_k.shape[2]
    yp, kp, vp, lfp, cvp = _trunk(x_prompt, _attention, jnp.zeros((bp, 8, CONV_W), F32), w, 512, 2, 512, 2)

    ckt = cache_k[0].transpose(0, 2, 3, 1).reshape(bs, ATTN_W, plen)
    cvt = cache_v[0].transpose(0, 2, 3, 1).reshape(bs, ATTN_W, plen)
    clft = cache_logf[0].transpose(0, 2, 1)
    st = jnp.concatenate([jnp.zeros((bs, 6, CONV_W), F32), state_conv[0]], axis=1)
    attend_s = lambda qa, ka, v: _attention_cache(qa, ka, v, ckt, cvt, clft)
    ys, ks, vs, lfs, cvs = _trunk(x_sample, attend_s, st, w, ss, 2, bs * ss, 1)
    return (yp, ys, kp, vp, lfp, cvp, ks, vs, lfs, cvs)
```

```python
import functools
import math

import jax
import jax.numpy as jnp
from jax import lax
from jax.experimental import pallas as pl
from jax.experimental.pallas import tpu as pltpu

F32 = jnp.float32
BF16 = jnp.bfloat16

D_MODEL = 1024
HEADS = 8
HEAD_DIM = 64
ATTN_W = HEADS * HEAD_DIM
CONV_W = 512
N_GROUPS = 4
EPG = 8
D_EXPERT = 256
EPS = 1e-6
LOG2E = math.log2(math.e)
Q_SCALE = HEAD_DIM ** -0.5 * LOG2E

LANES = 128
COL_Q, COL_K, COL_V, COL_F, COL_BG, COL_CG, COL_H = 0, 512, 1024, 1536, 1664, 2176, 2688
PROJ_PAD = 3200
W_SECTIONS = ((0, COL_Q), (512, COL_K), (1024, COL_V), (1544, COL_BG), (2056, COL_CG), (2568, COL_H))
W_F_ROW = 1536
F_OUT, F_HI, F_MID, F_LO = 0, 64, 72, 80

GROUP_SHIFT = 20
ROW_W = 2 * D_MODEL + LANES
META_ID = EPG
TILE_SHIFT = 9
EXPERT_TILE = 1 << TILE_SHIFT

VMEM_LIMIT = 60 * 1024 * 1024


def _dot(a, b):
    return jnp.dot(a, b, preferred_element_type=F32)


def _dot_nt(a, b):
    return lax.dot_general(a, b, (((1,), (1,)), ((), ())), preferred_element_type=F32)


def _rms(x, g):
    return x * lax.rsqrt(jnp.mean(x * x, axis=-1, keepdims=True) + EPS) * g


def _log_sigmoid(x):
    return jnp.minimum(x, 0.0) - jnp.log(1.0 + jnp.exp(-jnp.abs(x)))


def _pad_rows(x, n):
    return jnp.concatenate([x, jnp.zeros((n - x.shape[0], x.shape[1]), x.dtype)], axis=0)


def _split3(x):
    a1 = x.astype(BF16)
    r1 = x - a1.astype(F32)
    a2 = r1.astype(BF16)
    a3 = (r1 - a2.astype(F32)).astype(BF16)
    return a1, a2, a3


def _cumsum_rows(x):
    n = x.shape[0]
    r = lax.broadcasted_iota(jnp.int32, (n, n), 0)
    c = lax.broadcasted_iota(jnp.int32, (n, n), 1)
    tri = jnp.where(r >= c, 1.0, 0.0).astype(BF16)
    parts = _dot(tri, jnp.concatenate(_split3(x), axis=1))
    return parts[:, :LANES] + parts[:, LANES:2 * LANES] + parts[:, 2 * LANES:]


def _key_bias_lanes(cneg):
    lane = lax.broadcasted_iota(jnp.int32, cneg.shape, 1)
    hi = cneg.astype(BF16).astype(F32)
    r1 = cneg - hi
    mid = r1.astype(BF16).astype(F32)
    lo = r1 - mid
    out = jnp.where(lane < F_MID, hi, jnp.where(lane < F_LO, mid, lo))
    return jnp.where((lane >= F_HI) & (lane < F_LO + HEADS), out, 0.0)


def _head_block(x, h):
    blk = x[:, (h // 2) * LANES:(h // 2 + 1) * LANES]
    return pltpu.roll(blk, HEAD_DIM, 1) if h % 2 else blk


def _key_operand(k, bias_lanes, h):
    lane = lax.broadcasted_iota(jnp.int32, bias_lanes.shape, 1)
    return jnp.where(lane < HEAD_DIM, _head_block(k, h), bias_lanes).astype(BF16)


def _prepare_weight(wt_hbm, w_ref, stage, sem):
    for r0, c0 in W_SECTIONS:
        cp = pltpu.make_async_copy(wt_hbm.at[pl.ds(r0, ATTN_W), :], stage, sem)
        cp.start()
        cp.wait()
        w_ref[:, c0:c0 + ATTN_W] = stage[...].T.astype(BF16)
    cp = pltpu.make_async_copy(wt_hbm.at[pl.ds(W_F_ROW, HEADS), :], stage.at[pl.ds(0, HEADS), :], sem)
    cp.start()
    cp.wait()
    f = stage[0:HEADS, :]
    z = lambda n: jnp.zeros((n, D_MODEL), F32)
    fblk = jnp.concatenate([f, z(F_HI - HEADS), f, f, f, z(LANES - F_LO - HEADS)], axis=0)
    w_ref[:, COL_F:COL_F + LANES] = fblk.T.astype(BF16)


def _proj_kernel(prepare, x_ref, st_ref, gm_ref, w_in_ref, bf_ref, cw_ref, gc_ref,
                 k_ref, v_ref, lf_ref, qa_ref, ka_ref, mb_ref, ncv_ref, *rest):
    i = pl.program_id(1)
    nb, tm, _ = x_ref.shape
    if prepare:
        w_ref, ubuf, ccar, stage, sem = rest

        @pl.when((pl.program_id(0) == 0) & (i == 0))
        def _():
            _prepare_weight(w_in_ref, w_ref, stage, sem)
    else:
        w_ref = w_in_ref
        ubuf, ccar = rest

    @pl.when(i == 0)
    def _():
        ccar[...] = jnp.zeros_like(ccar)
        ubuf[:, 0:8, :] = st_ref[...]

    @pl.when(i > 0)
    def _():
        ubuf[:, 0:8, :] = ubuf[:, tm:tm + 8, :]

    lane = lax.broadcasted_iota(jnp.int32, (tm, LANES), 1)
    pad = (-tm) % LANES
    for bb in range(nb):
        xn = _rms(x_ref[bb], gm_ref[...]).astype(BF16)
        proj = lambda c0, n: _dot(xn, w_ref[:, c0:c0 + n])

        lfb = _log_sigmoid(proj(COL_F, LANES) + bf_ref[...])
        lf_ref[bb] = (_pad_rows(lfb, tm + pad) if pad else lfb).T[F_OUT:F_OUT + HEADS, 0:tm]
        c = _cumsum_rows(lfb) + ccar[bb, 0:1, :]
        ccar[bb] = jnp.broadcast_to(c[tm - 1:tm, :], ccar.shape[1:])
        bias_lanes = _key_bias_lanes(c * (-LOG2E))
        k = proj(COL_K, ATTN_W)
        k_ref[bb] = k
        for h in range(HEADS):
            ka_ref[bb, h] = _key_operand(k, bias_lanes, h)
        q = proj(COL_Q, ATTN_W)
        for h in range(HEADS):
            sel = jnp.where((lane == F_HI + h) | (lane == F_MID + h) | (lane == F_LO + h), 1.0, 0.0)
            qa_ref[bb, h] = jnp.where(lane < HEAD_DIM, _head_block(q, h) * Q_SCALE, sel).astype(BF16)
        v_ref[bb] = proj(COL_V, ATTN_W)

        u = proj(COL_CG, CONV_W) * proj(COL_H, CONV_W)
        ubuf[bb, 8:8 + tm, :] = u
        y = (cw_ref[0:1, :] * ubuf[bb, 6:6 + tm, :] + cw_ref[1:2, :] * ubuf[bb, 7:7 + tm, :]
             + cw_ref[2:3, :] * u)
        ob = proj(COL_BG, CONV_W) * y
        mb_ref[bb] = _rms(ob, gc_ref[...]).astype(BF16)
        ncv_ref[bb] = u[tm - 2:tm, :]


def _proj(x, state, gm, w, bias_f, cw, gc, tm, nb):
    b, s, _ = x.shape
    prepare = w.dtype == F32
    full = lambda shape: pl.BlockSpec(shape, lambda bi, i: (0,) * len(shape))
    out_specs = [
        pl.BlockSpec((nb, tm, ATTN_W), lambda bi, i: (bi, i, 0)),
        pl.BlockSpec((nb, tm, ATTN_W), lambda bi, i: (bi, i, 0)),
        pl.BlockSpec((nb, HEADS, tm), lambda bi, i: (bi, 0, i)),
        pl.BlockSpec((nb, HEADS, tm, LANES), lambda bi, i: (bi, 0, i, 0)),
        pl.BlockSpec((nb, HEADS, tm, LANES), lambda bi, i: (bi, 0, i, 0)),
        pl.BlockSpec((nb, tm, CONV_W), lambda bi, i: (bi, i, 0)),
        pl.BlockSpec((nb, 2, CONV_W), lambda bi, i: (bi, 0, 0)),
    ]
    out_shape = [
        jax.ShapeDtypeStruct((b, s, ATTN_W), F32),
        jax.ShapeDtypeStruct((b, s, ATTN_W), F32),
        jax.ShapeDtypeStruct((b, HEADS, s), F32),
        jax.ShapeDtypeStruct((b, HEADS, s, LANES), BF16),
        jax.ShapeDtypeStruct((b, HEADS, s, LANES), BF16),
        jax.ShapeDtypeStruct((b, s, CONV_W), BF16),
        jax.ShapeDtypeStruct((b, 2, CONV_W), F32),
    ]
    scratch = [pltpu.VMEM((nb, tm + 8, CONV_W), F32), pltpu.VMEM((nb, 8, LANES), F32)]
    if prepare:
        out_specs.append(full((D_MODEL, PROJ_PAD)))
        out_shape.append(jax.ShapeDtypeStruct((D_MODEL, PROJ_PAD), BF16))
        scratch += [pltpu.VMEM((ATTN_W, D_MODEL), F32), pltpu.SemaphoreType.DMA(())]
    return pl.pallas_call(
        functools.partial(_proj_kernel, prepare),
        grid=(b // nb, s // tm),
        in_specs=[
            pl.BlockSpec((nb, tm, D_MODEL), lambda bi, i: (bi, i, 0)),
            pl.BlockSpec((nb, 8, CONV_W), lambda bi, i: (bi, 0, 0)),
            full((1, D_MODEL)),
            pl.BlockSpec(memory_space=pl.ANY) if prepare else full((D_MODEL, PROJ_PAD)),
            full((1, LANES)),
            full((8, CONV_W)),
            full((1, CONV_W)),
        ],
        out_specs=out_specs,
        out_shape=out_shape,
        scratch_shapes=scratch,
        compiler_params=pltpu.CompilerParams(
            dimension_semantics=("arbitrary", "arbitrary"), vmem_limit_bytes=VMEM_LIMIT),
        name="proj",
    )(x, state, gm, w, bias_f, cw, gc)


ATT_BLK = 256


def _attn_kernel(qa_ref, ka_ref, v_ref, wgu_ref, wdn_ref, o_ref, wgu_bf_ref, wdn_bf_ref, sbuf):
    wgu_bf_ref[...] = wgu_ref[...].astype(BF16)
    wdn_bf_ref[...] = wdn_ref[...].astype(BF16)
    blk = ATT_BLK
    vt = v_ref[0].T.astype(BF16)
    r = lax.broadcasted_iota(jnp.int32, (blk, blk), 0)
    c = lax.broadcasted_iota(jnp.int32, (blk, blk), 1)
    causal = r <= c
    units = [(iq, hh) for iq in range(ka_ref.shape[2] // blk) for hh in range(2)]

    def scores(u):
        iq, hh = units[u]
        n = (iq + 1) * blk
        sbuf[u % nbuf, 0:n, :] = _dot_nt(ka_ref[0, hh, 0:n, :], qa_ref[0, hh, iq * blk:n, :])

    nbuf = sbuf.shape[0]
    for u in range(nbuf - 1):
        scores(u)
    outs = []
    for u, (iq, hh) in enumerate(units):
        if u + nbuf - 1 < len(units):
            scores(u + nbuf - 1)
        noff = iq * blk
        rows = slice(hh * HEAD_DIM, (hh + 1) * HEAD_DIM)
        s_d = jnp.where(causal, sbuf[u % nbuf, noff:noff + blk, :], -jnp.inf)
        m = jnp.max(s_d, axis=0, keepdims=True)
        if iq > 0:
            m = jnp.maximum(m, jnp.max(sbuf[u % nbuf, 0:noff, :], axis=0, keepdims=True))
        p_d = jnp.exp2(s_d - m)
        l = jnp.sum(p_d, axis=0, keepdims=True)
        acc = _dot(vt[rows, noff:noff + blk], p_d.astype(BF16))
        if iq > 0:
            p_o = jnp.exp2(sbuf[u % nbuf, 0:noff, :] - m)
            l = l + jnp.sum(p_o, axis=0, keepdims=True)
            acc = acc + _dot(vt[rows, 0:noff], p_o.astype(BF16))
        outs.append(acc / l)
        if hh == 1:
            o_ref[0, noff:noff + blk, :] = jnp.concatenate(outs, axis=0).T
            outs = []


def _attention(qa, ka, v, wgu, wdn):
    b, _, s, _ = qa.shape
    pairs = HEADS // 2
    ne = wgu.shape[0] // (b * pairs)
    assert ne * b * pairs == wgu.shape[0] == wdn.shape[0]
    wspec = lambda w: pl.BlockSpec((ne,) + w.shape[1:], lambda bi, p: (bi * pairs + p, 0, 0))
    return pl.pallas_call(
        _attn_kernel,
        grid=(b, pairs),
        in_specs=[
            pl.BlockSpec((1, 2, s, LANES), lambda bi, p: (bi, p, 0, 0)),
            pl.BlockSpec((1, 2, s, LANES), lambda bi, p: (bi, p, 0, 0)),
            pl.BlockSpec((1, s, LANES), lambda bi, p: (bi, 0, p)),
            wspec(wgu),
            wspec(wdn),
        ],
        out_specs=[pl.BlockSpec((1, s, LANES), lambda bi, p: (bi, 0, p)), wspec(wgu), wspec(wdn)],
        out_shape=[jax.ShapeDtypeStruct((b, s, ATTN_W), F32),
                   jax.ShapeDtypeStruct(wgu.shape, BF16), jax.ShapeDtypeStruct(wdn.shape, BF16)],
        scratch_shapes=[pltpu.VMEM((3, s, ATT_BLK), F32)],
        compiler_params=pltpu.CompilerParams(
            dimension_semantics=("arbitrary", "arbitrary"), vmem_limit_bytes=VMEM_LIMIT),
        name="attention",
    )(qa, ka, v, wgu, wdn)


Q_PAD = 128


def _cumsum_lanes(x):
    n = x.shape[1]
    r = lax.broadcasted_iota(jnp.int32, (n, n), 0)
    c = lax.broadcasted_iota(jnp.int32, (n, n), 1)
    tri = jnp.where(r <= c, 1.0, 0.0).astype(BF16)
    terms = jnp.concatenate([a.astype(F32) for a in _split3(x)], axis=0).astype(BF16)
    parts = _dot(terms, tri)
    rows = x.shape[0]
    return parts[0:rows] + parts[rows:2 * rows] + parts[2 * rows:]


def _attn_cache_kernel(qa_ref, ka_ref, v_ref, ckt_ref, cvt_ref, clft_ref, o_ref):
    t = qa_ref.shape[2]
    plen = ckt_ref.shape[2]
    cinc = _cumsum_lanes(clft_ref[0])
    bias = (cinc - cinc[:, plen - 1:plen]) * (-LOG2E)
    lane = lax.broadcasted_iota(jnp.int32, (t, LANES), 1)
    rq = lax.broadcasted_iota(jnp.int32, (t, Q_PAD), 0)
    ck = lax.broadcasted_iota(jnp.int32, (t, Q_PAD), 1)
    new_ok = ck <= rq

    pairs = []
    for p in range(HEADS // 2):
        rows = slice(p * LANES, (p + 1) * LANES)
        kt = ckt_ref[0, rows, :].astype(BF16)
        vt = cvt_ref[0, rows, :].astype(BF16)
        vn = _pad_rows(v_ref[0][:, rows], Q_PAD).astype(BF16)
        res = []
        for hh in range(2):
            h = 2 * p + hh
            qa = qa_ref[0, h]
            qh = jnp.where(lane < HEAD_DIM, qa.astype(F32), 0.0)
            qh = (pltpu.roll(qh, HEAD_DIM, 1) if hh else qh).astype(BF16)
            s_past = _dot(qh, kt) + bias[h:h + 1, :]
            s_new = jnp.where(new_ok, _dot_nt(qa, _pad_rows(ka_ref[0, h], Q_PAD)), -jnp.inf)
            m = jnp.maximum(jnp.max(s_past, axis=1, keepdims=True), jnp.max(s_new, axis=1, keepdims=True))
            p_past = jnp.exp2(s_past - m)
            p_new = jnp.exp2(s_new - m)
            l = jnp.sum(p_past, axis=1, keepdims=True) + jnp.sum(p_new, axis=1, keepdims=True)
            acc = _dot_nt(p_past.astype(BF16), vt) + _dot(p_new.astype(BF16), vn)
            res.append(acc / l)
        pairs.append(jnp.where(lane < HEAD_DIM, res[0], res[1]))
    o_ref[0] = jnp.concatenate(pairs, axis=1)


def _attention_cache(qa, ka, v, ckt, cvt, clft):
    b, _, t, _ = qa.shape
    plen = ckt.shape[2]
    return pl.pallas_call(
        _attn_cache_kernel,
        grid=(b,),
        in_specs=[
            pl.BlockSpec((1, HEADS, t, LANES), lambda bi: (bi, 0, 0, 0)),
            pl.BlockSpec((1, HEADS, t, LANES), lambda bi: (bi, 0, 0, 0)),
            pl.BlockSpec((1, t, ATTN_W), lambda bi: (bi, 0, 0)),
            pl.BlockSpec((1, ATTN_W, plen), lambda bi: (bi, 0, 0)),
            pl.BlockSpec((1, ATTN_W, plen), lambda bi: (bi, 0, 0)),
            pl.BlockSpec((1, HEADS, plen), lambda bi: (bi, 0, 0)),
        ],
        out_specs=pl.BlockSpec((1, t, ATTN_W), lambda bi: (bi, 0, 0)),
        out_shape=jax.ShapeDtypeStruct((b, t, ATTN_W), F32),
        compiler_params=pltpu.CompilerParams(
            dimension_semantics=("arbitrary",), vmem_limit_bytes=VMEM_LIMIT),
        name="attention_cache",
    )(qa, ka, v, ckt, cvt, clft)


def _mix_kernel(x_ref, o_ref, mb_ref, wo_ref, ga_ref, gf_ref, wr_ref, br_ref,
                rows_ref, slot_ref, cnt_ref, carry):
    i = pl.program_id(0)
    tm = slot_ref.shape[2]
    nsub = x_ref.shape[0] // tm
    row = lax.broadcasted_iota(jnp.int32, (EPG, tm), 0)
    row2 = lax.broadcasted_iota(jnp.int32, (2 * EPG, tm), 0)
    tok = lax.broadcasted_iota(jnp.int32, (2 * EPG, tm), 1)
    r2 = lax.broadcasted_iota(jnp.int32, (tm, tm), 0)
    c2 = lax.broadcasted_iota(jnp.int32, (tm, tm), 1)
    earlier = jnp.where(r2 < c2, 1.0, 0.0).astype(BF16)

    @pl.when(i == 0)
    def _():
        carry[...] = jnp.zeros_like(carry)

    total = carry[:, 0:1]
    for j in range(nsub):
        rs = slice(j * tm, (j + 1) * tm)
        oa = _rms(o_ref[rs, :], ga_ref[...]).astype(BF16)
        merged = jnp.concatenate([oa, mb_ref[rs, :]], axis=1)
        x2 = x_ref[rs, :] + _dot(merged, wo_ref[...])
        rows_ref[rs, D_MODEL:2 * D_MODEL] = x2
        xb = _rms(x2, gf_ref[...]).astype(BF16)
        rows_ref[rs, 0:D_MODEL] = xb.astype(F32)

        lt = (_dot(xb, wr_ref[...]) + br_ref[...]).T
        lg = jnp.where(row < N_GROUPS, lt[0:EPG], -jnp.inf)
        eg = jnp.exp(lg - jnp.max(lg, axis=0, keepdims=True))
        pg = eg / jnp.sum(eg, axis=0, keepdims=True)
        pg_top = jnp.max(pg, axis=0, keepdims=True)
        g_top = jnp.min(jnp.where(pg == pg_top, row, EPG), axis=0, keepdims=True)

        le = lt[EPG:2 * EPG]
        for g in range(1, N_GROUPS):
            le = jnp.where(g_top == g, lt[EPG * (g + 1):EPG * (g + 2)], le)
        ee = jnp.exp(le - jnp.max(le, axis=0, keepdims=True))
        pe = ee / jnp.sum(ee, axis=0, keepdims=True)
        v1 = jnp.max(pe, axis=0, keepdims=True)
        i1 = jnp.min(jnp.where(pe == v1, row, EPG), axis=0, keepdims=True)
        pe2 = jnp.where(row == i1, -1.0, pe)
        v2 = jnp.max(pe2, axis=0, keepdims=True)
        i2 = jnp.min(jnp.where(pe2 == v2, row, EPG), axis=0, keepdims=True)
        den = v1 + v2
        wg = jnp.where(row == i1, v1 / den * pg_top, 0.0) + jnp.where(row == i2, v2 / den * pg_top, 0.0)
        meta = jnp.where(row2 == META_ID, ((i * nsub + j) * tm + tok).astype(F32),
                         jnp.concatenate([wg, jnp.zeros((EPG, tm), F32)], axis=0))
        rows_ref[rs, 2 * D_MODEL:] = jnp.concatenate([meta, jnp.zeros((LANES - 2 * EPG, tm), F32)], axis=0).T

        onehot = jnp.where(row == g_top, 1.0, 0.0)
        rank = jnp.sum(onehot * (_dot(onehot.astype(BF16), earlier) + total), axis=0, keepdims=True)
        slot_ref[i * nsub + j] = g_top * (1 << GROUP_SHIFT) + rank.astype(jnp.int32)
        total = total + jnp.sum(onehot, axis=1, keepdims=True)

    carry[...] = jnp.broadcast_to(total, carry.shape)
    cnt_ref[...] = carry[...].astype(jnp.int32)

    @pl.when(i == pl.num_programs(0) - 1)
    def _():
        padded = jnp.floor((total + (EXPERT_TILE - 1)) * (1.0 / EXPERT_TILE)) * EXPERT_TILE
        o1 = padded[0:1]
        o2 = o1 + padded[1:2]
        o3 = o2 + padded[2:3]
        code = slot_ref[...]
        g = code >> GROUP_SHIFT
        first = jnp.where(g == 0, 0.0, jnp.where(g == 1, o1, jnp.where(g == 2, o2, o3)))
        slot_ref[...] = first.astype(jnp.int32) + (code & ((1 << GROUP_SHIFT) - 1))


def _mix(x, o, mb, wo, ga, gf, wr, br, tm, nsub):
    n = x.shape[0]
    assert n < (1 << GROUP_SHIFT)
    tb = tm * nsub
    full = lambda shape: pl.BlockSpec(shape, lambda i: (0,) * len(shape))
    return pl.pallas_call(
        _mix_kernel,
        grid=(n // tb,),
        in_specs=[
            pl.BlockSpec((tb, D_MODEL), lambda i: (i, 0)),
            pl.BlockSpec((tb, ATTN_W), lambda i: (i, 0)),
            pl.BlockSpec((tb, CONV_W), lambda i: (i, 0)),
            full((D_MODEL, D_MODEL)),
            full((1, ATTN_W)),
            full((1, D_MODEL)),
            full((D_MODEL, LANES)),
            full((1, LANES)),
        ],
        out_specs=[
            pl.BlockSpec((tb, ROW_W), lambda i: (i, 0)),
            full((n // tm, 1, tm)),
            full((EPG, LANES)),
        ],
        out_shape=[
            jax.ShapeDtypeStruct((n, ROW_W), F32),
            jax.ShapeDtypeStruct((n // tm, 1, tm), jnp.int32),
            jax.ShapeDtypeStruct((EPG, LANES), jnp.int32),
        ],
        scratch_shapes=[pltpu.VMEM((EPG, LANES), F32)],
        compiler_params=pltpu.CompilerParams(
            dimension_semantics=("arbitrary",), vmem_limit_bytes=VMEM_LIMIT),
        name="mix",
    )(x, o, mb, wo, ga, gf, wr, br)


def _sorted_tiles(n):
    return n // EXPERT_TILE + N_GROUPS


def _group_tiles(cnt_ref):
    nt = [(cnt_ref[g] + (EXPERT_TILE - 1)) >> TILE_SHIFT for g in range(N_GROUPS)]
    s1 = nt[0]
    s2 = s1 + nt[1]
    s3 = s2 + nt[2]
    return s1, s2, s3, s3 + nt[3]


def _scatter_kernel(cnt_ref, rows_ref, slot_ref, xs_ref, zeros, sem, zsem):
    tm = rows_ref.shape[0]

    @pl.when(pl.program_id(0) == 0)
    def _():
        zeros[...] = jnp.zeros_like(zeros)
        s1, s2, s3, tot = _group_tiles(cnt_ref)
        ends = (s1, s2, s3, tot)

        def fill(tile):
            return pltpu.make_async_copy(zeros, xs_ref.at[pl.ds(tile * EXPERT_TILE, EXPERT_TILE), :], zsem)

        for g in range(N_GROUPS):
            partial = (cnt_ref[g] & (EXPERT_TILE - 1)) != 0

            @pl.when(partial)
            def _():
                fill(ends[g] - 1).start()
                fill(ends[g] - 1).wait()

        def fill_tail(tile, c):
            fill(tile).start()
            fill(tile).wait()
            return c

        lax.fori_loop(tot, xs_ref.shape[0] // EXPERT_TILE, fill_tail, 0)

    copy_of_row = lambda r: pltpu.make_async_copy(
        rows_ref.at[pl.ds(r, 1), :], xs_ref.at[pl.ds(slot_ref[0, 0, r], 1), :], sem)
    for r in range(tm):
        copy_of_row(r).start(priority=r % 2)
    for r in range(tm):
        copy_of_row(r).wait()


def _scatter(cnt, rows, slots, tm):
    n = rows.shape[0]
    return pl.pallas_call(
        _scatter_kernel,
        grid_spec=pltpu.PrefetchScalarGridSpec(
            num_scalar_prefetch=1,
            grid=(n // tm,),
            in_specs=[
                pl.BlockSpec((tm, ROW_W), lambda i, c: (i, 0)),
                pl.BlockSpec((1, 1, tm), lambda i, c: (i, 0, 0), memory_space=pltpu.SMEM),
            ],
            out_specs=pl.BlockSpec(memory_space=pl.ANY),
            scratch_shapes=[pltpu.VMEM((EXPERT_TILE, ROW_W), F32),
                            pltpu.SemaphoreType.DMA(()), pltpu.SemaphoreType.DMA(())],
        ),
        out_shape=jax.ShapeDtypeStruct((_sorted_tiles(n) * EXPERT_TILE, ROW_W), F32),
        compiler_params=pltpu.CompilerParams(
            dimension_semantics=("arbitrary",), vmem_limit_bytes=VMEM_LIMIT),
        name="scatter",
    )(cnt, rows, slots)


def _tile_of(t, cnt_ref):
    s1, s2, s3, tot = _group_tiles(cnt_ref)
    tc = jnp.minimum(t, tot - 1)
    g = (tc >= s1).astype(jnp.int32) + (tc >= s2).astype(jnp.int32) + (tc >= s3).astype(jnp.int32)
    first = jnp.where(g == 0, 0, jnp.where(g == 1, s1, jnp.where(g == 2, s2, s3)))
    cnt = jnp.where(g == 0, cnt_ref[0], jnp.where(g == 1, cnt_ref[1], jnp.where(g == 2, cnt_ref[2], cnt_ref[3])))
    return tc, g, jnp.where(t < tot, cnt - (tc - first) * EXPERT_TILE, 0)


def _experts_kernel(cnt_ref, xs_ref, wgu_ref, wdn_ref, gn_ref, y_hbm, yb0, yb1, idv, ids, pend, sem, isem):
    t = pl.program_id(0)
    _, _, cur_valid = _tile_of(t, cnt_ref)
    prev_valid = jnp.where(t > 0, jnp.clip(_tile_of(t - 1, cnt_ref)[2], 0, EXPERT_TILE), 0)
    fast = (prev_valid == EXPERT_TILE) & (cur_valid > 0)

    @pl.when(t == 0)
    def _():
        pend[0] = 0

    def drain(k):
        one = pltpu.make_async_copy(yb0.at[pl.ds(0, 1), :], y_hbm.at[pl.ds(0, 1), :], sem.at[k])

        @pl.when(pend[0] == EXPERT_TILE)
        def _():
            for _ in range(EXPERT_TILE):
                one.wait()

        @pl.when(pend[0] != EXPERT_TILE)
        def _():
            def body(r, c):
                one.wait()
                return c
            lax.fori_loop(0, pend[0], body, 0)

    def compute(ybuf, par):
        idv[par:par + 1, :] = xs_ref[:, 2 * D_MODEL:].T[META_ID:META_ID + 1, :].astype(jnp.int32)
        pltpu.make_async_copy(idv.at[par], ids.at[par], isem).start()
        xb = xs_ref[:, 0:D_MODEL].astype(BF16)
        wts = xs_ref[:, 2 * D_MODEL:]
        tot = xs_ref[:, D_MODEL:2 * D_MODEL]
        for e in range(EPG):
            gu = _dot(xb, wgu_ref[0, e])
            gate = gu[:, :D_EXPERT]
            hmid = gate * jax.nn.sigmoid(gate) * gu[:, D_EXPERT:] * wts[:, e:e + 1]
            tot = tot + _dot(hmid.astype(BF16), wdn_ref[0, e])
        y = _rms(tot, gn_ref[...])
        drain(1 - par)
        ybuf[...] = y

    def ids_wait(par):
        pltpu.make_async_copy(idv.at[par], ids.at[par], isem).wait()

    def row_copy(ybuf, par, r):
        return pltpu.make_async_copy(ybuf.at[pl.ds(r, 1), :], y_hbm.at[pl.ds(ids[1 - par, r], 1), :], sem.at[par])

    for par, (ycur, yprev) in enumerate(((yb0, yb1), (yb1, yb0))):
        @pl.when(((t % 2) == par) & fast)
        def _():
            ids_wait(1 - par)
            for r in range(EXPERT_TILE):
                row_copy(yprev, par, r).start(priority=r % 2)
            compute(ycur, par)

        @pl.when(((t % 2) == par) & jnp.logical_not(fast))
        def _():
            @pl.when(prev_valid > 0)
            def _():
                ids_wait(1 - par)

                def start(r, c):
                    row_copy(yprev, par, r).start()
                    return c
                lax.fori_loop(0, prev_valid, start, 0)

            @pl.when(cur_valid > 0)
            def _():
                compute(ycur, par)

            @pl.when(cur_valid <= 0)
            def _():
                drain(1 - par)

        @pl.when((t % 2) == par)
        def _():
            pend[0] = prev_valid

            @pl.when(t == pl.num_programs(0) - 1)
            def _():
                drain(par)


def _experts(cnt, xs, wgu, wdn, gn, n):
    tiles = xs.shape[0] // EXPERT_TILE
    grp = lambda t, c: (_tile_of(t, c)[1], 0, 0, 0)
    return pl.pallas_call(
        _experts_kernel,
        grid_spec=pltpu.PrefetchScalarGridSpec(
            num_scalar_prefetch=1,
            grid=(tiles + 1,),
            in_specs=[
                pl.BlockSpec((EXPERT_TILE, ROW_W), lambda t, c: (_tile_of(t, c)[0], 0)),
                pl.BlockSpec((1, EPG, D_MODEL, 2 * D_EXPERT), grp),
                pl.BlockSpec((1, EPG, D_EXPERT, D_MODEL), grp),
                pl.BlockSpec((1, D_MODEL), lambda t, c: (0, 0)),
            ],
            out_specs=pl.BlockSpec(memory_space=pl.ANY),
            scratch_shapes=[pltpu.VMEM((EXPERT_TILE, D_MODEL), F32), pltpu.VMEM((EXPERT_TILE, D_MODEL), F32),
                            pltpu.VMEM((8, EXPERT_TILE), jnp.int32), pltpu.SMEM((2, EXPERT_TILE), jnp.int32),
                            pltpu.SMEM((1,), jnp.int32), pltpu.SemaphoreType.DMA((2,)), pltpu.SemaphoreType.DMA(())],
        ),
        out_shape=jax.ShapeDtypeStruct((n, D_MODEL), F32),
        compiler_params=pltpu.CompilerParams(
            dimension_semantics=("arbitrary",), vmem_limit_bytes=VMEM_LIMIT),
        name="experts",
    )(cnt, xs, wgu, wdn, gn)


def _forget_lanes(f):
    z = lambda n: jnp.zeros(f.shape[:-1] + (n,), f.dtype)
    return jnp.concatenate([f, z(F_HI - HEADS), f, f, f, z(LANES - F_LO - HEADS)], axis=-1)


def _trunk(x, attend, state, w, tm_proj, nb_proj, tm_tok, nsub_tok):
    b, s, _ = x.shape
    k, v, lf, qa, ka, mb, ncv, *built = _proj(
        x, state, w["gm"], w["w_proj"], w["bias_f"], w["cw"], w["gc"], tm_proj, nb_proj)
    if built:
        w["w_proj"] = built[0]
    o = attend(qa, ka, v)
    n = b * s
    rows, slots, cnt = _mix(x.reshape(n, D_MODEL), o.reshape(n, ATTN_W), mb.reshape(n, CONV_W),
                            w["wo"], w["ga"], w["gf"], w["wr"], w["br"], tm_tok, nsub_tok)
    cnt = cnt[:, 0]
    y = _experts(cnt, _scatter(cnt, rows, slots, tm_tok), w["wgu"], w["wdn"], w["gn"], n)
    return (y.reshape(b, s, D_MODEL), k.reshape(1, b, s, HEADS, HEAD_DIM), v.reshape(1, b, s, HEADS, HEAD_DIM),
            lf.transpose(0, 2, 1).reshape(1, b, s, HEADS), ncv.reshape(1, b, 2, CONV_W))


def kernel(x_prompt, x_sample, cache_k, cache_v, cache_logf, state_conv, norm_mix_g, w_in, b_forget, conv_w, norm_attn_g, norm_conv_g, w_out, norm_ffn_g, w_router_group, b_router_group, w_router_expert, b_router_expert, w_expert_gate_up, w_expert_down, norm_final_g):
    assert w_in.shape[0] == 1, "single-layer trunk"
    wr = jnp.concatenate(
        [w_router_group[0], jnp.zeros((D_MODEL, EPG - N_GROUPS), F32),
         w_router_expert[0].transpose(1, 0, 2).reshape(D_MODEL, N_GROUPS * EPG),
         jnp.zeros((D_MODEL, LANES - EPG - N_GROUPS * EPG), F32)], axis=1).astype(BF16)
    br = jnp.concatenate(
        [b_router_group[0], jnp.zeros((EPG - N_GROUPS,), F32), b_router_expert[0].reshape(-1),
         jnp.zeros((LANES - EPG - N_GROUPS * EPG,), F32)]).reshape(1, LANES)
    w = dict(
        gm=norm_mix_g[0].reshape(1, D_MODEL), w_proj=w_in[0].T, bias_f=_forget_lanes(b_forget[0]).reshape(1, LANES),
        cw=jnp.concatenate([conv_w[0], jnp.zeros((8 - conv_w.shape[1], CONV_W), F32)], axis=0),
        gc=norm_conv_g[0].reshape(1, CONV_W), wo=w_out[0].astype(BF16), ga=norm_attn_g[0].reshape(1, ATTN_W),
        gf=norm_ffn_g[0].reshape(1, D_MODEL), wr=wr, br=br,
        wgu=w_expert_gate_up[0], wdn=w_expert_down[0],
        gn=norm_final_g.reshape(1, D_MODEL))

    bp, sp, _ = x_prompt.shape
    bs, ss, _ = x_sample.shape
    plen = cache_k.shape[2]

    def attend_p(qa, ka, v):
        wgu, wdn = w["wgu"], w["wdn"]
        o, wgu_bf, wdn_bf = _attention(qa, ka, v, wgu.reshape((-1,) + wgu.shape[2:]), wdn.reshape((-1,) + wdn.shape[2:]))
        w["wgu"], w["wdn"] = wgu_bf.reshape(wgu.shape), wdn_bf.reshape(wdn.shape)
        return o

    yp, kp, vp, lfp, cvp = _trunk(x_prompt, attend_p, jnp.zeros((bp, 8, CONV_W), F32), w, 512, 2, 512, 2)

    ckt = cache_k[0].transpose(0, 2, 3, 1).reshape(bs, ATTN_W, plen)
    cvt = cache_v[0].transpose(0, 2, 3, 1).reshape(bs, ATTN_W, plen)
    clft = cache_logf[0].transpose(0, 2, 1)
    st = jnp.concatenate([jnp.zeros((bs, 6, CONV_W), F32), state_conv[0]], axis=1)
    attend_s = lambda qa, ka, v: _attention_cache(qa, ka, v, ckt, cvt, clft)
    ys, ks, vs, lfs, cvs = _trunk(x_sample, attend_s, st, w, ss, 2, bs * ss, 1)
    return (yp, ys, kp, vp, lfp, cvp, ks, vs, lfs, cvs)
```

```python
import functools
import math

import jax
import jax.numpy as jnp
from jax import lax
from jax.experimental import pallas as pl
from jax.experimental.pallas import tpu as pltpu

F32 = jnp.float32
BF16 = jnp.bfloat16

D_MODEL = 1024
HEADS = 8
HEAD_DIM = 64
ATTN_W = HEADS * HEAD_DIM
CONV_W = 512
N_GROUPS = 4
EPG = 8
D_EXPERT = 256
EPS = 1e-6
LOG2E = math.log2(math.e)
Q_SCALE = HEAD_DIM ** -0.5 * LOG2E

LANES = 128
COL_Q, COL_K, COL_V, COL_F, COL_BG, COL_CG, COL_H = 0, 512, 1024, 1536, 1664, 2176, 2688
PROJ_PAD = 3200
W_SECTIONS = ((0, COL_Q), (512, COL_K), (1024, COL_V), (1544, COL_BG), (2056, COL_CG), (2568, COL_H))
W_F_ROW = 1536
F_OUT, F_HI, F_MID, F_LO = 0, 64, 72, 80

GROUP_SHIFT = 20
ROW_W = 2 * D_MODEL + LANES
META_ID = EPG


def _tile_rows(n):
    return 512 if n >= 4096 else 128

VMEM_LIMIT = 60 * 1024 * 1024


def _dot(a, b):
    return jnp.dot(a, b, preferred_element_type=F32)


def _dot_nt(a, b):
    return lax.dot_general(a, b, (((1,), (1,)), ((), ())), preferred_element_type=F32)


def _rms(x, g):
    return x * lax.rsqrt(jnp.mean(x * x, axis=-1, keepdims=True) + EPS) * g


def _log_sigmoid(x):
    return jnp.minimum(x, 0.0) - jnp.log(1.0 + jnp.exp(-jnp.abs(x)))


def _pad_rows(x, n):
    return jnp.concatenate([x, jnp.zeros((n - x.shape[0], x.shape[1]), x.dtype)], axis=0)


def _split3(x):
    a1 = x.astype(BF16)
    r1 = x - a1.astype(F32)
    a2 = r1.astype(BF16)
    a3 = (r1 - a2.astype(F32)).astype(BF16)
    return a1, a2, a3


def _cumsum_rows(x):
    n = x.shape[0]
    r = lax.broadcasted_iota(jnp.int32, (n, n), 0)
    c = lax.broadcasted_iota(jnp.int32, (n, n), 1)
    tri = jnp.where(r >= c, 1.0, 0.0).astype(BF16)
    parts = _dot(tri, jnp.concatenate(_split3(x), axis=1))
    return parts[:, :LANES] + parts[:, LANES:2 * LANES] + parts[:, 2 * LANES:]


def _key_bias_lanes(cneg):
    lane = lax.broadcasted_iota(jnp.int32, cneg.shape, 1)
    hi = cneg.astype(BF16).astype(F32)
    r1 = cneg - hi
    mid = r1.astype(BF16).astype(F32)
    lo = r1 - mid
    out = jnp.where(lane < F_MID, hi, jnp.where(lane < F_LO, mid, lo))
    return jnp.where((lane >= F_HI) & (lane < F_LO + HEADS), out, 0.0)


def _head_block(x, h):
    blk = x[:, (h // 2) * LANES:(h // 2 + 1) * LANES]
    return pltpu.roll(blk, HEAD_DIM, 1) if h % 2 else blk


def _key_operand(k, bias_lanes, h):
    lane = lax.broadcasted_iota(jnp.int32, bias_lanes.shape, 1)
    return jnp.where(lane < HEAD_DIM, _head_block(k, h), bias_lanes).astype(BF16)


def _prepare_weight(wt_hbm, w_ref, stage, sem):
    for r0, c0 in W_SECTIONS:
        cp = pltpu.make_async_copy(wt_hbm.at[pl.ds(r0, ATTN_W), :], stage, sem)
        cp.start()
        cp.wait()
        w_ref[:, c0:c0 + ATTN_W] = stage[...].T.astype(BF16)
    cp = pltpu.make_async_copy(wt_hbm.at[pl.ds(W_F_ROW, HEADS), :], stage.at[pl.ds(0, HEADS), :], sem)
    cp.start()
    cp.wait()
    f = stage[0:HEADS, :]
    z = lambda n: jnp.zeros((n, D_MODEL), F32)
    fblk = jnp.concatenate([f, z(F_HI - HEADS), f, f, f, z(LANES - F_LO - HEADS)], axis=0)
    w_ref[:, COL_F:COL_F + LANES] = fblk.T.astype(BF16)


def _proj_kernel(prepare, x_ref, st_ref, gm_ref, w_in_ref, bf_ref, cw_ref, gc_ref,
                 k_ref, v_ref, lf_ref, qa_ref, ka_ref, mb_ref, ncv_ref, *rest):
    i = pl.program_id(1)
    nb, tm, _ = x_ref.shape
    if prepare:
        w_ref, ubuf, ccar, stage, sem = rest

        @pl.when((pl.program_id(0) == 0) & (i == 0))
        def _():
            _prepare_weight(w_in_ref, w_ref, stage, sem)
    else:
        w_ref = w_in_ref
        ubuf, ccar = rest

    @pl.when(i == 0)
    def _():
        ccar[...] = jnp.zeros_like(ccar)
        ubuf[:, 0:8, :] = st_ref[...]

    @pl.when(i > 0)
    def _():
        ubuf[:, 0:8, :] = ubuf[:, tm:tm + 8, :]

    lane = lax.broadcasted_iota(jnp.int32, (tm, LANES), 1)
    pad = (-tm) % LANES
    for bb in range(nb):
        xn = _rms(x_ref[bb], gm_ref[...]).astype(BF16)
        proj = lambda c0, n: _dot(xn, w_ref[:, c0:c0 + n])

        lfb = _log_sigmoid(proj(COL_F, LANES) + bf_ref[...])
        lf_ref[bb] = (_pad_rows(lfb, tm + pad) if pad else lfb).T[F_OUT:F_OUT + HEADS, 0:tm]
        c = _cumsum_rows(lfb) + ccar[bb, 0:1, :]
        ccar[bb] = jnp.broadcast_to(c[tm - 1:tm, :], ccar.shape[1:])
        bias_lanes = _key_bias_lanes(c * (-LOG2E))
        k = proj(COL_K, ATTN_W)
        k_ref[bb] = k
        for h in range(HEADS):
            ka_ref[bb, h] = _key_operand(k, bias_lanes, h)
        q = proj(COL_Q, ATTN_W)
        for h in range(HEADS):
            sel = jnp.where((lane == F_HI + h) | (lane == F_MID + h) | (lane == F_LO + h), 1.0, 0.0)
            qa_ref[bb, h] = jnp.where(lane < HEAD_DIM, _head_block(q, h) * Q_SCALE, sel).astype(BF16)
        v_ref[bb] = proj(COL_V, ATTN_W)

        u = proj(COL_CG, CONV_W) * proj(COL_H, CONV_W)
        ubuf[bb, 8:8 + tm, :] = u
        y = (cw_ref[0:1, :] * ubuf[bb, 6:6 + tm, :] + cw_ref[1:2, :] * ubuf[bb, 7:7 + tm, :]
             + cw_ref[2:3, :] * u)
        ob = proj(COL_BG, CONV_W) * y
        mb_ref[bb] = _rms(ob, gc_ref[...]).astype(BF16)
        ncv_ref[bb] = u[tm - 2:tm, :]


def _proj(x, state, gm, w, bias_f, cw, gc, tm, nb):
    b, s, _ = x.shape
    prepare = w.dtype == F32
    full = lambda shape: pl.BlockSpec(shape, lambda bi, i: (0,) * len(shape))
    out_specs = [
        pl.BlockSpec((nb, tm, ATTN_W), lambda bi, i: (bi, i, 0)),
        pl.BlockSpec((nb, tm, ATTN_W), lambda bi, i: (bi, i, 0)),
        pl.BlockSpec((nb, HEADS, tm), lambda bi, i: (bi, 0, i)),
        pl.BlockSpec((nb, HEADS, tm, LANES), lambda bi, i: (bi, 0, i, 0)),
        pl.BlockSpec((nb, HEADS, tm, LANES), lambda bi, i: (bi, 0, i, 0)),
        pl.BlockSpec((nb, tm, CONV_W), lambda bi, i: (bi, i, 0)),
        pl.BlockSpec((nb, 2, CONV_W), lambda bi, i: (bi, 0, 0)),
    ]
    out_shape = [
        jax.ShapeDtypeStruct((b, s, ATTN_W), F32),
        jax.ShapeDtypeStruct((b, s, ATTN_W), F32),
        jax.ShapeDtypeStruct((b, HEADS, s), F32),
        jax.ShapeDtypeStruct((b, HEADS, s, LANES), BF16),
        jax.ShapeDtypeStruct((b, HEADS, s, LANES), BF16),
        jax.ShapeDtypeStruct((b, s, CONV_W), BF16),
        jax.ShapeDtypeStruct((b, 2, CONV_W), F32),
    ]
    scratch = [pltpu.VMEM((nb, tm + 8, CONV_W), F32), pltpu.VMEM((nb, 8, LANES), F32)]
    if prepare:
        out_specs.append(full((D_MODEL, PROJ_PAD)))
        out_shape.append(jax.ShapeDtypeStruct((D_MODEL, PROJ_PAD), BF16))
        scratch += [pltpu.VMEM((ATTN_W, D_MODEL), F32), pltpu.SemaphoreType.DMA(())]
    return pl.pallas_call(
        functools.partial(_proj_kernel, prepare),
        grid=(b // nb, s // tm),
        in_specs=[
            pl.BlockSpec((nb, tm, D_MODEL), lambda bi, i: (bi, i, 0)),
            pl.BlockSpec((nb, 8, CONV_W), lambda bi, i: (bi, 0, 0)),
            full((1, D_MODEL)),
            pl.BlockSpec(memory_space=pl.ANY) if prepare else full((D_MODEL, PROJ_PAD)),
            full((1, LANES)),
            full((8, CONV_W)),
            full((1, CONV_W)),
        ],
        out_specs=out_specs,
        out_shape=out_shape,
        scratch_shapes=scratch,
        compiler_params=pltpu.CompilerParams(
            dimension_semantics=("arbitrary", "arbitrary"), vmem_limit_bytes=VMEM_LIMIT),
        name="proj",
    )(x, state, gm, w, bias_f, cw, gc)


ATT_BLK = 256


def _attn_kernel(qa_ref, ka_ref, v_ref, wgu_ref, wdn_ref, o_ref, wgu_bf_ref, wdn_bf_ref, sbuf):
    wgu_bf_ref[...] = wgu_ref[...].astype(BF16)
    wdn_bf_ref[...] = wdn_ref[...].astype(BF16)
    blk = ATT_BLK
    vt = v_ref[0].T.astype(BF16)
    r = lax.broadcasted_iota(jnp.int32, (blk, blk), 0)
    c = lax.broadcasted_iota(jnp.int32, (blk, blk), 1)
    causal = r <= c
    units = [(iq, hh) for iq in range(ka_ref.shape[2] // blk) for hh in range(2)]

    def scores(u):
        iq, hh = units[u]
        n = (iq + 1) * blk
        sbuf[u % nbuf, 0:n, :] = _dot_nt(ka_ref[0, hh, 0:n, :], qa_ref[0, hh, iq * blk:n, :])

    nbuf = sbuf.shape[0]
    for u in range(nbuf - 1):
        scores(u)
    outs = []
    for u, (iq, hh) in enumerate(units):
        if u + nbuf - 1 < len(units):
            scores(u + nbuf - 1)
        noff = iq * blk
        rows = slice(hh * HEAD_DIM, (hh + 1) * HEAD_DIM)
        s_d = jnp.where(causal, sbuf[u % nbuf, noff:noff + blk, :], -jnp.inf)
        m = jnp.max(s_d, axis=0, keepdims=True)
        if iq > 0:
            m = jnp.maximum(m, jnp.max(sbuf[u % nbuf, 0:noff, :], axis=0, keepdims=True))
        p_d = jnp.exp2(s_d - m)
        l = jnp.sum(p_d, axis=0, keepdims=True)
        acc = _dot(vt[rows, noff:noff + blk], p_d.astype(BF16))
        if iq > 0:
            p_o = jnp.exp2(sbuf[u % nbuf, 0:noff, :] - m)
            l = l + jnp.sum(p_o, axis=0, keepdims=True)
            acc = acc + _dot(vt[rows, 0:noff], p_o.astype(BF16))
        outs.append(acc / l)
        if hh == 1:
            o_ref[0, noff:noff + blk, :] = jnp.concatenate(outs, axis=0).T
            outs = []


def _attention(qa, ka, v, wgu, wdn):
    b, _, s, _ = qa.shape
    pairs = HEADS // 2
    ne = wgu.shape[0] // (b * pairs)
    assert ne * b * pairs == wgu.shape[0] == wdn.shape[0]
    wspec = lambda w: pl.BlockSpec((ne,) + w.shape[1:], lambda bi, p: (bi * pairs + p, 0, 0))
    return pl.pallas_call(
        _attn_kernel,
        grid=(b, pairs),
        in_specs=[
            pl.BlockSpec((1, 2, s, LANES), lambda bi, p: (bi, p, 0, 0)),
            pl.BlockSpec((1, 2, s, LANES), lambda bi, p: (bi, p, 0, 0)),
            pl.BlockSpec((1, s, LANES), lambda bi, p: (bi, 0, p)),
            wspec(wgu),
            wspec(wdn),
        ],
        out_specs=[pl.BlockSpec((1, s, LANES), lambda bi, p: (bi, 0, p)), wspec(wgu), wspec(wdn)],
        out_shape=[jax.ShapeDtypeStruct((b, s, ATTN_W), F32),
                   jax.ShapeDtypeStruct(wgu.shape, BF16), jax.ShapeDtypeStruct(wdn.shape, BF16)],
        scratch_shapes=[pltpu.VMEM((3, s, ATT_BLK), F32)],
        compiler_params=pltpu.CompilerParams(
            dimension_semantics=("arbitrary", "arbitrary"), vmem_limit_bytes=VMEM_LIMIT),
        name="attention",
    )(qa, ka, v, wgu, wdn)


Q_PAD = 128


def _cumsum_lanes(x):
    n = x.shape[1]
    r = lax.broadcasted_iota(jnp.int32, (n, n), 0)
    c = lax.broadcasted_iota(jnp.int32, (n, n), 1)
    tri = jnp.where(r <= c, 1.0, 0.0).astype(BF16)
    terms = jnp.concatenate([a.astype(F32) for a in _split3(x)], axis=0).astype(BF16)
    parts = _dot(terms, tri)
    rows = x.shape[0]
    return parts[0:rows] + parts[rows:2 * rows] + parts[2 * rows:]


def _attn_cache_kernel(qa_ref, ka_ref, v_ref, ckt_ref, cvt_ref, clft_ref, o_ref):
    t = qa_ref.shape[2]
    plen = ckt_ref.shape[2]
    cinc = _cumsum_lanes(clft_ref[0])
    bias = (cinc - cinc[:, plen - 1:plen]) * (-LOG2E)
    lane = lax.broadcasted_iota(jnp.int32, (t, LANES), 1)
    rq = lax.broadcasted_iota(jnp.int32, (t, Q_PAD), 0)
    ck = lax.broadcasted_iota(jnp.int32, (t, Q_PAD), 1)
    new_ok = ck <= rq

    pairs = []
    for p in range(HEADS // 2):
        rows = slice(p * LANES, (p + 1) * LANES)
        kt = ckt_ref[0, rows, :].astype(BF16)
        vt = cvt_ref[0, rows, :].astype(BF16)
        vn = _pad_rows(v_ref[0][:, rows], Q_PAD).astype(BF16)
        res = []
        for hh in range(2):
            h = 2 * p + hh
            qa = qa_ref[0, h]
            qh = jnp.where(lane < HEAD_DIM, qa.astype(F32), 0.0)
            qh = (pltpu.roll(qh, HEAD_DIM, 1) if hh else qh).astype(BF16)
            s_past = _dot(qh, kt) + bias[h:h + 1, :]
            s_new = jnp.where(new_ok, _dot_nt(qa, _pad_rows(ka_ref[0, h], Q_PAD)), -jnp.inf)
            m = jnp.maximum(jnp.max(s_past, axis=1, keepdims=True), jnp.max(s_new, axis=1, keepdims=True))
            p_past = jnp.exp2(s_past - m)
            p_new = jnp.exp2(s_new - m)
            l = jnp.sum(p_past, axis=1, keepdims=True) + jnp.sum(p_new, axis=1, keepdims=True)
            acc = _dot_nt(p_past.astype(BF16), vt) + _dot(p_new.astype(BF16), vn)
            res.append(acc / l)
        pairs.append(jnp.where(lane < HEAD_DIM, res[0], res[1]))
    o_ref[0] = jnp.concatenate(pairs, axis=1)


def _attention_cache(qa, ka, v, ckt, cvt, clft):
    b, _, t, _ = qa.shape
    plen = ckt.shape[2]
    return pl.pallas_call(
        _attn_cache_kernel,
        grid=(b,),
        in_specs=[
            pl.BlockSpec((1, HEADS, t, LANES), lambda bi: (bi, 0, 0, 0)),
            pl.BlockSpec((1, HEADS, t, LANES), lambda bi: (bi, 0, 0, 0)),
            pl.BlockSpec((1, t, ATTN_W), lambda bi: (bi, 0, 0)),
            pl.BlockSpec((1, ATTN_W, plen), lambda bi: (bi, 0, 0)),
            pl.BlockSpec((1, ATTN_W, plen), lambda bi: (bi, 0, 0)),
            pl.BlockSpec((1, HEADS, plen), lambda bi: (bi, 0, 0)),
        ],
        out_specs=pl.BlockSpec((1, t, ATTN_W), lambda bi: (bi, 0, 0)),
        out_shape=jax.ShapeDtypeStruct((b, t, ATTN_W), F32),
        compiler_params=pltpu.CompilerParams(
            dimension_semantics=("arbitrary",), vmem_limit_bytes=VMEM_LIMIT),
        name="attention_cache",
    )(qa, ka, v, ckt, cvt, clft)


def _mix_kernel(et, x_ref, o_ref, mb_ref, wo_ref, ga_ref, gf_ref, wr_ref, br_ref,
                rows_ref, slot_ref, cnt_ref, carry):
    i = pl.program_id(0)
    tm = slot_ref.shape[2]
    nsub = x_ref.shape[0] // tm
    row = lax.broadcasted_iota(jnp.int32, (EPG, tm), 0)
    row2 = lax.broadcasted_iota(jnp.int32, (2 * EPG, tm), 0)
    tok = lax.broadcasted_iota(jnp.int32, (2 * EPG, tm), 1)
    r2 = lax.broadcasted_iota(jnp.int32, (tm, tm), 0)
    c2 = lax.broadcasted_iota(jnp.int32, (tm, tm), 1)
    earlier = jnp.where(r2 < c2, 1.0, 0.0).astype(BF16)

    @pl.when(i == 0)
    def _():
        carry[...] = jnp.zeros_like(carry)

    total = carry[:, 0:1]
    for j in range(nsub):
        rs = slice(j * tm, (j + 1) * tm)
        oa = _rms(o_ref[rs, :], ga_ref[...]).astype(BF16)
        merged = jnp.concatenate([oa, mb_ref[rs, :]], axis=1)
        x2 = x_ref[rs, :] + _dot(merged, wo_ref[...])
        rows_ref[rs, D_MODEL:2 * D_MODEL] = x2
        xb = _rms(x2, gf_ref[...]).astype(BF16)
        rows_ref[rs, 0:D_MODEL] = xb.astype(F32)

        lt = (_dot(xb, wr_ref[...]) + br_ref[...]).T
        lg = jnp.where(row < N_GROUPS, lt[0:EPG], -jnp.inf)
        eg = jnp.exp(lg - jnp.max(lg, axis=0, keepdims=True))
        pg = eg / jnp.sum(eg, axis=0, keepdims=True)
        pg_top = jnp.max(pg, axis=0, keepdims=True)
        g_top = jnp.min(jnp.where(pg == pg_top, row, EPG), axis=0, keepdims=True)

        le = lt[EPG:2 * EPG]
        for g in range(1, N_GROUPS):
            le = jnp.where(g_top == g, lt[EPG * (g + 1):EPG * (g + 2)], le)
        ee = jnp.exp(le - jnp.max(le, axis=0, keepdims=True))
        pe = ee / jnp.sum(ee, axis=0, keepdims=True)
        v1 = jnp.max(pe, axis=0, keepdims=True)
        i1 = jnp.min(jnp.where(pe == v1, row, EPG), axis=0, keepdims=True)
        pe2 = jnp.where(row == i1, -1.0, pe)
        v2 = jnp.max(pe2, axis=0, keepdims=True)
        i2 = jnp.min(jnp.where(pe2 == v2, row, EPG), axis=0, keepdims=True)
        den = v1 + v2
        wg = jnp.where(row == i1, v1 / den * pg_top, 0.0) + jnp.where(row == i2, v2 / den * pg_top, 0.0)
        meta = jnp.where(row2 == META_ID, ((i * nsub + j) * tm + tok).astype(F32),
                         jnp.concatenate([wg, jnp.zeros((EPG, tm), F32)], axis=0))
        rows_ref[rs, 2 * D_MODEL:] = jnp.concatenate([meta, jnp.zeros((LANES - 2 * EPG, tm), F32)], axis=0).T

        onehot = jnp.where(row == g_top, 1.0, 0.0)
        rank = jnp.sum(onehot * (_dot(onehot.astype(BF16), earlier) + total), axis=0, keepdims=True)
        slot_ref[i * nsub + j] = g_top * (1 << GROUP_SHIFT) + rank.astype(jnp.int32)
        total = total + jnp.sum(onehot, axis=1, keepdims=True)

    carry[...] = jnp.broadcast_to(total, carry.shape)
    cnt_ref[...] = carry[...].astype(jnp.int32)

    @pl.when(i == pl.num_programs(0) - 1)
    def _():
        padded = jnp.floor((total + (et - 1)) * (1.0 / et)) * et
        o1 = padded[0:1]
        o2 = o1 + padded[1:2]
        o3 = o2 + padded[2:3]
        code = slot_ref[...]
        g = code >> GROUP_SHIFT
        first = jnp.where(g == 0, 0.0, jnp.where(g == 1, o1, jnp.where(g == 2, o2, o3)))
        slot_ref[...] = first.astype(jnp.int32) + (code & ((1 << GROUP_SHIFT) - 1))


def _mix(x, o, mb, wo, ga, gf, wr, br, tm, nsub):
    n = x.shape[0]
    assert n < (1 << GROUP_SHIFT)
    tb = tm * nsub
    full = lambda shape: pl.BlockSpec(shape, lambda i: (0,) * len(shape))
    return pl.pallas_call(
        functools.partial(_mix_kernel, _tile_rows(n)),
        grid=(n // tb,),
        in_specs=[
            pl.BlockSpec((tb, D_MODEL), lambda i: (i, 0)),
            pl.BlockSpec((tb, ATTN_W), lambda i: (i, 0)),
            pl.BlockSpec((tb, CONV_W), lambda i: (i, 0)),
            full((D_MODEL, D_MODEL)),
            full((1, ATTN_W)),
            full((1, D_MODEL)),
            full((D_MODEL, LANES)),
            full((1, LANES)),
        ],
        out_specs=[
            pl.BlockSpec((tb, ROW_W), lambda i: (i, 0)),
            full((n // tm, 1, tm)),
            full((EPG, LANES)),
        ],
        out_shape=[
            jax.ShapeDtypeStruct((n, ROW_W), F32),
            jax.ShapeDtypeStruct((n // tm, 1, tm), jnp.int32),
            jax.ShapeDtypeStruct((EPG, LANES), jnp.int32),
        ],
        scratch_shapes=[pltpu.VMEM((EPG, LANES), F32)],
        compiler_params=pltpu.CompilerParams(
            dimension_semantics=("arbitrary",), vmem_limit_bytes=VMEM_LIMIT),
        name="mix",
    )(x, o, mb, wo, ga, gf, wr, br)


def _sorted_tiles(n):
    return n // _tile_rows(n) + N_GROUPS


def _group_tiles(cnt_ref, et):
    nt = [(cnt_ref[g] + (et - 1)) >> (et.bit_length() - 1) for g in range(N_GROUPS)]
    s1 = nt[0]
    s2 = s1 + nt[1]
    s3 = s2 + nt[2]
    return s1, s2, s3, s3 + nt[3]


def _scatter_kernel(cnt_ref, rows_ref, slot_ref, xs_ref, zeros, sem, zsem):
    tm = rows_ref.shape[0]
    et = zeros.shape[0]

    @pl.when(pl.program_id(0) == 0)
    def _():
        zeros[...] = jnp.zeros_like(zeros)
        s1, s2, s3, tot = _group_tiles(cnt_ref, et)
        ends = (s1, s2, s3, tot)

        def fill(tile):
            return pltpu.make_async_copy(zeros, xs_ref.at[pl.ds(tile * et, et), :], zsem)

        for g in range(N_GROUPS):
            partial = (cnt_ref[g] & (et - 1)) != 0

            @pl.when(partial)
            def _():
                fill(ends[g] - 1).start()
                fill(ends[g] - 1).wait()

        def fill_tail(tile, c):
            fill(tile).start()
            fill(tile).wait()
            return c

        lax.fori_loop(tot, xs_ref.shape[0] // et, fill_tail, 0)

    copy_of_row = lambda r: pltpu.make_async_copy(
        rows_ref.at[pl.ds(r, 1), :], xs_ref.at[pl.ds(slot_ref[0, 0, r], 1), :], sem)
    for r in range(tm):
        copy_of_row(r).start(priority=r % 2)
    for r in range(tm):
        copy_of_row(r).wait()


def _scatter(cnt, rows, slots, tm):
    n = rows.shape[0]
    et = _tile_rows(n)
    return pl.pallas_call(
        _scatter_kernel,
        grid_spec=pltpu.PrefetchScalarGridSpec(
            num_scalar_prefetch=1,
            grid=(n // tm,),
            in_specs=[
                pl.BlockSpec((tm, ROW_W), lambda i, c: (i, 0)),
                pl.BlockSpec((1, 1, tm), lambda i, c: (i, 0, 0), memory_space=pltpu.SMEM),
            ],
            out_specs=pl.BlockSpec(memory_space=pl.ANY),
            scratch_shapes=[pltpu.VMEM((et, ROW_W), F32),
                            pltpu.SemaphoreType.DMA(()), pltpu.SemaphoreType.DMA(())],
        ),
        out_shape=jax.ShapeDtypeStruct((_sorted_tiles(n) * et, ROW_W), F32),
        compiler_params=pltpu.CompilerParams(
            dimension_semantics=("arbitrary",), vmem_limit_bytes=VMEM_LIMIT),
        name="scatter",
    )(cnt, rows, slots)


def _tile_of(t, cnt_ref, et):
    s1, s2, s3, tot = _group_tiles(cnt_ref, et)
    tc = jnp.minimum(t, tot - 1)
    g = (tc >= s1).astype(jnp.int32) + (tc >= s2).astype(jnp.int32) + (tc >= s3).astype(jnp.int32)
    first = jnp.where(g == 0, 0, jnp.where(g == 1, s1, jnp.where(g == 2, s2, s3)))
    cnt = jnp.where(g == 0, cnt_ref[0], jnp.where(g == 1, cnt_ref[1], jnp.where(g == 2, cnt_ref[2], cnt_ref[3])))
    return tc, g, jnp.where(t < tot, cnt - (tc - first) * et, 0)


def _experts_kernel(cnt_ref, xs_ref, wgu_ref, wdn_ref, gn_ref, y_hbm, yb0, yb1, idv, ids, pend, sem, isem):
    t = pl.program_id(0)
    et = xs_ref.shape[0]
    _, _, cur_valid = _tile_of(t, cnt_ref, et)
    prev_valid = jnp.where(t > 0, jnp.clip(_tile_of(t - 1, cnt_ref, et)[2], 0, et), 0)
    fast = (prev_valid == et) & (cur_valid > 0)

    @pl.when(t == 0)
    def _():
        pend[0] = 0

    def drain(k):
        one = pltpu.make_async_copy(yb0.at[pl.ds(0, 1), :], y_hbm.at[pl.ds(0, 1), :], sem.at[k])

        @pl.when(pend[0] == et)
        def _():
            for _ in range(et):
                one.wait()

        @pl.when(pend[0] != et)
        def _():
            def body(r, c):
                one.wait()
                return c
            lax.fori_loop(0, pend[0], body, 0)

    def compute(ybuf, par):
        idv[par:par + 1, :] = xs_ref[:, 2 * D_MODEL:].T[META_ID:META_ID + 1, :].astype(jnp.int32)
        pltpu.make_async_copy(idv.at[par], ids.at[par], isem).start()
        xb = xs_ref[:, 0:D_MODEL].astype(BF16)
        wts = xs_ref[:, 2 * D_MODEL:]
        tot = xs_ref[:, D_MODEL:2 * D_MODEL]
        for e in range(EPG):
            gu = _dot(xb, wgu_ref[0, e])
            gate = gu[:, :D_EXPERT]
            hmid = gate * jax.nn.sigmoid(gate) * gu[:, D_EXPERT:] * wts[:, e:e + 1]
            tot = tot + _dot(hmid.astype(BF16), wdn_ref[0, e])
        y = _rms(tot, gn_ref[...])
        drain(1 - par)
        ybuf[...] = y

    def ids_wait(par):
        pltpu.make_async_copy(idv.at[par], ids.at[par], isem).wait()

    def row_copy(ybuf, par, r):
        return pltpu.make_async_copy(ybuf.at[pl.ds(r, 1), :], y_hbm.at[pl.ds(ids[1 - par, r], 1), :], sem.at[par])

    for par, (ycur, yprev) in enumerate(((yb0, yb1), (yb1, yb0))):
        @pl.when(((t % 2) == par) & fast)
        def _():
            ids_wait(1 - par)
            for r in range(et):
                row_copy(yprev, par, r).start(priority=r % 2)
            compute(ycur, par)

        @pl.when(((t % 2) == par) & jnp.logical_not(fast))
        def _():
            @pl.when(prev_valid > 0)
            def _():
                ids_wait(1 - par)

                def start(r, c):
                    row_copy(yprev, par, r).start()
                    return c
                lax.fori_loop(0, prev_valid, start, 0)

            @pl.when(cur_valid > 0)
            def _():
                compute(ycur, par)

            @pl.when(cur_valid <= 0)
            def _():
                drain(1 - par)

        @pl.when((t % 2) == par)
        def _():
            pend[0] = prev_valid

            @pl.when(t == pl.num_programs(0) - 1)
            def _():
                drain(par)


def _experts(cnt, xs, wgu, wdn, gn, n):
    et = _tile_rows(n)
    tiles = xs.shape[0] // et
    grp = lambda t, c: (_tile_of(t, c, et)[1], 0, 0, 0)
    return pl.pallas_call(
        _experts_kernel,
        grid_spec=pltpu.PrefetchScalarGridSpec(
            num_scalar_prefetch=1,
            grid=(tiles + 1,),
            in_specs=[
                pl.BlockSpec((et, ROW_W), lambda t, c: (_tile_of(t, c, et)[0], 0)),
                pl.BlockSpec((1, EPG, D_MODEL, 2 * D_EXPERT), grp),
                pl.BlockSpec((1, EPG, D_EXPERT, D_MODEL), grp),
                pl.BlockSpec((1, D_MODEL), lambda t, c: (0, 0)),
            ],
            out_specs=pl.BlockSpec(memory_space=pl.ANY),
            scratch_shapes=[pltpu.VMEM((et, D_MODEL), F32), pltpu.VMEM((et, D_MODEL), F32),
                            pltpu.VMEM((8, et), jnp.int32), pltpu.SMEM((2, et), jnp.int32),
                            pltpu.SMEM((1,), jnp.int32), pltpu.SemaphoreType.DMA((2,)), pltpu.SemaphoreType.DMA(())],
        ),
        out_shape=jax.ShapeDtypeStruct((n, D_MODEL), F32),
        compiler_params=pltpu.CompilerParams(
            dimension_semantics=("arbitrary",), vmem_limit_bytes=VMEM_LIMIT),
        name="experts",
    )(cnt, xs, wgu, wdn, gn)


def _forget_lanes(f):
    z = lambda n: jnp.zeros(f.shape[:-1] + (n,), f.dtype)
    return jnp.concatenate([f, z(F_HI - HEADS), f, f, f, z(LANES - F_LO - HEADS)], axis=-1)


def _trunk(x, attend, state, w, tm_proj, nb_proj, tm_tok, nsub_tok):
    b, s, _ = x.shape
    k, v, lf, qa, ka, mb, ncv, *built = _proj(
        x, state, w["gm"], w["w_proj"], w["bias_f"], w["cw"], w["gc"], tm_proj, nb_proj)
    if built:
        w["w_proj"] = built[0]
    o = attend(qa, ka, v)
    n = b * s
    rows, slots, cnt = _mix(x.reshape(n, D_MODEL), o.reshape(n, ATTN_W), mb.reshape(n, CONV_W),
                            w["wo"], w["ga"], w["gf"], w["wr"], w["br"], tm_tok, nsub_tok)
    cnt = cnt[:, 0]
    y = _experts(cnt, _scatter(cnt, rows, slots, tm_tok), w["wgu"], w["wdn"], w["gn"], n)
    return (y.reshape(b, s, D_MODEL), k.reshape(1, b, s, HEADS, HEAD_DIM), v.reshape(1, b, s, HEADS, HEAD_DIM),
            lf.transpose(0, 2, 1).reshape(1, b, s, HEADS), ncv.reshape(1, b, 2, CONV_W))


def kernel(x_prompt, x_sample, cache_k, cache_v, cache_logf, state_conv, norm_mix_g, w_in, b_forget, conv_w, norm_attn_g, norm_conv_g, w_out, norm_ffn_g, w_router_group, b_router_group, w_router_expert, b_router_expert, w_expert_gate_up, w_expert_down, norm_final_g):
    assert w_in.shape[0] == 1, "single-layer trunk"
    wr = jnp.concatenate(
        [w_router_group[0], jnp.zeros((D_MODEL, EPG - N_GROUPS), F32),
         w_router_expert[0].transpose(1, 0, 2).reshape(D_MODEL, N_GROUPS * EPG),
         jnp.zeros((D_MODEL, LANES - EPG - N_GROUPS * EPG), F32)], axis=1).astype(BF16)
    br = jnp.concatenate(
        [b_router_group[0], jnp.zeros((EPG - N_GROUPS,), F32), b_router_expert[0].reshape(-1),
         jnp.zeros((LANES - EPG - N_GROUPS * EPG,), F32)]).reshape(1, LANES)
    w = dict(
        gm=norm_mix_g[0].reshape(1, D_MODEL), w_proj=w_in[0].T, bias_f=_forget_lanes(b_forget[0]).reshape(1, LANES),
        cw=jnp.concatenate([conv_w[0], jnp.zeros((8 - conv_w.shape[1], CONV_W), F32)], axis=0),
        gc=norm_conv_g[0].reshape(1, CONV_W), wo=w_out[0].astype(BF16), ga=norm_attn_g[0].reshape(1, ATTN_W),
        gf=norm_ffn_g[0].reshape(1, D_MODEL), wr=wr, br=br,
        wgu=w_expert_gate_up[0], wdn=w_expert_down[0],
        gn=norm_final_g.reshape(1, D_MODEL))

    bp, sp, _ = x_prompt.shape
    bs, ss, _ = x_sample.shape
    plen = cache_k.shape[2]

    def attend_p(qa, ka, v):
        wgu, wdn = w["wgu"], w["wdn"]
        o, wgu_bf, wdn_bf = _attention(qa, ka, v, wgu.reshape((-1,) + wgu.shape[2:]), wdn.reshape((-1,) + wdn.shape[2:]))
        w["wgu"], w["wdn"] = wgu_bf.reshape(wgu.shape), wdn_bf.reshape(wdn.shape)
        return o

    yp, kp, vp, lfp, cvp = _trunk(x_prompt, attend_p, jnp.zeros((bp, 8, CONV_W), F32), w, 512, 2, 512, 2)

    ckt = cache_k[0].transpose(0, 2, 3, 1).reshape(bs, ATTN_W, plen)
    cvt = cache_v[0].transpose(0, 2, 3, 1).reshape(bs, ATTN_W, plen)
    clft = cache_logf[0].transpose(0, 2, 1)
    st = jnp.concatenate([jnp.zeros((bs, 6, CONV_W), F32), state_conv[0]], axis=1)
    attend_s = lambda qa, ka, v: _attention_cache(qa, ka, v, ckt, cvt, clft)
    ys, ks, vs, lfs, cvs = _trunk(x_sample, attend_s, st, w, ss, bs, bs * ss, 1)
    return (yp, ys, kp, vp, lfp, cvp, ks, vs, lfs, cvs)
```

```python
import functools
import math

import jax
import jax.numpy as jnp
from jax import lax
from jax.experimental import pallas as pl
from jax.experimental.pallas import tpu as pltpu

F32 = jnp.float32
BF16 = jnp.bfloat16

D_MODEL = 1024
HEADS = 8
HEAD_DIM = 64
ATTN_W = HEADS * HEAD_DIM
CONV_W = 512
N_GROUPS = 4
EPG = 8
D_EXPERT = 256
EPS = 1e-6
LOG2E = math.log2(math.e)
Q_SCALE = HEAD_DIM ** -0.5 * LOG2E

LANES = 128
COL_Q, COL_K, COL_V, COL_F, COL_BG, COL_CG, COL_H = 0, 512, 1024, 1536, 1664, 2176, 2688
PROJ_PAD = 3200
W_SECTIONS = ((0, COL_Q), (512, COL_K), (1024, COL_V), (1544, COL_BG), (2056, COL_CG), (2568, COL_H))
W_F_ROW = 1536
F_OUT, F_HI, F_MID, F_LO = 0, 64, 72, 80

GROUP_SHIFT = 20
ROW_W = D_MODEL + LANES
META_ID = EPG


def _tile_rows(n):
    return 512 if n >= 4096 else 128

VMEM_LIMIT = 60 * 1024 * 1024


def _dot(a, b):
    return jnp.dot(a, b, preferred_element_type=F32)


def _dot_nt(a, b):
    return lax.dot_general(a, b, (((1,), (1,)), ((), ())), preferred_element_type=F32)


def _rms(x, g):
    return x * lax.rsqrt(jnp.mean(x * x, axis=-1, keepdims=True) + EPS) * g


def _log_sigmoid(x):
    return jnp.minimum(x, 0.0) - jnp.log(1.0 + jnp.exp(-jnp.abs(x)))


def _pad_rows(x, n):
    return jnp.concatenate([x, jnp.zeros((n - x.shape[0], x.shape[1]), x.dtype)], axis=0)


def _split3(x):
    a1 = x.astype(BF16)
    r1 = x - a1.astype(F32)
    a2 = r1.astype(BF16)
    a3 = (r1 - a2.astype(F32)).astype(BF16)
    return a1, a2, a3


def _cumsum_rows(x):
    n = x.shape[0]
    r = lax.broadcasted_iota(jnp.int32, (n, n), 0)
    c = lax.broadcasted_iota(jnp.int32, (n, n), 1)
    tri = jnp.where(r >= c, 1.0, 0.0).astype(BF16)
    parts = _dot(tri, jnp.concatenate(_split3(x), axis=1))
    return parts[:, :LANES] + parts[:, LANES:2 * LANES] + parts[:, 2 * LANES:]


def _key_bias_lanes(cneg):
    lane = lax.broadcasted_iota(jnp.int32, cneg.shape, 1)
    hi = cneg.astype(BF16).astype(F32)
    r1 = cneg - hi
    mid = r1.astype(BF16).astype(F32)
    lo = r1 - mid
    out = jnp.where(lane < F_MID, hi, jnp.where(lane < F_LO, mid, lo))
    return jnp.where((lane >= F_HI) & (lane < F_LO + HEADS), out, 0.0)


def _head_block(x, h):
    blk = x[:, (h // 2) * LANES:(h // 2 + 1) * LANES]
    return pltpu.roll(blk, HEAD_DIM, 1) if h % 2 else blk


def _key_operand(k, bias_lanes, h):
    lane = lax.broadcasted_iota(jnp.int32, bias_lanes.shape, 1)
    return jnp.where(lane < HEAD_DIM, _head_block(k, h), bias_lanes).astype(BF16)


def _prepare_weight(wt_hbm, w_ref, stage, sem):
    for r0, c0 in W_SECTIONS:
        cp = pltpu.make_async_copy(wt_hbm.at[pl.ds(r0, ATTN_W), :], stage, sem)
        cp.start()
        cp.wait()
        w_ref[:, c0:c0 + ATTN_W] = stage[...].T.astype(BF16)
    cp = pltpu.make_async_copy(wt_hbm.at[pl.ds(W_F_ROW, HEADS), :], stage.at[pl.ds(0, HEADS), :], sem)
    cp.start()
    cp.wait()
    f = stage[0:HEADS, :]
    z = lambda n: jnp.zeros((n, D_MODEL), F32)
    fblk = jnp.concatenate([f, z(F_HI - HEADS), f, f, f, z(LANES - F_LO - HEADS)], axis=0)
    w_ref[:, COL_F:COL_F + LANES] = fblk.T.astype(BF16)


def _proj_kernel(prepare, x_ref, st_ref, gm_ref, w_in_ref, bf_ref, cw_ref, gc_ref,
                 k_ref, v_ref, lf_ref, qa_ref, ka_ref, mb_ref, ncv_ref, *rest):
    i = pl.program_id(1)
    nb, tm, _ = x_ref.shape
    if prepare:
        w_ref, ubuf, ccar, stage, sem = rest

        @pl.when((pl.program_id(0) == 0) & (i == 0))
        def _():
            _prepare_weight(w_in_ref, w_ref, stage, sem)
    else:
        w_ref = w_in_ref
        ubuf, ccar = rest

    @pl.when(i == 0)
    def _():
        ccar[...] = jnp.zeros_like(ccar)
        ubuf[:, 0:8, :] = st_ref[...]

    @pl.when(i > 0)
    def _():
        ubuf[:, 0:8, :] = ubuf[:, tm:tm + 8, :]

    lane = lax.broadcasted_iota(jnp.int32, (tm, LANES), 1)
    pad = (-tm) % LANES
    for bb in range(nb):
        xn = _rms(x_ref[bb], gm_ref[...]).astype(BF16)
        proj = lambda c0, n: _dot(xn, w_ref[:, c0:c0 + n])

        lfb = _log_sigmoid(proj(COL_F, LANES) + bf_ref[...])
        lf_ref[bb] = (_pad_rows(lfb, tm + pad) if pad else lfb).T[F_OUT:F_OUT + HEADS, 0:tm]
        c = _cumsum_rows(lfb) + ccar[bb, 0:1, :]
        ccar[bb] = jnp.broadcast_to(c[tm - 1:tm, :], ccar.shape[1:])
        bias_lanes = _key_bias_lanes(c * (-LOG2E))
        k = proj(COL_K, ATTN_W)
        k_ref[bb] = k
        for h in range(HEADS):
            ka_ref[bb, h] = _key_operand(k, bias_lanes, h)
        q = proj(COL_Q, ATTN_W)
        for h in range(HEADS):
            sel = jnp.where((lane == F_HI + h) | (lane == F_MID + h) | (lane == F_LO + h), 1.0, 0.0)
            qa_ref[bb, h] = jnp.where(lane < HEAD_DIM, _head_block(q, h) * Q_SCALE, sel).astype(BF16)
        v_ref[bb] = proj(COL_V, ATTN_W)

        u = proj(COL_CG, CONV_W) * proj(COL_H, CONV_W)
        ubuf[bb, 8:8 + tm, :] = u
        y = (cw_ref[0:1, :] * ubuf[bb, 6:6 + tm, :] + cw_ref[1:2, :] * ubuf[bb, 7:7 + tm, :]
             + cw_ref[2:3, :] * u)
        ob = proj(COL_BG, CONV_W) * y
        mb_ref[bb] = _rms(ob, gc_ref[...]).astype(BF16)
        ncv_ref[bb] = u[tm - 2:tm, :]


def _proj(x, state, gm, w, bias_f, cw, gc, tm, nb):
    b, s, _ = x.shape
    prepare = w.dtype == F32
    full = lambda shape: pl.BlockSpec(shape, lambda bi, i: (0,) * len(shape))
    out_specs = [
        pl.BlockSpec((nb, tm, ATTN_W), lambda bi, i: (bi, i, 0)),
        pl.BlockSpec((nb, tm, ATTN_W), lambda bi, i: (bi, i, 0)),
        pl.BlockSpec((nb, HEADS, tm), lambda bi, i: (bi, 0, i)),
        pl.BlockSpec((nb, HEADS, tm, LANES), lambda bi, i: (bi, 0, i, 0)),
        pl.BlockSpec((nb, HEADS, tm, LANES), lambda bi, i: (bi, 0, i, 0)),
        pl.BlockSpec((nb, tm, CONV_W), lambda bi, i: (bi, i, 0)),
        pl.BlockSpec((nb, 2, CONV_W), lambda bi, i: (bi, 0, 0)),
    ]
    out_shape = [
        jax.ShapeDtypeStruct((b, s, ATTN_W), F32),
        jax.ShapeDtypeStruct((b, s, ATTN_W), F32),
        jax.ShapeDtypeStruct((b, HEADS, s), F32),
        jax.ShapeDtypeStruct((b, HEADS, s, LANES), BF16),
        jax.ShapeDtypeStruct((b, HEADS, s, LANES), BF16),
        jax.ShapeDtypeStruct((b, s, CONV_W), BF16),
        jax.ShapeDtypeStruct((b, 2, CONV_W), F32),
    ]
    scratch = [pltpu.VMEM((nb, tm + 8, CONV_W), F32), pltpu.VMEM((nb, 8, LANES), F32)]
    if prepare:
        out_specs.append(full((D_MODEL, PROJ_PAD)))
        out_shape.append(jax.ShapeDtypeStruct((D_MODEL, PROJ_PAD), BF16))
        scratch += [pltpu.VMEM((ATTN_W, D_MODEL), F32), pltpu.SemaphoreType.DMA(())]
    return pl.pallas_call(
        functools.partial(_proj_kernel, prepare),
        grid=(b // nb, s // tm),
        in_specs=[
            pl.BlockSpec((nb, tm, D_MODEL), lambda bi, i: (bi, i, 0)),
            pl.BlockSpec((nb, 8, CONV_W), lambda bi, i: (bi, 0, 0)),
            full((1, D_MODEL)),
            pl.BlockSpec(memory_space=pl.ANY) if prepare else full((D_MODEL, PROJ_PAD)),
            full((1, LANES)),
            full((8, CONV_W)),
            full((1, CONV_W)),
        ],
        out_specs=out_specs,
        out_shape=out_shape,
        scratch_shapes=scratch,
        compiler_params=pltpu.CompilerParams(
            dimension_semantics=("arbitrary", "arbitrary"), vmem_limit_bytes=VMEM_LIMIT),
        name="proj",
    )(x, state, gm, w, bias_f, cw, gc)


ATT_BLK = 256
PV_CHUNK = 256


def _attn_kernel(qa_ref, ka_ref, v_ref, wgu_ref, wdn_ref, o_ref, wgu_bf_ref, wdn_bf_ref, sbuf):
    wgu_bf_ref[...] = wgu_ref[...].astype(BF16)
    wdn_bf_ref[...] = wdn_ref[...].astype(BF16)
    blk = ATT_BLK
    vt = v_ref[0].T.astype(BF16)
    r = lax.broadcasted_iota(jnp.int32, (blk, blk), 0)
    c = lax.broadcasted_iota(jnp.int32, (blk, blk), 1)
    causal = r <= c
    units = [(iq, hh) for iq in range(ka_ref.shape[2] // blk) for hh in range(2)]

    def scores(u):
        iq, hh = units[u]
        n = (iq + 1) * blk
        sbuf[u % nbuf, 0:n, :] = _dot_nt(ka_ref[0, hh, 0:n, :], qa_ref[0, hh, iq * blk:n, :])

    nbuf = sbuf.shape[0]
    for u in range(nbuf - 1):
        scores(u)
    outs = []
    for u, (iq, hh) in enumerate(units):
        if u + nbuf - 1 < len(units):
            scores(u + nbuf - 1)
        noff = iq * blk
        rows = slice(hh * HEAD_DIM, (hh + 1) * HEAD_DIM)
        s_d = jnp.where(causal, sbuf[u % nbuf, noff:noff + blk, :], -jnp.inf)
        m = jnp.max(s_d, axis=0, keepdims=True)
        if iq > 0:
            m = jnp.maximum(m, jnp.max(sbuf[u % nbuf, 0:noff, :], axis=0, keepdims=True))
        p_d = jnp.exp2(s_d - m)
        l = jnp.sum(p_d, axis=0, keepdims=True)
        acc = _dot(vt[rows, noff:noff + blk], p_d.astype(BF16))
        for c0 in range(0, noff, PV_CHUNK):
            c1 = min(c0 + PV_CHUNK, noff)
            p_o = jnp.exp2(sbuf[u % nbuf, c0:c1, :] - m)
            l = l + jnp.sum(p_o, axis=0, keepdims=True)
            acc = acc + _dot(vt[rows, c0:c1], p_o.astype(BF16))
        outs.append(acc / l)
        if hh == 1:
            o_ref[0, noff:noff + blk, :] = jnp.concatenate(outs, axis=0).T
            outs = []


def _attention(qa, ka, v, wgu, wdn):
    b, _, s, _ = qa.shape
    pairs = HEADS // 2
    ne = wgu.shape[0] // (b * pairs)
    assert ne * b * pairs == wgu.shape[0] == wdn.shape[0]
    wspec = lambda w: pl.BlockSpec((ne,) + w.shape[1:], lambda bi, p: (bi * pairs + p, 0, 0))
    return pl.pallas_call(
        _attn_kernel,
        grid=(b, pairs),
        in_specs=[
            pl.BlockSpec((1, 2, s, LANES), lambda bi, p: (bi, p, 0, 0)),
            pl.BlockSpec((1, 2, s, LANES), lambda bi, p: (bi, p, 0, 0)),
            pl.BlockSpec((1, s, LANES), lambda bi, p: (bi, 0, p)),
            wspec(wgu),
            wspec(wdn),
        ],
        out_specs=[pl.BlockSpec((1, s, LANES), lambda bi, p: (bi, 0, p)), wspec(wgu), wspec(wdn)],
        out_shape=[jax.ShapeDtypeStruct((b, s, ATTN_W), F32),
                   jax.ShapeDtypeStruct(wgu.shape, BF16), jax.ShapeDtypeStruct(wdn.shape, BF16)],
        scratch_shapes=[pltpu.VMEM((4, s, ATT_BLK), F32)],
        compiler_params=pltpu.CompilerParams(
            dimension_semantics=("arbitrary", "arbitrary"), vmem_limit_bytes=VMEM_LIMIT),
        name="attention",
    )(qa, ka, v, wgu, wdn)


Q_PAD = 128


def _cumsum_lanes(x):
    n = x.shape[1]
    r = lax.broadcasted_iota(jnp.int32, (n, n), 0)
    c = lax.broadcasted_iota(jnp.int32, (n, n), 1)
    tri = jnp.where(r <= c, 1.0, 0.0).astype(BF16)
    terms = jnp.concatenate([a.astype(F32) for a in _split3(x)], axis=0).astype(BF16)
    parts = _dot(terms, tri)
    rows = x.shape[0]
    return parts[0:rows] + parts[rows:2 * rows] + parts[2 * rows:]


def _attn_cache_kernel(qa_ref, ka_ref, v_ref, ckt_ref, cvt_ref, clft_ref, o_ref):
    t = qa_ref.shape[2]
    plen = ckt_ref.shape[2]
    cinc = _cumsum_lanes(clft_ref[0])
    bias = (cinc - cinc[:, plen - 1:plen]) * (-LOG2E)
    lane = lax.broadcasted_iota(jnp.int32, (t, LANES), 1)
    rq = lax.broadcasted_iota(jnp.int32, (t, Q_PAD), 0)
    ck = lax.broadcasted_iota(jnp.int32, (t, Q_PAD), 1)
    new_ok = ck <= rq

    pairs = []
    for p in range(HEADS // 2):
        rows = slice(p * LANES, (p + 1) * LANES)
        kt = ckt_ref[0, rows, :].astype(BF16)
        vt = cvt_ref[0, rows, :].astype(BF16)
        vn = _pad_rows(v_ref[0][:, rows], Q_PAD).astype(BF16)
        res = []
        for hh in range(2):
            h = 2 * p + hh
            qa = qa_ref[0, h]
            qh = jnp.where(lane < HEAD_DIM, qa.astype(F32), 0.0)
            qh = (pltpu.roll(qh, HEAD_DIM, 1) if hh else qh).astype(BF16)
            s_past = _dot(qh, kt) + bias[h:h + 1, :]
            s_new = jnp.where(new_ok, _dot_nt(qa, _pad_rows(ka_ref[0, h], Q_PAD)), -jnp.inf)
            m = jnp.maximum(jnp.max(s_past, axis=1, keepdims=True), jnp.max(s_new, axis=1, keepdims=True))
            p_past = jnp.exp2(s_past - m)
            p_new = jnp.exp2(s_new - m)
            l = jnp.sum(p_past, axis=1, keepdims=True) + jnp.sum(p_new, axis=1, keepdims=True)
            acc = _dot_nt(p_past.astype(BF16), vt) + _dot(p_new.astype(BF16), vn)
            res.append(acc / l)
        pairs.append(jnp.where(lane < HEAD_DIM, res[0], res[1]))
    o_ref[0] = jnp.concatenate(pairs, axis=1)


def _attention_cache(qa, ka, v, ckt, cvt, clft):
    b, _, t, _ = qa.shape
    plen = ckt.shape[2]
    return pl.pallas_call(
        _attn_cache_kernel,
        grid=(b,),
        in_specs=[
            pl.BlockSpec((1, HEADS, t, LANES), lambda bi: (bi, 0, 0, 0)),
            pl.BlockSpec((1, HEADS, t, LANES), lambda bi: (bi, 0, 0, 0)),
            pl.BlockSpec((1, t, ATTN_W), lambda bi: (bi, 0, 0)),
            pl.BlockSpec((1, ATTN_W, plen), lambda bi: (bi, 0, 0)),
            pl.BlockSpec((1, ATTN_W, plen), lambda bi: (bi, 0, 0)),
            pl.BlockSpec((1, HEADS, plen), lambda bi: (bi, 0, 0)),
        ],
        out_specs=pl.BlockSpec((1, t, ATTN_W), lambda bi: (bi, 0, 0)),
        out_shape=jax.ShapeDtypeStruct((b, t, ATTN_W), F32),
        compiler_params=pltpu.CompilerParams(
            dimension_semantics=("arbitrary",), vmem_limit_bytes=VMEM_LIMIT),
        name="attention_cache",
    )(qa, ka, v, ckt, cvt, clft)


def _mix_kernel(et, x_ref, o_ref, mb_ref, wo_ref, ga_ref, gf_ref, wr_ref, br_ref,
                rows_ref, slot_ref, cnt_ref, carry):
    i = pl.program_id(0)
    tm = slot_ref.shape[2]
    nsub = x_ref.shape[0] // tm
    row = lax.broadcasted_iota(jnp.int32, (EPG, tm), 0)
    row2 = lax.broadcasted_iota(jnp.int32, (2 * EPG, tm), 0)
    tok = lax.broadcasted_iota(jnp.int32, (2 * EPG, tm), 1)
    r2 = lax.broadcasted_iota(jnp.int32, (tm, tm), 0)
    c2 = lax.broadcasted_iota(jnp.int32, (tm, tm), 1)
    earlier = jnp.where(r2 < c2, 1.0, 0.0).astype(BF16)

    @pl.when(i == 0)
    def _():
        carry[...] = jnp.zeros_like(carry)

    total = carry[:, 0:1]
    for j in range(nsub):
        rs = slice(j * tm, (j + 1) * tm)
        oa = _rms(o_ref[rs, :], ga_ref[...]).astype(BF16)
        merged = jnp.concatenate([oa, mb_ref[rs, :]], axis=1)
        x2 = x_ref[rs, :] + _dot(merged, wo_ref[...])
        rows_ref[rs, 0:D_MODEL] = x2
        xb = _rms(x2, gf_ref[...]).astype(BF16)

        lt = (_dot(xb, wr_ref[...]) + br_ref[...]).T
        lg = jnp.where(row < N_GROUPS, lt[0:EPG], -jnp.inf)
        eg = jnp.exp(lg - jnp.max(lg, axis=0, keepdims=True))
        pg = eg / jnp.sum(eg, axis=0, keepdims=True)
        pg_top = jnp.max(pg, axis=0, keepdims=True)
        g_top = jnp.min(jnp.where(pg == pg_top, row, EPG), axis=0, keepdims=True)

        le = lt[EPG:2 * EPG]
        for g in range(1, N_GROUPS):
            le = jnp.where(g_top == g, lt[EPG * (g + 1):EPG * (g + 2)], le)
        ee = jnp.exp(le - jnp.max(le, axis=0, keepdims=True))
        pe = ee / jnp.sum(ee, axis=0, keepdims=True)
        v1 = jnp.max(pe, axis=0, keepdims=True)
        i1 = jnp.min(jnp.where(pe == v1, row, EPG), axis=0, keepdims=True)
        pe2 = jnp.where(row == i1, -1.0, pe)
        v2 = jnp.max(pe2, axis=0, keepdims=True)
        i2 = jnp.min(jnp.where(pe2 == v2, row, EPG), axis=0, keepdims=True)
        den = v1 + v2
        wg = jnp.where(row == i1, v1 / den * pg_top, 0.0) + jnp.where(row == i2, v2 / den * pg_top, 0.0)
        meta = jnp.where(row2 == META_ID, ((i * nsub + j) * tm + tok).astype(F32),
                         jnp.concatenate([wg, jnp.zeros((EPG, tm), F32)], axis=0))
        rows_ref[rs, D_MODEL:] = jnp.concatenate([meta, jnp.zeros((LANES - 2 * EPG, tm), F32)], axis=0).T

        onehot = jnp.where(row == g_top, 1.0, 0.0)
        rank = jnp.sum(onehot * (_dot(onehot.astype(BF16), earlier) + total), axis=0, keepdims=True)
        slot_ref[i * nsub + j] = g_top * (1 << GROUP_SHIFT) + rank.astype(jnp.int32)
        total = total + jnp.sum(onehot, axis=1, keepdims=True)

    carry[...] = jnp.broadcast_to(total, carry.shape)
    cnt_ref[...] = carry[...].astype(jnp.int32)

    @pl.when(i == pl.num_programs(0) - 1)
    def _():
        padded = jnp.floor((total + (et - 1)) * (1.0 / et)) * et
        o1 = padded[0:1]
        o2 = o1 + padded[1:2]
        o3 = o2 + padded[2:3]
        code = slot_ref[...]
        g = code >> GROUP_SHIFT
        first = jnp.where(g == 0, 0.0, jnp.where(g == 1, o1, jnp.where(g == 2, o2, o3)))
        slot_ref[...] = first.astype(jnp.int32) + (code & ((1 << GROUP_SHIFT) - 1))


def _mix(x, o, mb, wo, ga, gf, wr, br, tm, nsub):
    n = x.shape[0]
    assert n < (1 << GROUP_SHIFT)
    tb = tm * nsub
    full = lambda shape: pl.BlockSpec(shape, lambda i: (0,) * len(shape))
    return pl.pallas_call(
        functools.partial(_mix_kernel, _tile_rows(n)),
        grid=(n // tb,),
        in_specs=[
            pl.BlockSpec((tb, D_MODEL), lambda i: (i, 0)),
            pl.BlockSpec((tb, ATTN_W), lambda i: (i, 0)),
            pl.BlockSpec((tb, CONV_W), lambda i: (i, 0)),
            full((D_MODEL, D_MODEL)),
            full((1, ATTN_W)),
            full((1, D_MODEL)),
            full((D_MODEL, LANES)),
            full((1, LANES)),
        ],
        out_specs=[
            pl.BlockSpec((tb, ROW_W), lambda i: (i, 0)),
            full((n // tm, 1, tm)),
            full((EPG, LANES)),
        ],
        out_shape=[
            jax.ShapeDtypeStruct((n, ROW_W), F32),
            jax.ShapeDtypeStruct((n // tm, 1, tm), jnp.int32),
            jax.ShapeDtypeStruct((EPG, LANES), jnp.int32),
        ],
        scratch_shapes=[pltpu.VMEM((EPG, LANES), F32)],
        compiler_params=pltpu.CompilerParams(
            dimension_semantics=("arbitrary",), vmem_limit_bytes=VMEM_LIMIT),
        name="mix",
    )(x, o, mb, wo, ga, gf, wr, br)


def _sorted_tiles(n):
    return n // _tile_rows(n) + N_GROUPS


def _group_tiles(cnt_ref, et):
    nt = [(cnt_ref[g] + (et - 1)) >> (et.bit_length() - 1) for g in range(N_GROUPS)]
    s1 = nt[0]
    s2 = s1 + nt[1]
    s3 = s2 + nt[2]
    return s1, s2, s3, s3 + nt[3]


def _scatter_kernel(cnt_ref, rows_ref, slot_ref, xs_ref, zeros, sem, zsem):
    tm = rows_ref.shape[0]
    et = zeros.shape[0]

    @pl.when(pl.program_id(0) == 0)
    def _():
        zeros[...] = jnp.zeros_like(zeros)
        s1, s2, s3, tot = _group_tiles(cnt_ref, et)
        ends = (s1, s2, s3, tot)

        def fill(tile):
            return pltpu.make_async_copy(zeros, xs_ref.at[pl.ds(tile * et, et), :], zsem)

        for g in range(N_GROUPS):
            partial = (cnt_ref[g] & (et - 1)) != 0

            @pl.when(partial)
            def _():
                fill(ends[g] - 1).start()
                fill(ends[g] - 1).wait()

        def fill_tail(tile, c):
            fill(tile).start()
            fill(tile).wait()
            return c

        lax.fori_loop(tot, xs_ref.shape[0] // et, fill_tail, 0)

    copy_of_row = lambda r: pltpu.make_async_copy(
        rows_ref.at[pl.ds(r, 1), :], xs_ref.at[pl.ds(slot_ref[0, 0, r], 1), :], sem)
    for r in range(tm):
        copy_of_row(r).start(priority=r % 2)
    for r in range(tm):
        copy_of_row(r).wait()


def _scatter(cnt, rows, slots, tm):
    n = rows.shape[0]
    et = _tile_rows(n)
    return pl.pallas_call(
        _scatter_kernel,
        grid_spec=pltpu.PrefetchScalarGridSpec(
            num_scalar_prefetch=1,
            grid=(n // tm,),
            in_specs=[
                pl.BlockSpec((tm, ROW_W), lambda i, c: (i, 0)),
                pl.BlockSpec((1, 1, tm), lambda i, c: (i, 0, 0), memory_space=pltpu.SMEM),
            ],
            out_specs=pl.BlockSpec(memory_space=pl.ANY),
            scratch_shapes=[pltpu.VMEM((et, ROW_W), F32),
                            pltpu.SemaphoreType.DMA(()), pltpu.SemaphoreType.DMA(())],
        ),
        out_shape=jax.ShapeDtypeStruct((_sorted_tiles(n) * et, ROW_W), F32),
        compiler_params=pltpu.CompilerParams(
            dimension_semantics=("arbitrary",), vmem_limit_bytes=VMEM_LIMIT),
        name="scatter",
    )(cnt, rows, slots)


def _tile_of(t, cnt_ref, et):
    s1, s2, s3, tot = _group_tiles(cnt_ref, et)
    tc = jnp.minimum(t, tot - 1)
    g = (tc >= s1).astype(jnp.int32) + (tc >= s2).astype(jnp.int32) + (tc >= s3).astype(jnp.int32)
    first = jnp.where(g == 0, 0, jnp.where(g == 1, s1, jnp.where(g == 2, s2, s3)))
    cnt = jnp.where(g == 0, cnt_ref[0], jnp.where(g == 1, cnt_ref[1], jnp.where(g == 2, cnt_ref[2], cnt_ref[3])))
    return tc, g, jnp.where(t < tot, cnt - (tc - first) * et, 0)


def _experts_kernel(cnt_ref, xs_ref, wgu_ref, wdn_ref, gf_ref, gn_ref, y_hbm, yb0, yb1, idv, ids, pend, sem, isem):
    t = pl.program_id(0)
    et = xs_ref.shape[0]
    _, _, cur_valid = _tile_of(t, cnt_ref, et)
    prev_valid = jnp.where(t > 0, jnp.clip(_tile_of(t - 1, cnt_ref, et)[2], 0, et), 0)
    fast = (prev_valid == et) & (cur_valid > 0)

    @pl.when(t == 0)
    def _():
        pend[0] = 0

    def drain(k):
        one = pltpu.make_async_copy(yb0.at[pl.ds(0, 1), :], y_hbm.at[pl.ds(0, 1), :], sem.at[k])

        @pl.when(pend[0] == et)
        def _():
            for _ in range(et):
                one.wait()

        @pl.when(pend[0] != et)
        def _():
            def body(r, c):
                one.wait()
                return c
            lax.fori_loop(0, pend[0], body, 0)

    def compute(ybuf, par):
        idv[par:par + 1, :] = xs_ref[:, D_MODEL:].T[META_ID:META_ID + 1, :].astype(jnp.int32)
        pltpu.make_async_copy(idv.at[par], ids.at[par], isem).start()
        tot = xs_ref[:, 0:D_MODEL]
        xb = _rms(tot, gf_ref[...]).astype(BF16)
        wts = xs_ref[:, D_MODEL:]
        for e in range(EPG):
            gu = _dot(xb, wgu_ref[0, e])
            gate = gu[:, :D_EXPERT]
            hmid = gate * jax.nn.sigmoid(gate) * gu[:, D_EXPERT:] * wts[:, e:e + 1]
            tot = tot + _dot(hmid.astype(BF16), wdn_ref[0, e])
        y = _rms(tot, gn_ref[...])
        drain(1 - par)
        ybuf[...] = y

    def ids_wait(par):
        pltpu.make_async_copy(idv.at[par], ids.at[par], isem).wait()

    def row_copy(ybuf, par, r):
        return pltpu.make_async_copy(ybuf.at[pl.ds(r, 1), :], y_hbm.at[pl.ds(ids[1 - par, r], 1), :], sem.at[par])

    for par, (ycur, yprev) in enumerate(((yb0, yb1), (yb1, yb0))):
        @pl.when(((t % 2) == par) & fast)
        def _():
            ids_wait(1 - par)
            for r in range(et):
                row_copy(yprev, par, r).start(priority=r % 2)
            compute(ycur, par)

        @pl.when(((t % 2) == par) & jnp.logical_not(fast))
        def _():
            @pl.when(prev_valid > 0)
            def _():
                ids_wait(1 - par)

                def start(r, c):
                    row_copy(yprev, par, r).start()
                    return c
                lax.fori_loop(0, prev_valid, start, 0)

            @pl.when(cur_valid > 0)
            def _():
                compute(ycur, par)

            @pl.when(cur_valid <= 0)
            def _():
                drain(1 - par)

        @pl.when((t % 2) == par)
        def _():
            pend[0] = prev_valid

            @pl.when(t == pl.num_programs(0) - 1)
            def _():
                drain(par)


def _experts(cnt, xs, wgu, wdn, gf, gn, n):
    et = _tile_rows(n)
    tiles = xs.shape[0] // et
    grp = lambda t, c: (_tile_of(t, c, et)[1], 0, 0, 0)
    return pl.pallas_call(
        _experts_kernel,
        grid_spec=pltpu.PrefetchScalarGridSpec(
            num_scalar_prefetch=1,
            grid=(tiles + 1,),
            in_specs=[
                pl.BlockSpec((et, ROW_W), lambda t, c: (_tile_of(t, c, et)[0], 0)),
                pl.BlockSpec((1, EPG, D_MODEL, 2 * D_EXPERT), grp),
                pl.BlockSpec((1, EPG, D_EXPERT, D_MODEL), grp),
                pl.BlockSpec((1, D_MODEL), lambda t, c: (0, 0)),
                pl.BlockSpec((1, D_MODEL), lambda t, c: (0, 0)),
            ],
            out_specs=pl.BlockSpec(memory_space=pl.ANY),
            scratch_shapes=[pltpu.VMEM((et, D_MODEL), F32), pltpu.VMEM((et, D_MODEL), F32),
                            pltpu.VMEM((8, et), jnp.int32), pltpu.SMEM((2, et), jnp.int32),
                            pltpu.SMEM((1,), jnp.int32), pltpu.SemaphoreType.DMA((2,)), pltpu.SemaphoreType.DMA(())],
        ),
        out_shape=jax.ShapeDtypeStruct((n, D_MODEL), F32),
        compiler_params=pltpu.CompilerParams(
            dimension_semantics=("arbitrary",), vmem_limit_bytes=VMEM_LIMIT),
        name="experts",
    )(cnt, xs, wgu, wdn, gf, gn)


def _forget_lanes(f):
    z = lambda n: jnp.zeros(f.shape[:-1] + (n,), f.dtype)
    return jnp.concatenate([f, z(F_HI - HEADS), f, f, f, z(LANES - F_LO - HEADS)], axis=-1)


def _trunk(x, attend, state, w, tm_proj, nb_proj, tm_tok, nsub_tok):
    b, s, _ = x.shape
    k, v, lf, qa, ka, mb, ncv, *built = _proj(
        x, state, w["gm"], w["w_proj"], w["bias_f"], w["cw"], w["gc"], tm_proj, nb_proj)
    if built:
        w["w_proj"] = built[0]
    o = attend(qa, ka, v)
    n = b * s
    rows, slots, cnt = _mix(x.reshape(n, D_MODEL), o.reshape(n, ATTN_W), mb.reshape(n, CONV_W),
                            w["wo"], w["ga"], w["gf"], w["wr"], w["br"], tm_tok, nsub_tok)
    cnt = cnt[:, 0]
    y = _experts(cnt, _scatter(cnt, rows, slots, tm_tok), w["wgu"], w["wdn"], w["gf"], w["gn"], n)
    return (y.reshape(b, s, D_MODEL), k.reshape(1, b, s, HEADS, HEAD_DIM), v.reshape(1, b, s, HEADS, HEAD_DIM),
            lf.transpose(0, 2, 1).reshape(1, b, s, HEADS), ncv.reshape(1, b, 2, CONV_W))


def kernel(x_prompt, x_sample, cache_k, cache_v, cache_logf, state_conv, norm_mix_g, w_in, b_forget, conv_w, norm_attn_g, norm_conv_g, w_out, norm_ffn_g, w_router_group, b_router_group, w_router_expert, b_router_expert, w_expert_gate_up, w_expert_down, norm_final_g):
    assert w_in.shape[0] == 1, "single-layer trunk"
    wr = jnp.concatenate(
        [w_router_group[0], jnp.zeros((D_MODEL, EPG - N_GROUPS), F32),
         w_router_expert[0].transpose(1, 0, 2).reshape(D_MODEL, N_GROUPS * EPG),
         jnp.zeros((D_MODEL, LANES - EPG - N_GROUPS * EPG), F32)], axis=1).astype(BF16)
    br = jnp.concatenate(
        [b_router_group[0], jnp.zeros((EPG - N_GROUPS,), F32), b_router_expert[0].reshape(-1),
         jnp.zeros((LANES - EPG - N_GROUPS * EPG,), F32)]).reshape(1, LANES)
    w = dict(
        gm=norm_mix_g[0].reshape(1, D_MODEL), w_proj=w_in[0].T, bias_f=_forget_lanes(b_forget[0]).reshape(1, LANES),
        cw=jnp.concatenate([conv_w[0], jnp.zeros((8 - conv_w.shape[1], CONV_W), F32)], axis=0),
        gc=norm_conv_g[0].reshape(1, CONV_W), wo=w_out[0].astype(BF16), ga=norm_attn_g[0].reshape(1, ATTN_W),
        gf=norm_ffn_g[0].reshape(1, D_MODEL), wr=wr, br=br,
        wgu=w_expert_gate_up[0], wdn=w_expert_down[0],
        gn=norm_final_g.reshape(1, D_MODEL))

    bp, sp, _ = x_prompt.shape
    bs, ss, _ = x_sample.shape
    plen = cache_k.shape[2]

    def attend_p(qa, ka, v):
        wgu, wdn = w["wgu"], w["wdn"]
        o, wgu_bf, wdn_bf = _attention(qa, ka, v, wgu.reshape((-1,) + wgu.shape[2:]), wdn.reshape((-1,) + wdn.shape[2:]))
        w["wgu"], w["wdn"] = wgu_bf.reshape(wgu.shape), wdn_bf.reshape(wdn.shape)
        return o

    yp, kp, vp, lfp, cvp = _trunk(x_prompt, attend_p, jnp.zeros((bp, 8, CONV_W), F32), w, 512, 2, 512, 2)

    ckt = cache_k[0].transpose(0, 2, 3, 1).reshape(bs, ATTN_W, plen)
    cvt = cache_v[0].transpose(0, 2, 3, 1).reshape(bs, ATTN_W, plen)
    clft = cache_logf[0].transpose(0, 2, 1)
    st = jnp.concatenate([jnp.zeros((bs, 6, CONV_W), F32), state_conv[0]], axis=1)
    attend_s = lambda qa, ka, v: _attention_cache(qa, ka, v, ckt, cvt, clft)
    ys, ks, vs, lfs, cvs = _trunk(x_sample, attend_s, st, w, ss, bs, bs * ss, 1)
    return (yp, ys, kp, vp, lfp, cvp, ks, vs, lfs, cvs)
```

```python
import functools
import math

import jax
import jax.numpy as jnp
from jax import lax
from jax.experimental import pallas as pl
from jax.experimental.pallas import tpu as pltpu

F32 = jnp.float32
BF16 = jnp.bfloat16

D_MODEL = 1024
HEADS = 8
HEAD_DIM = 64
ATTN_W = HEADS * HEAD_DIM
CONV_W = 512
N_GROUPS = 4
EPG = 8
D_EXPERT = 256
EPS = 1e-6
LOG2E = math.log2(math.e)
Q_SCALE = HEAD_DIM ** -0.5 * LOG2E

LANES = 128
COL_Q, COL_K, COL_V, COL_F, COL_BG, COL_CG, COL_H = 0, 512, 1024, 1536, 1664, 2176, 2688
PROJ_PAD = 3200
W_SECTIONS = ((0, COL_Q), (512, COL_K), (1024, COL_V), (1544, COL_BG), (2056, COL_CG), (2568, COL_H))
W_F_ROW = 1536
F_OUT, F_HI, F_MID, F_LO = 0, 64, 72, 80

GROUP_SHIFT = 20
ROW_W = D_MODEL + LANES
META_ID = EPG


def _tile_rows(n):
    return 512 if n >= 4096 else 128

VMEM_LIMIT = 60 * 1024 * 1024


def _dot(a, b):
    return jnp.dot(a, b, preferred_element_type=F32)


def _dot_nt(a, b):
    return lax.dot_general(a, b, (((1,), (1,)), ((), ())), preferred_element_type=F32)


def _rms(x, g):
    return x * lax.rsqrt(jnp.mean(x * x, axis=-1, keepdims=True) + EPS) * g


def _log_sigmoid(x):
    return jnp.minimum(x, 0.0) - jnp.log(1.0 + jnp.exp(-jnp.abs(x)))


def _pad_rows(x, n):
    return jnp.concatenate([x, jnp.zeros((n - x.shape[0], x.shape[1]), x.dtype)], axis=0)


def _split3(x):
    a1 = x.astype(BF16)
    r1 = x - a1.astype(F32)
    a2 = r1.astype(BF16)
    a3 = (r1 - a2.astype(F32)).astype(BF16)
    return a1, a2, a3


def _cumsum_rows(x):
    n = x.shape[0]
    r = lax.broadcasted_iota(jnp.int32, (n, n), 0)
    c = lax.broadcasted_iota(jnp.int32, (n, n), 1)
    tri = jnp.where(r >= c, 1.0, 0.0).astype(BF16)
    parts = _dot(tri, jnp.concatenate(_split3(x), axis=1))
    return parts[:, :LANES] + parts[:, LANES:2 * LANES] + parts[:, 2 * LANES:]


def _key_bias_lanes(cneg):
    lane = lax.broadcasted_iota(jnp.int32, cneg.shape, 1)
    hi = cneg.astype(BF16).astype(F32)
    r1 = cneg - hi
    mid = r1.astype(BF16).astype(F32)
    lo = r1 - mid
    out = jnp.where(lane < F_MID, hi, jnp.where(lane < F_LO, mid, lo))
    return jnp.where((lane >= F_HI) & (lane < F_LO + HEADS), out, 0.0)


def _head_block(x, h):
    blk = x[:, (h // 2) * LANES:(h // 2 + 1) * LANES]
    return pltpu.roll(blk, HEAD_DIM, 1) if h % 2 else blk


def _key_operand(k, bias_lanes, h):
    lane = lax.broadcasted_iota(jnp.int32, bias_lanes.shape, 1)
    return jnp.where(lane < HEAD_DIM, _head_block(k, h), bias_lanes).astype(BF16)


def _prepare_weight(wt_hbm, w_ref, stage, sem):
    for r0, c0 in W_SECTIONS:
        cp = pltpu.make_async_copy(wt_hbm.at[pl.ds(r0, ATTN_W), :], stage, sem)
        cp.start()
        cp.wait()
        w_ref[:, c0:c0 + ATTN_W] = stage[...].T.astype(BF16)
    cp = pltpu.make_async_copy(wt_hbm.at[pl.ds(W_F_ROW, HEADS), :], stage.at[pl.ds(0, HEADS), :], sem)
    cp.start()
    cp.wait()
    f = stage[0:HEADS, :]
    z = lambda n: jnp.zeros((n, D_MODEL), F32)
    fblk = jnp.concatenate([f, z(F_HI - HEADS), f, f, f, z(LANES - F_LO - HEADS)], axis=0)
    w_ref[:, COL_F:COL_F + LANES] = fblk.T.astype(BF16)


def _proj_kernel(prepare, x_ref, st_ref, gm_ref, w_in_ref, bf_ref, cw_ref, gc_ref,
                 k_ref, v_ref, lf_ref, qa_ref, ka_ref, mb_ref, ncv_ref, *rest):
    i = pl.program_id(1)
    nb, tm, _ = x_ref.shape
    if prepare:
        w_ref, ubuf, ccar, stage, sem = rest

        @pl.when((pl.program_id(0) == 0) & (i == 0))
        def _():
            _prepare_weight(w_in_ref, w_ref, stage, sem)
    else:
        w_ref = w_in_ref
        ubuf, ccar = rest

    @pl.when(i == 0)
    def _():
        ccar[...] = jnp.zeros_like(ccar)
        ubuf[:, 0:8, :] = st_ref[...]

    @pl.when(i > 0)
    def _():
        ubuf[:, 0:8, :] = ubuf[:, tm:tm + 8, :]

    lane = lax.broadcasted_iota(jnp.int32, (tm, LANES), 1)
    pad = (-tm) % LANES
    xn = _rms(x_ref[...].reshape(nb * tm, D_MODEL), gm_ref[...]).astype(BF16)
    sections = {c0: _dot(xn, w_ref[:, c0:c0 + n]) for c0, n in
                ((COL_F, LANES), (COL_K, ATTN_W), (COL_Q, ATTN_W), (COL_V, ATTN_W), (COL_CG, CONV_W),
                 (COL_H, CONV_W), (COL_BG, CONV_W))}
    for bb in range(nb):
        proj = lambda c0, n: sections[c0][bb * tm:(bb + 1) * tm, :]

        lfb = _log_sigmoid(proj(COL_F, LANES) + bf_ref[...])
        lf_ref[bb] = (_pad_rows(lfb, tm + pad) if pad else lfb).T[F_OUT:F_OUT + HEADS, 0:tm]
        c = _cumsum_rows(lfb) + ccar[bb, 0:1, :]
        ccar[bb] = jnp.broadcast_to(c[tm - 1:tm, :], ccar.shape[1:])
        bias_lanes = _key_bias_lanes(c * (-LOG2E))
        k = proj(COL_K, ATTN_W)
        k_ref[bb] = k
        for h in range(HEADS):
            ka_ref[bb, h] = _key_operand(k, bias_lanes, h)
        q = proj(COL_Q, ATTN_W)
        for h in range(HEADS):
            sel = jnp.where((lane == F_HI + h) | (lane == F_MID + h) | (lane == F_LO + h), 1.0, 0.0)
            qa_ref[bb, h] = jnp.where(lane < HEAD_DIM, _head_block(q, h) * Q_SCALE, sel).astype(BF16)
        v_ref[bb] = proj(COL_V, ATTN_W)

        u = proj(COL_CG, CONV_W) * proj(COL_H, CONV_W)
        ubuf[bb, 8:8 + tm, :] = u
        y = (cw_ref[0:1, :] * ubuf[bb, 6:6 + tm, :] + cw_ref[1:2, :] * ubuf[bb, 7:7 + tm, :]
             + cw_ref[2:3, :] * u)
        ob = proj(COL_BG, CONV_W) * y
        mb_ref[bb] = _rms(ob, gc_ref[...]).astype(BF16)
        ncv_ref[bb] = u[tm - 2:tm, :]


def _proj(x, state, gm, w, bias_f, cw, gc, tm, nb):
    b, s, _ = x.shape
    prepare = w.dtype == F32
    full = lambda shape: pl.BlockSpec(shape, lambda bi, i: (0,) * len(shape))
    out_specs = [
        pl.BlockSpec((nb, tm, ATTN_W), lambda bi, i: (bi, i, 0)),
        pl.BlockSpec((nb, tm, ATTN_W), lambda bi, i: (bi, i, 0)),
        pl.BlockSpec((nb, HEADS, tm), lambda bi, i: (bi, 0, i)),
        pl.BlockSpec((nb, HEADS, tm, LANES), lambda bi, i: (bi, 0, i, 0)),
        pl.BlockSpec((nb, HEADS, tm, LANES), lambda bi, i: (bi, 0, i, 0)),
        pl.BlockSpec((nb, tm, CONV_W), lambda bi, i: (bi, i, 0)),
        pl.BlockSpec((nb, 2, CONV_W), lambda bi, i: (bi, 0, 0)),
    ]
    out_shape = [
        jax.ShapeDtypeStruct((b, s, ATTN_W), F32),
        jax.ShapeDtypeStruct((b, s, ATTN_W), F32),
        jax.ShapeDtypeStruct((b, HEADS, s), F32),
        jax.ShapeDtypeStruct((b, HEADS, s, LANES), BF16),
        jax.ShapeDtypeStruct((b, HEADS, s, LANES), BF16),
        jax.ShapeDtypeStruct((b, s, CONV_W), BF16),
        jax.ShapeDtypeStruct((b, 2, CONV_W), F32),
    ]
    scratch = [pltpu.VMEM((nb, tm + 8, CONV_W), F32), pltpu.VMEM((nb, 8, LANES), F32)]
    if prepare:
        out_specs.append(full((D_MODEL, PROJ_PAD)))
        out_shape.append(jax.ShapeDtypeStruct((D_MODEL, PROJ_PAD), BF16))
        scratch += [pltpu.VMEM((ATTN_W, D_MODEL), F32), pltpu.SemaphoreType.DMA(())]
    return pl.pallas_call(
        functools.partial(_proj_kernel, prepare),
        grid=(b // nb, s // tm),
        in_specs=[
            pl.BlockSpec((nb, tm, D_MODEL), lambda bi, i: (bi, i, 0)),
            pl.BlockSpec((nb, 8, CONV_W), lambda bi, i: (bi, 0, 0)),
            full((1, D_MODEL)),
            pl.BlockSpec(memory_space=pl.ANY) if prepare else full((D_MODEL, PROJ_PAD)),
            full((1, LANES)),
            full((8, CONV_W)),
            full((1, CONV_W)),
        ],
        out_specs=out_specs,
        out_shape=out_shape,
        scratch_shapes=scratch,
        compiler_params=pltpu.CompilerParams(
            dimension_semantics=("arbitrary", "arbitrary"), vmem_limit_bytes=VMEM_LIMIT),
        name="proj",
    )(x, state, gm, w, bias_f, cw, gc)


ATT_BLK = 256
PV_CHUNK = 256


def _attn_kernel(qa_ref, ka_ref, v_ref, wgu_ref, wdn_ref, o_ref, wgu_bf_ref, wdn_bf_ref, sbuf):
    wgu_bf_ref[...] = wgu_ref[...].astype(BF16)
    wdn_bf_ref[...] = wdn_ref[...].astype(BF16)
    blk = ATT_BLK
    vt = v_ref[0].T.astype(BF16)
    r = lax.broadcasted_iota(jnp.int32, (blk, blk), 0)
    c = lax.broadcasted_iota(jnp.int32, (blk, blk), 1)
    causal = r <= c
    units = [(iq, hh) for iq in range(ka_ref.shape[2] // blk) for hh in range(2)]

    def scores(u):
        iq, hh = units[u]
        n = (iq + 1) * blk
        sbuf[u % nbuf, 0:n, :] = _dot_nt(ka_ref[0, hh, 0:n, :], qa_ref[0, hh, iq * blk:n, :])

    nbuf = sbuf.shape[0]
    for u in range(nbuf - 1):
        scores(u)
    outs = []
    for u, (iq, hh) in enumerate(units):
        if u + nbuf - 1 < len(units):
            scores(u + nbuf - 1)
        noff = iq * blk
        rows = slice(hh * HEAD_DIM, (hh + 1) * HEAD_DIM)
        s_d = jnp.where(causal, sbuf[u % nbuf, noff:noff + blk, :], -jnp.inf)
        m = jnp.max(s_d, axis=0, keepdims=True)
        if iq > 0:
            m = jnp.maximum(m, jnp.max(sbuf[u % nbuf, 0:noff, :], axis=0, keepdims=True))
        p_d = jnp.exp2(s_d - m)
        l = jnp.sum(p_d, axis=0, keepdims=True)
        acc = _dot(vt[rows, noff:noff + blk], p_d.astype(BF16))
        for c0 in range(0, noff, PV_CHUNK):
            c1 = min(c0 + PV_CHUNK, noff)
            p_o = jnp.exp2(sbuf[u % nbuf, c0:c1, :] - m)
            l = l + jnp.sum(p_o, axis=0, keepdims=True)
            acc = acc + _dot(vt[rows, c0:c1], p_o.astype(BF16))
        outs.append(acc / l)
        if hh == 1:
            o_ref[0, noff:noff + blk, :] = jnp.concatenate(outs, axis=0).T
            outs = []


def _attention(qa, ka, v, wgu, wdn):
    b, _, s, _ = qa.shape
    pairs = HEADS // 2
    ne = wgu.shape[0] // (b * pairs)
    assert ne * b * pairs == wgu.shape[0] == wdn.shape[0]
    wspec = lambda w: pl.BlockSpec((ne,) + w.shape[1:], lambda bi, p: (bi * pairs + p, 0, 0))
    return pl.pallas_call(
        _attn_kernel,
        grid=(b, pairs),
        in_specs=[
            pl.BlockSpec((1, 2, s, LANES), lambda bi, p: (bi, p, 0, 0)),
            pl.BlockSpec((1, 2, s, LANES), lambda bi, p: (bi, p, 0, 0)),
            pl.BlockSpec((1, s, LANES), lambda bi, p: (bi, 0, p)),
            wspec(wgu),
            wspec(wdn),
        ],
        out_specs=[pl.BlockSpec((1, s, LANES), lambda bi, p: (bi, 0, p)), wspec(wgu), wspec(wdn)],
        out_shape=[jax.ShapeDtypeStruct((b, s, ATTN_W), F32),
                   jax.ShapeDtypeStruct(wgu.shape, BF16), jax.ShapeDtypeStruct(wdn.shape, BF16)],
        scratch_shapes=[pltpu.VMEM((4, s, ATT_BLK), F32)],
        compiler_params=pltpu.CompilerParams(
            dimension_semantics=("arbitrary", "arbitrary"), vmem_limit_bytes=VMEM_LIMIT),
        name="attention",
    )(qa, ka, v, wgu, wdn)


Q_PAD = 128


def _cumsum_lanes(x):
    n = x.shape[1]
    r = lax.broadcasted_iota(jnp.int32, (n, n), 0)
    c = lax.broadcasted_iota(jnp.int32, (n, n), 1)
    tri = jnp.where(r <= c, 1.0, 0.0).astype(BF16)
    terms = jnp.concatenate([a.astype(F32) for a in _split3(x)], axis=0).astype(BF16)
    parts = _dot(terms, tri)
    rows = x.shape[0]
    return parts[0:rows] + parts[rows:2 * rows] + parts[2 * rows:]


def _attn_cache_kernel(qa_ref, ka_ref, v_ref, ckt_ref, cvt_ref, clft_ref, o_ref):
    nb, _, t, _ = qa_ref.shape
    plen = ckt_ref.shape[2]
    lane = lax.broadcasted_iota(jnp.int32, (t, LANES), 1)
    rq = lax.broadcasted_iota(jnp.int32, (t, Q_PAD), 0)
    ck = lax.broadcasted_iota(jnp.int32, (t, Q_PAD), 1)
    new_ok = ck <= rq

    for bb in range(nb):
        cinc = _cumsum_lanes(clft_ref[bb])
        bias = (cinc - cinc[:, plen - 1:plen]) * (-LOG2E)
        pairs = []
        for p in range(HEADS // 2):
            rows = slice(p * LANES, (p + 1) * LANES)
            kt = ckt_ref[bb, rows, :].astype(BF16)
            vt = cvt_ref[bb, rows, :].astype(BF16)
            vn = _pad_rows(v_ref[bb][:, rows], Q_PAD).astype(BF16)
            res = []
            for hh in range(2):
                h = 2 * p + hh
                qa = qa_ref[bb, h]
                qh = jnp.where(lane < HEAD_DIM, qa.astype(F32), 0.0)
                qh = (pltpu.roll(qh, HEAD_DIM, 1) if hh else qh).astype(BF16)
                s_past = _dot(qh, kt) + bias[h:h + 1, :]
                s_new = jnp.where(new_ok, _dot_nt(qa, _pad_rows(ka_ref[bb, h], Q_PAD)), -jnp.inf)
                m = jnp.maximum(jnp.max(s_past, axis=1, keepdims=True), jnp.max(s_new, axis=1, keepdims=True))
                p_past = jnp.exp2(s_past - m)
                p_new = jnp.exp2(s_new - m)
                l = jnp.sum(p_past, axis=1, keepdims=True) + jnp.sum(p_new, axis=1, keepdims=True)
                acc = _dot_nt(p_past.astype(BF16), vt) + _dot(p_new.astype(BF16), vn)
                res.append(acc / l)
            pairs.append(jnp.where(lane < HEAD_DIM, res[0], res[1]))
        o_ref[bb] = jnp.concatenate(pairs, axis=1)


def _attention_cache(qa, ka, v, ckt, cvt, clft, nb):
    b, _, t, _ = qa.shape
    plen = ckt.shape[2]
    return pl.pallas_call(
        _attn_cache_kernel,
        grid=(b // nb,),
        in_specs=[
            pl.BlockSpec((nb, HEADS, t, LANES), lambda bi: (bi, 0, 0, 0)),
            pl.BlockSpec((nb, HEADS, t, LANES), lambda bi: (bi, 0, 0, 0)),
            pl.BlockSpec((nb, t, ATTN_W), lambda bi: (bi, 0, 0)),
            pl.BlockSpec((nb, ATTN_W, plen), lambda bi: (bi, 0, 0)),
            pl.BlockSpec((nb, ATTN_W, plen), lambda bi: (bi, 0, 0)),
            pl.BlockSpec((nb, HEADS, plen), lambda bi: (bi, 0, 0)),
        ],
        out_specs=pl.BlockSpec((nb, t, ATTN_W), lambda bi: (bi, 0, 0)),
        out_shape=jax.ShapeDtypeStruct((b, t, ATTN_W), F32),
        compiler_params=pltpu.CompilerParams(
            dimension_semantics=("arbitrary",), vmem_limit_bytes=VMEM_LIMIT),
        name="attention_cache",
    )(qa, ka, v, ckt, cvt, clft)


def _mix_kernel(et, x_ref, o_ref, mb_ref, wo_ref, ga_ref, gf_ref, wr_ref, br_ref,
                rows_ref, slot_ref, cnt_ref, carry):
    i = pl.program_id(0)
    tm = slot_ref.shape[2]
    nsub = x_ref.shape[0] // tm
    row = lax.broadcasted_iota(jnp.int32, (EPG, tm), 0)
    row2 = lax.broadcasted_iota(jnp.int32, (2 * EPG, tm), 0)
    tok = lax.broadcasted_iota(jnp.int32, (2 * EPG, tm), 1)
    r2 = lax.broadcasted_iota(jnp.int32, (tm, tm), 0)
    c2 = lax.broadcasted_iota(jnp.int32, (tm, tm), 1)
    earlier = jnp.where(r2 < c2, 1.0, 0.0).astype(BF16)

    @pl.when(i == 0)
    def _():
        carry[...] = jnp.zeros_like(carry)

    total = carry[:, 0:1]
    for j in range(nsub):
        rs = slice(j * tm, (j + 1) * tm)
        oa = _rms(o_ref[rs, :], ga_ref[...]).astype(BF16)
        merged = jnp.concatenate([oa, mb_ref[rs, :]], axis=1)
        x2 = x_ref[rs, :] + _dot(merged, wo_ref[...])
        rows_ref[rs, 0:D_MODEL] = x2
        xb = _rms(x2, gf_ref[...]).astype(BF16)

        lt = (_dot(xb, wr_ref[...]) + br_ref[...]).T
        lg = jnp.where(row < N_GROUPS, lt[0:EPG], -jnp.inf)
        eg = jnp.exp(lg - jnp.max(lg, axis=0, keepdims=True))
        pg = eg / jnp.sum(eg, axis=0, keepdims=True)
        pg_top = jnp.max(pg, axis=0, keepdims=True)
        g_top = jnp.min(jnp.where(pg == pg_top, row, EPG), axis=0, keepdims=True)

        le = lt[EPG:2 * EPG]
        for g in range(1, N_GROUPS):
            le = jnp.where(g_top == g, lt[EPG * (g + 1):EPG * (g + 2)], le)
        ee = jnp.exp(le - jnp.max(le, axis=0, keepdims=True))
        pe = ee / jnp.sum(ee, axis=0, keepdims=True)
        v1 = jnp.max(pe, axis=0, keepdims=True)
        i1 = jnp.min(jnp.where(pe == v1, row, EPG), axis=0, keepdims=True)
        pe2 = jnp.where(row == i1, -1.0, pe)
        v2 = jnp.max(pe2, axis=0, keepdims=True)
        i2 = jnp.min(jnp.where(pe2 == v2, row, EPG), axis=0, keepdims=True)
        den = v1 + v2
        wg = jnp.where(row == i1, v1 / den * pg_top, 0.0) + jnp.where(row == i2, v2 / den * pg_top, 0.0)
        meta = jnp.where(row2 == META_ID, ((i * nsub + j) * tm + tok).astype(F32),
                         jnp.concatenate([wg, jnp.zeros((EPG, tm), F32)], axis=0))
        rows_ref[rs, D_MODEL:] = jnp.concatenate([meta, jnp.zeros((LANES - 2 * EPG, tm), F32)], axis=0).T

        onehot = jnp.where(row == g_top, 1.0, 0.0)
        rank = jnp.sum(onehot * (_dot(onehot.astype(BF16), earlier) + total), axis=0, keepdims=True)
        slot_ref[i * nsub + j] = g_top * (1 << GROUP_SHIFT) + rank.astype(jnp.int32)
        total = total + jnp.sum(onehot, axis=1, keepdims=True)

    carry[...] = jnp.broadcast_to(total, carry.shape)
    cnt_ref[...] = carry[...].astype(jnp.int32)

    @pl.when(i == pl.num_programs(0) - 1)
    def _():
        padded = jnp.floor((total + (et - 1)) * (1.0 / et)) * et
        o1 = padded[0:1]
        o2 = o1 + padded[1:2]
        o3 = o2 + padded[2:3]
        code = slot_ref[...]
        g = code >> GROUP_SHIFT
        first = jnp.where(g == 0, 0.0, jnp.where(g == 1, o1, jnp.where(g == 2, o2, o3)))
        slot_ref[...] = first.astype(jnp.int32) + (code & ((1 << GROUP_SHIFT) - 1))


def _mix(x, o, mb, wo, ga, gf, wr, br, tm, nsub):
    n = x.shape[0]
    assert n < (1 << GROUP_SHIFT)
    tb = tm * nsub
    full = lambda shape: pl.BlockSpec(shape, lambda i: (0,) * len(shape))
    return pl.pallas_call(
        functools.partial(_mix_kernel, _tile_rows(n)),
        grid=(n // tb,),
        in_specs=[
            pl.BlockSpec((tb, D_MODEL), lambda i: (i, 0)),
            pl.BlockSpec((tb, ATTN_W), lambda i: (i, 0)),
            pl.BlockSpec((tb, CONV_W), lambda i: (i, 0)),
            full((D_MODEL, D_MODEL)),
            full((1, ATTN_W)),
            full((1, D_MODEL)),
            full((D_MODEL, LANES)),
            full((1, LANES)),
        ],
        out_specs=[
            pl.BlockSpec((tb, ROW_W), lambda i: (i, 0)),
            full((n // tm, 1, tm)),
            full((EPG, LANES)),
        ],
        out_shape=[
            jax.ShapeDtypeStruct((n, ROW_W), F32),
            jax.ShapeDtypeStruct((n // tm, 1, tm), jnp.int32),
            jax.ShapeDtypeStruct((EPG, LANES), jnp.int32),
        ],
        scratch_shapes=[pltpu.VMEM((EPG, LANES), F32)],
        compiler_params=pltpu.CompilerParams(
            dimension_semantics=("arbitrary",), vmem_limit_bytes=VMEM_LIMIT),
        name="mix",
    )(x, o, mb, wo, ga, gf, wr, br)


def _sorted_tiles(n):
    return n // _tile_rows(n) + N_GROUPS


def _group_tiles(cnt_ref, et):
    nt = [(cnt_ref[g] + (et - 1)) >> (et.bit_length() - 1) for g in range(N_GROUPS)]
    s1 = nt[0]
    s2 = s1 + nt[1]
    s3 = s2 + nt[2]
    return s1, s2, s3, s3 + nt[3]


def _scatter_kernel(cnt_ref, rows_ref, slot_ref, xs_ref, zeros, sem, zsem):
    tm = rows_ref.shape[0]
    et = zeros.shape[0]

    @pl.when(pl.program_id(0) == 0)
    def _():
        zeros[...] = jnp.zeros_like(zeros)
        s1, s2, s3, tot = _group_tiles(cnt_ref, et)
        ends = (s1, s2, s3, tot)

        def fill(tile):
            return pltpu.make_async_copy(zeros, xs_ref.at[pl.ds(tile * et, et), :], zsem)

        for g in range(N_GROUPS):
            partial = (cnt_ref[g] & (et - 1)) != 0

            @pl.when(partial)
            def _():
                fill(ends[g] - 1).start()
                fill(ends[g] - 1).wait()

        def fill_tail(tile, c):
            fill(tile).start()
            fill(tile).wait()
            return c

        lax.fori_loop(tot, xs_ref.shape[0] // et, fill_tail, 0)

    copy_of_row = lambda r: pltpu.make_async_copy(
        rows_ref.at[pl.ds(r, 1), :], xs_ref.at[pl.ds(slot_ref[0, 0, r], 1), :], sem)
    for r in range(tm):
        copy_of_row(r).start(priority=r % 2)
    for r in range(tm):
        copy_of_row(r).wait()


def _scatter(cnt, rows, slots, tm):
    n = rows.shape[0]
    et = _tile_rows(n)
    return pl.pallas_call(
        _scatter_kernel,
        grid_spec=pltpu.PrefetchScalarGridSpec(
            num_scalar_prefetch=1,
            grid=(n // tm,),
            in_specs=[
                pl.BlockSpec((tm, ROW_W), lambda i, c: (i, 0)),
                pl.BlockSpec((1, 1, tm), lambda i, c: (i, 0, 0), memory_space=pltpu.SMEM),
            ],
            out_specs=pl.BlockSpec(memory_space=pl.ANY),
            scratch_shapes=[pltpu.VMEM((et, ROW_W), F32),
                            pltpu.SemaphoreType.DMA(()), pltpu.SemaphoreType.DMA(())],
        ),
        out_shape=jax.ShapeDtypeStruct((_sorted_tiles(n) * et, ROW_W), F32),
        compiler_params=pltpu.CompilerParams(
            dimension_semantics=("arbitrary",), vmem_limit_bytes=VMEM_LIMIT),
        name="scatter",
    )(cnt, rows, slots)


def _tile_of(t, cnt_ref, et):
    s1, s2, s3, tot = _group_tiles(cnt_ref, et)
    tc = jnp.minimum(t, tot - 1)
    g = (tc >= s1).astype(jnp.int32) + (tc >= s2).astype(jnp.int32) + (tc >= s3).astype(jnp.int32)
    first = jnp.where(g == 0, 0, jnp.where(g == 1, s1, jnp.where(g == 2, s2, s3)))
    cnt = jnp.where(g == 0, cnt_ref[0], jnp.where(g == 1, cnt_ref[1], jnp.where(g == 2, cnt_ref[2], cnt_ref[3])))
    return tc, g, jnp.where(t < tot, cnt - (tc - first) * et, 0)


def _experts_kernel(cnt_ref, xs_ref, wgu_ref, wdn_ref, gf_ref, gn_ref, y_hbm, yb0, yb1, idv, ids, pend, sem, isem):
    t = pl.program_id(0)
    et = xs_ref.shape[0]
    _, _, cur_valid = _tile_of(t, cnt_ref, et)
    prev_valid = jnp.where(t > 0, jnp.clip(_tile_of(t - 1, cnt_ref, et)[2], 0, et), 0)
    fast = (prev_valid == et) & (cur_valid > 0)

    @pl.when(t == 0)
    def _():
        pend[0] = 0

    def drain(k):
        one = pltpu.make_async_copy(yb0.at[pl.ds(0, 1), :], y_hbm.at[pl.ds(0, 1), :], sem.at[k])

        @pl.when(pend[0] == et)
        def _():
            for _ in range(et):
                one.wait()

        @pl.when(pend[0] != et)
        def _():
            def body(r, c):
                one.wait()
                return c
            lax.fori_loop(0, pend[0], body, 0)

    def compute(ybuf, par):
        idv[par:par + 1, :] = xs_ref[:, D_MODEL:].T[META_ID:META_ID + 1, :].astype(jnp.int32)
        pltpu.make_async_copy(idv.at[par], ids.at[par], isem).start()
        tot = xs_ref[:, 0:D_MODEL]
        xb = _rms(tot, gf_ref[...]).astype(BF16)
        wts = xs_ref[:, D_MODEL:]
        for e in range(EPG):
            gu = _dot(xb, wgu_ref[0, e])
            gate = gu[:, :D_EXPERT]
            hmid = gate * jax.nn.sigmoid(gate) * gu[:, D_EXPERT:] * wts[:, e:e + 1]
            tot = tot + _dot(hmid.astype(BF16), wdn_ref[0, e])
        y = _rms(tot, gn_ref[...])
        drain(1 - par)
        ybuf[...] = y

    def ids_wait(par):
        pltpu.make_async_copy(idv.at[par], ids.at[par], isem).wait()

    def row_copy(ybuf, par, r):
        return pltpu.make_async_copy(ybuf.at[pl.ds(r, 1), :], y_hbm.at[pl.ds(ids[1 - par, r], 1), :], sem.at[par])

    for par, (ycur, yprev) in enumerate(((yb0, yb1), (yb1, yb0))):
        @pl.when(((t % 2) == par) & fast)
        def _():
            ids_wait(1 - par)
            for r in range(et):
                row_copy(yprev, par, r).start(priority=r % 2)
            compute(ycur, par)

        @pl.when(((t % 2) == par) & jnp.logical_not(fast))
        def _():
            @pl.when(prev_valid > 0)
            def _():
                ids_wait(1 - par)

                def start(r, c):
                    row_copy(yprev, par, r).start()
                    return c
                lax.fori_loop(0, prev_valid, start, 0)

            @pl.when(cur_valid > 0)
            def _():
                compute(ycur, par)

            @pl.when(cur_valid <= 0)
            def _():
                drain(1 - par)

        @pl.when((t % 2) == par)
        def _():
            pend[0] = prev_valid

            @pl.when(t == pl.num_programs(0) - 1)
            def _():
                drain(par)


def _experts(cnt, xs, wgu, wdn, gf, gn, n):
    et = _tile_rows(n)
    tiles = xs.shape[0] // et
    grp = lambda t, c: (_tile_of(t, c, et)[1], 0, 0, 0)
    return pl.pallas_call(
        _experts_kernel,
        grid_spec=pltpu.PrefetchScalarGridSpec(
            num_scalar_prefetch=1,
            grid=(tiles + 1,),
            in_specs=[
                pl.BlockSpec((et, ROW_W), lambda t, c: (_tile_of(t, c, et)[0], 0)),
                pl.BlockSpec((1, EPG, D_MODEL, 2 * D_EXPERT), grp),
                pl.BlockSpec((1, EPG, D_EXPERT, D_MODEL), grp),
                pl.BlockSpec((1, D_MODEL), lambda t, c: (0, 0)),
                pl.BlockSpec((1, D_MODEL), lambda t, c: (0, 0)),
            ],
            out_specs=pl.BlockSpec(memory_space=pl.ANY),
            scratch_shapes=[pltpu.VMEM((et, D_MODEL), F32), pltpu.VMEM((et, D_MODEL), F32),
                            pltpu.VMEM((8, et), jnp.int32), pltpu.SMEM((2, et), jnp.int32),
                            pltpu.SMEM((1,), jnp.int32), pltpu.SemaphoreType.DMA((2,)), pltpu.SemaphoreType.DMA(())],
        ),
        out_shape=jax.ShapeDtypeStruct((n, D_MODEL), F32),
        compiler_params=pltpu.CompilerParams(
            dimension_semantics=("arbitrary",), vmem_limit_bytes=VMEM_LIMIT),
        name="experts",
    )(cnt, xs, wgu, wdn, gf, gn)


def _forget_lanes(f):
    z = lambda n: jnp.zeros(f.shape[:-1] + (n,), f.dtype)
    return jnp.concatenate([f, z(F_HI - HEADS), f, f, f, z(LANES - F_LO - HEADS)], axis=-1)


def _trunk(x, attend, state, w, tm_proj, nb_proj, tm_tok, nsub_tok):
    b, s, _ = x.shape
    k, v, lf, qa, ka, mb, ncv, *built = _proj(
        x, state, w["gm"], w["w_proj"], w["bias_f"], w["cw"], w["gc"], tm_proj, nb_proj)
    if built:
        w["w_proj"] = built[0]
    o = attend(qa, ka, v)
    n = b * s
    rows, slots, cnt = _mix(x.reshape(n, D_MODEL), o.reshape(n, ATTN_W), mb.reshape(n, CONV_W),
                            w["wo"], w["ga"], w["gf"], w["wr"], w["br"], tm_tok, nsub_tok)
    cnt = cnt[:, 0]
    y = _experts(cnt, _scatter(cnt, rows, slots, tm_tok), w["wgu"], w["wdn"], w["gf"], w["gn"], n)
    return (y.reshape(b, s, D_MODEL), k.reshape(1, b, s, HEADS, HEAD_DIM), v.reshape(1, b, s, HEADS, HEAD_DIM),
            lf.transpose(0, 2, 1).reshape(1, b, s, HEADS), ncv.reshape(1, b, 2, CONV_W))


def kernel(x_prompt, x_sample, cache_k, cache_v, cache_logf, state_conv, norm_mix_g, w_in, b_forget, conv_w, norm_attn_g, norm_conv_g, w_out, norm_ffn_g, w_router_group, b_router_group, w_router_expert, b_router_expert, w_expert_gate_up, w_expert_down, norm_final_g):
    assert w_in.shape[0] == 1, "single-layer trunk"
    wr = jnp.concatenate(
        [w_router_group[0], jnp.zeros((D_MODEL, EPG - N_GROUPS), F32),
         w_router_expert[0].transpose(1, 0, 2).reshape(D_MODEL, N_GROUPS * EPG),
         jnp.zeros((D_MODEL, LANES - EPG - N_GROUPS * EPG), F32)], axis=1).astype(BF16)
    br = jnp.concatenate(
        [b_router_group[0], jnp.zeros((EPG - N_GROUPS,), F32), b_router_expert[0].reshape(-1),
         jnp.zeros((LANES - EPG - N_GROUPS * EPG,), F32)]).reshape(1, LANES)
    w = dict(
        gm=norm_mix_g[0].reshape(1, D_MODEL), w_proj=w_in[0].T, bias_f=_forget_lanes(b_forget[0]).reshape(1, LANES),
        cw=jnp.concatenate([conv_w[0], jnp.zeros((8 - conv_w.shape[1], CONV_W), F32)], axis=0),
        gc=norm_conv_g[0].reshape(1, CONV_W), wo=w_out[0].astype(BF16), ga=norm_attn_g[0].reshape(1, ATTN_W),
        gf=norm_ffn_g[0].reshape(1, D_MODEL), wr=wr, br=br,
        wgu=w_expert_gate_up[0], wdn=w_expert_down[0],
        gn=norm_final_g.reshape(1, D_MODEL))

    bp, sp, _ = x_prompt.shape
    bs, ss, _ = x_sample.shape
    plen = cache_k.shape[2]

    def attend_p(qa, ka, v):
        wgu, wdn = w["wgu"], w["wdn"]
        o, wgu_bf, wdn_bf = _attention(qa, ka, v, wgu.reshape((-1,) + wgu.shape[2:]), wdn.reshape((-1,) + wdn.shape[2:]))
        w["wgu"], w["wdn"] = wgu_bf.reshape(wgu.shape), wdn_bf.reshape(wdn.shape)
        return o

    yp, kp, vp, lfp, cvp = _trunk(x_prompt, attend_p, jnp.zeros((bp, 8, CONV_W), F32), w, 512, 2, 512, 2)

    ckt = cache_k[0].transpose(0, 2, 3, 1).reshape(bs, ATTN_W, plen)
    cvt = cache_v[0].transpose(0, 2, 3, 1).reshape(bs, ATTN_W, plen)
    clft = cache_logf[0].transpose(0, 2, 1)
    st = jnp.concatenate([jnp.zeros((bs, 6, CONV_W), F32), state_conv[0]], axis=1)
    attend_s = lambda qa, ka, v: _attention_cache(qa, ka, v, ckt, cvt, clft, 2)
    ys, ks, vs, lfs, cvs = _trunk(x_sample, attend_s, st, w, ss, bs, bs * ss, 1)
    return (yp, ys, kp, vp, lfp, cvp, ks, vs, lfs, cvs)
```

```python
import functools
import math

import jax
import jax.numpy as jnp
from jax import lax
from jax.experimental import pallas as pl
from jax.experimental.pallas import tpu as pltpu

F32 = jnp.float32
BF16 = jnp.bfloat16

D_MODEL = 1024
HEADS = 8
HEAD_DIM = 64
ATTN_W = HEADS * HEAD_DIM
CONV_W = 512
N_GROUPS = 4
EPG = 8
D_EXPERT = 256
EPS = 1e-6
LOG2E = math.log2(math.e)
Q_SCALE = HEAD_DIM ** -0.5 * LOG2E

LANES = 128
COL_Q, COL_K, COL_V, COL_F, COL_BG, COL_CG, COL_H = 0, 512, 1024, 1536, 1664, 2176, 2688
PROJ_PAD = 3200
W_SECTIONS = ((0, COL_Q), (512, COL_K), (1024, COL_V), (1544, COL_BG), (2056, COL_CG), (2568, COL_H))
W_F_ROW = 1536
F_OUT, F_HI, F_MID, F_LO = 0, 64, 72, 80

GROUP_SHIFT = 20
ROW_W = D_MODEL + LANES
META_ID = EPG


def _tile_rows(n):
    return 512 if n >= 4096 else 128

VMEM_LIMIT = 60 * 1024 * 1024


def _dot(a, b):
    return jnp.dot(a, b, preferred_element_type=F32)


def _dot_nt(a, b):
    return lax.dot_general(a, b, (((1,), (1,)), ((), ())), preferred_element_type=F32)


def _rms(x, g):
    return x * lax.rsqrt(jnp.mean(x * x, axis=-1, keepdims=True) + EPS) * g


def _log_sigmoid(x):
    return jnp.minimum(x, 0.0) - jnp.log(1.0 + jnp.exp(-jnp.abs(x)))


def _pad_rows(x, n):
    return jnp.concatenate([x, jnp.zeros((n - x.shape[0], x.shape[1]), x.dtype)], axis=0)


def _split3(x):
    a1 = x.astype(BF16)
    r1 = x - a1.astype(F32)
    a2 = r1.astype(BF16)
    a3 = (r1 - a2.astype(F32)).astype(BF16)
    return a1, a2, a3


def _cumsum_rows(x):
    n = x.shape[0]
    r = lax.broadcasted_iota(jnp.int32, (n, n), 0)
    c = lax.broadcasted_iota(jnp.int32, (n, n), 1)
    tri = jnp.where(r >= c, 1.0, 0.0).astype(BF16)
    parts = _dot(tri, jnp.concatenate(_split3(x), axis=1))
    return parts[:, :LANES] + parts[:, LANES:2 * LANES] + parts[:, 2 * LANES:]


def _key_bias_lanes(cneg):
    lane = lax.broadcasted_iota(jnp.int32, cneg.shape, 1)
    hi = cneg.astype(BF16).astype(F32)
    r1 = cneg - hi
    mid = r1.astype(BF16).astype(F32)
    lo = r1 - mid
    out = jnp.where(lane < F_MID, hi, jnp.where(lane < F_LO, mid, lo))
    return jnp.where((lane >= F_HI) & (lane < F_LO + HEADS), out, 0.0)


def _head_block(x, h):
    blk = x[:, (h // 2) * LANES:(h // 2 + 1) * LANES]
    return pltpu.roll(blk, HEAD_DIM, 1) if h % 2 else blk


def _key_operand(k, bias_lanes, h):
    lane = lax.broadcasted_iota(jnp.int32, bias_lanes.shape, 1)
    return jnp.where(lane < HEAD_DIM, _head_block(k, h), bias_lanes).astype(BF16)


def _prepare_weight(wt_hbm, w_ref, stage, sem):
    for r0, c0 in W_SECTIONS:
        cp = pltpu.make_async_copy(wt_hbm.at[pl.ds(r0, ATTN_W), :], stage, sem)
        cp.start()
        cp.wait()
        w_ref[:, c0:c0 + ATTN_W] = stage[...].T.astype(BF16)
    cp = pltpu.make_async_copy(wt_hbm.at[pl.ds(W_F_ROW, HEADS), :], stage.at[pl.ds(0, HEADS), :], sem)
    cp.start()
    cp.wait()
    f = stage[0:HEADS, :]
    z = lambda n: jnp.zeros((n, D_MODEL), F32)
    fblk = jnp.concatenate([f, z(F_HI - HEADS), f, f, f, z(LANES - F_LO - HEADS)], axis=0)
    w_ref[:, COL_F:COL_F + LANES] = fblk.T.astype(BF16)


def _proj_kernel(prepare, x_ref, st_ref, gm_ref, w_in_ref, bf_ref, cw_ref, gc_ref,
                 k_ref, v_ref, lf_ref, qa_ref, ka_ref, mb_ref, ncv_ref, *rest):
    i = pl.program_id(1)
    nb, tm, _ = x_ref.shape
    if prepare:
        w_ref, ubuf, ccar, stage, sem = rest

        @pl.when((pl.program_id(0) == 0) & (i == 0))
        def _():
            _prepare_weight(w_in_ref, w_ref, stage, sem)
    else:
        w_ref = w_in_ref
        ubuf, ccar = rest

    @pl.when(i == 0)
    def _():
        ccar[...] = jnp.zeros_like(ccar)
        ubuf[:, 0:8, :] = st_ref[...]

    @pl.when(i > 0)
    def _():
        ubuf[:, 0:8, :] = ubuf[:, tm:tm + 8, :]

    lane = lax.broadcasted_iota(jnp.int32, (tm, LANES), 1)
    pad = (-tm) % LANES
    xn = _rms(x_ref[...].reshape(nb * tm, D_MODEL), gm_ref[...]).astype(BF16)
    sections = {c0: _dot(xn, w_ref[:, c0:c0 + n]) for c0, n in
                ((COL_F, LANES), (COL_K, ATTN_W), (COL_Q, ATTN_W), (COL_V, ATTN_W), (COL_CG, CONV_W),
                 (COL_H, CONV_W), (COL_BG, CONV_W))}
    for bb in range(nb):
        proj = lambda c0, n: sections[c0][bb * tm:(bb + 1) * tm, :]

        lfb = _log_sigmoid(proj(COL_F, LANES) + bf_ref[...])
        lf_ref[bb] = (_pad_rows(lfb, tm + pad) if pad else lfb).T[F_OUT:F_OUT + HEADS, 0:tm]
        c = _cumsum_rows(lfb) + ccar[bb, 0:1, :]
        ccar[bb] = jnp.broadcast_to(c[tm - 1:tm, :], ccar.shape[1:])
        bias_lanes = _key_bias_lanes(c * (-LOG2E))
        k = proj(COL_K, ATTN_W)
        k_ref[bb] = k
        for h in range(HEADS):
            ka_ref[bb, h] = _key_operand(k, bias_lanes, h)
        q = proj(COL_Q, ATTN_W)
        for h in range(HEADS):
            sel = jnp.where((lane == F_HI + h) | (lane == F_MID + h) | (lane == F_LO + h), 1.0, 0.0)
            qa_ref[bb, h] = jnp.where(lane < HEAD_DIM, _head_block(q, h) * Q_SCALE, sel).astype(BF16)
        v_ref[bb] = proj(COL_V, ATTN_W)

        u = proj(COL_CG, CONV_W) * proj(COL_H, CONV_W)
        ubuf[bb, 8:8 + tm, :] = u
        y = (cw_ref[0:1, :] * ubuf[bb, 6:6 + tm, :] + cw_ref[1:2, :] * ubuf[bb, 7:7 + tm, :]
             + cw_ref[2:3, :] * u)
        ob = proj(COL_BG, CONV_W) * y
        mb_ref[bb] = _rms(ob, gc_ref[...]).astype(BF16)
        ncv_ref[bb] = u[tm - 2:tm, :]


def _proj(x, state, gm, w, bias_f, cw, gc, tm, nb):
    b, s, _ = x.shape
    prepare = w.dtype == F32
    full = lambda shape: pl.BlockSpec(shape, lambda bi, i: (0,) * len(shape))
    out_specs = [
        pl.BlockSpec((nb, tm, ATTN_W), lambda bi, i: (bi, i, 0)),
        pl.BlockSpec((nb, tm, ATTN_W), lambda bi, i: (bi, i, 0)),
        pl.BlockSpec((nb, HEADS, tm), lambda bi, i: (bi, 0, i)),
        pl.BlockSpec((nb, HEADS, tm, LANES), lambda bi, i: (bi, 0, i, 0)),
        pl.BlockSpec((nb, HEADS, tm, LANES), lambda bi, i: (bi, 0, i, 0)),
        pl.BlockSpec((nb, tm, CONV_W), lambda bi, i: (bi, i, 0)),
        pl.BlockSpec((nb, 2, CONV_W), lambda bi, i: (bi, 0, 0)),
    ]
    out_shape = [
        jax.ShapeDtypeStruct((b, s, ATTN_W), F32),
        jax.ShapeDtypeStruct((b, s, ATTN_W), F32),
        jax.ShapeDtypeStruct((b, HEADS, s), F32),
        jax.ShapeDtypeStruct((b, HEADS, s, LANES), BF16),
        jax.ShapeDtypeStruct((b, HEADS, s, LANES), BF16),
        jax.ShapeDtypeStruct((b, s, CONV_W), BF16),
        jax.ShapeDtypeStruct((b, 2, CONV_W), F32),
    ]
    scratch = [pltpu.VMEM((nb, tm + 8, CONV_W), F32), pltpu.VMEM((nb, 8, LANES), F32)]
    if prepare:
        out_specs.append(full((D_MODEL, PROJ_PAD)))
        out_shape.append(jax.ShapeDtypeStruct((D_MODEL, PROJ_PAD), BF16))
        scratch += [pltpu.VMEM((ATTN_W, D_MODEL), F32), pltpu.SemaphoreType.DMA(())]
    return pl.pallas_call(
        functools.partial(_proj_kernel, prepare),
        grid=(b // nb, s // tm),
        in_specs=[
            pl.BlockSpec((nb, tm, D_MODEL), lambda bi, i: (bi, i, 0)),
            pl.BlockSpec((nb, 8, CONV_W), lambda bi, i: (bi, 0, 0)),
            full((1, D_MODEL)),
            pl.BlockSpec(memory_space=pl.ANY) if prepare else full((D_MODEL, PROJ_PAD)),
            full((1, LANES)),
            full((8, CONV_W)),
            full((1, CONV_W)),
        ],
        out_specs=out_specs,
        out_shape=out_shape,
        scratch_shapes=scratch,
        compiler_params=pltpu.CompilerParams(
            dimension_semantics=("arbitrary", "arbitrary"), vmem_limit_bytes=VMEM_LIMIT),
        name="proj",
    )(x, state, gm, w, bias_f, cw, gc)


ATT_BLK = 256
PV_CHUNK = 256


def _attn_kernel(qa_ref, ka_ref, v_ref, wgu_ref, wdn_ref, o_ref, wgu_bf_ref, wdn_bf_ref, sbuf):
    wgu_bf_ref[...] = wgu_ref[...].astype(BF16)
    wdn_bf_ref[...] = wdn_ref[...].astype(BF16)
    blk = ATT_BLK
    vt = v_ref[0].T.astype(BF16)
    r = lax.broadcasted_iota(jnp.int32, (blk, blk), 0)
    c = lax.broadcasted_iota(jnp.int32, (blk, blk), 1)
    causal = r <= c
    units = [(iq, hh) for iq in range(ka_ref.shape[2] // blk) for hh in range(2)]

    def scores(u):
        iq, hh = units[u]
        n = (iq + 1) * blk
        sbuf[u % nbuf, 0:n, :] = _dot_nt(ka_ref[0, hh, 0:n, :], qa_ref[0, hh, iq * blk:n, :])

    nbuf = sbuf.shape[0]
    for u in range(nbuf - 1):
        scores(u)
    outs = []
    for u, (iq, hh) in enumerate(units):
        if u + nbuf - 1 < len(units):
            scores(u + nbuf - 1)
        noff = iq * blk
        rows = slice(hh * HEAD_DIM, (hh + 1) * HEAD_DIM)
        s_d = jnp.where(causal, sbuf[u % nbuf, noff:noff + blk, :], -jnp.inf)
        m = jnp.max(s_d, axis=0, keepdims=True)
        if iq > 0:
            m = jnp.maximum(m, jnp.max(sbuf[u % nbuf, 0:noff, :], axis=0, keepdims=True))
        p_d = jnp.exp2(s_d - m)
        l = jnp.sum(p_d, axis=0, keepdims=True)
        acc = _dot(vt[rows, noff:noff + blk], p_d.astype(BF16))
        for c0 in range(0, noff, PV_CHUNK):
            c1 = min(c0 + PV_CHUNK, noff)
            p_o = jnp.exp2(sbuf[u % nbuf, c0:c1, :] - m)
            l = l + jnp.sum(p_o, axis=0, keepdims=True)
            acc = acc + _dot(vt[rows, c0:c1], p_o.astype(BF16))
        outs.append(acc / l)
        if hh == 1:
            o_ref[0, noff:noff + blk, :] = jnp.concatenate(outs, axis=0).T
            outs = []


def _attention(qa, ka, v, wgu, wdn):
    b, _, s, _ = qa.shape
    pairs = HEADS // 2
    ne = wgu.shape[0] // (b * pairs)
    assert ne * b * pairs == wgu.shape[0] == wdn.shape[0]
    wspec = lambda w: pl.BlockSpec((ne,) + w.shape[1:], lambda bi, p: (bi * pairs + p, 0, 0))
    return pl.pallas_call(
        _attn_kernel,
        grid=(b, pairs),
        in_specs=[
            pl.BlockSpec((1, 2, s, LANES), lambda bi, p: (bi, p, 0, 0)),
            pl.BlockSpec((1, 2, s, LANES), lambda bi, p: (bi, p, 0, 0)),
            pl.BlockSpec((1, s, LANES), lambda bi, p: (bi, 0, p)),
            wspec(wgu),
            wspec(wdn),
        ],
        out_specs=[pl.BlockSpec((1, s, LANES), lambda bi, p: (bi, 0, p)), wspec(wgu), wspec(wdn)],
        out_shape=[jax.ShapeDtypeStruct((b, s, ATTN_W), F32),
                   jax.ShapeDtypeStruct(wgu.shape, BF16), jax.ShapeDtypeStruct(wdn.shape, BF16)],
        scratch_shapes=[pltpu.VMEM((4, s, ATT_BLK), F32)],
        compiler_params=pltpu.CompilerParams(
            dimension_semantics=("arbitrary", "arbitrary"), vmem_limit_bytes=VMEM_LIMIT),
        name="attention",
    )(qa, ka, v, wgu, wdn)


Q_PAD = 128


def _cumsum_lanes(x):
    n = x.shape[1]
    r = lax.broadcasted_iota(jnp.int32, (n, n), 0)
    c = lax.broadcasted_iota(jnp.int32, (n, n), 1)
    tri = jnp.where(r <= c, 1.0, 0.0).astype(BF16)
    terms = jnp.concatenate([a.astype(F32) for a in _split3(x)], axis=0).astype(BF16)
    parts = _dot(terms, tri)
    rows = x.shape[0]
    return parts[0:rows] + parts[rows:2 * rows] + parts[2 * rows:]


def _attn_cache_kernel(qa_ref, ka_ref, v_ref, ckt_ref, cvt_ref, clft_ref, o_ref):
    nb, _, t, _ = qa_ref.shape
    plen = ckt_ref.shape[2]
    lane = lax.broadcasted_iota(jnp.int32, (t, LANES), 1)
    rq = lax.broadcasted_iota(jnp.int32, (t, Q_PAD), 0)
    ck = lax.broadcasted_iota(jnp.int32, (t, Q_PAD), 1)
    new_ok = ck <= rq

    for bb in range(nb):
        cinc = _cumsum_lanes(clft_ref[bb])
        bias = (cinc - cinc[:, plen - 1:plen]) * (-LOG2E)
        pairs = []
        for p in range(HEADS // 2):
            rows = slice(p * LANES, (p + 1) * LANES)
            kt = ckt_ref[bb, rows, :].astype(BF16)
            vt = cvt_ref[bb, rows, :].astype(BF16)
            vn = _pad_rows(v_ref[bb][:, rows], Q_PAD).astype(BF16)
            res = []
            for hh in range(2):
                h = 2 * p + hh
                qa = qa_ref[bb, h]
                qh = jnp.where(lane < HEAD_DIM, qa.astype(F32), 0.0)
                qh = (pltpu.roll(qh, HEAD_DIM, 1) if hh else qh).astype(BF16)
                s_past = _dot(qh, kt) + bias[h:h + 1, :]
                s_new = jnp.where(new_ok, _dot_nt(qa, _pad_rows(ka_ref[bb, h], Q_PAD)), -jnp.inf)
                m = jnp.maximum(jnp.max(s_past, axis=1, keepdims=True), jnp.max(s_new, axis=1, keepdims=True))
                p_past = jnp.exp2(s_past - m)
                p_new = jnp.exp2(s_new - m)
                l = jnp.sum(p_past, axis=1, keepdims=True) + jnp.sum(p_new, axis=1, keepdims=True)
                acc = _dot_nt(p_past.astype(BF16), vt) + _dot(p_new.astype(BF16), vn)
                res.append(acc / l)
            pairs.append(jnp.where(lane < HEAD_DIM, res[0], res[1]))
        o_ref[bb] = jnp.concatenate(pairs, axis=1)


def _attention_cache(qa, ka, v, ckt, cvt, clft, nb):
    b, _, t, _ = qa.shape
    plen = ckt.shape[2]
    return pl.pallas_call(
        _attn_cache_kernel,
        grid=(b // nb,),
        in_specs=[
            pl.BlockSpec((nb, HEADS, t, LANES), lambda bi: (bi, 0, 0, 0)),
            pl.BlockSpec((nb, HEADS, t, LANES), lambda bi: (bi, 0, 0, 0)),
            pl.BlockSpec((nb, t, ATTN_W), lambda bi: (bi, 0, 0)),
            pl.BlockSpec((nb, ATTN_W, plen), lambda bi: (bi, 0, 0)),
            pl.BlockSpec((nb, ATTN_W, plen), lambda bi: (bi, 0, 0)),
            pl.BlockSpec((nb, HEADS, plen), lambda bi: (bi, 0, 0)),
        ],
        out_specs=pl.BlockSpec((nb, t, ATTN_W), lambda bi: (bi, 0, 0)),
        out_shape=jax.ShapeDtypeStruct((b, t, ATTN_W), F32),
        compiler_params=pltpu.CompilerParams(
            dimension_semantics=("arbitrary",), vmem_limit_bytes=VMEM_LIMIT),
        name="attention_cache",
    )(qa, ka, v, ckt, cvt, clft)


def _mix_kernel(et, x_ref, o_ref, mb_ref, wo_ref, ga_ref, gf_ref, wr_ref, br_ref,
                rows_ref, slot_ref, cnt_ref, carry):
    i = pl.program_id(0)
    tm = slot_ref.shape[2]
    nsub = x_ref.shape[0] // tm
    row = lax.broadcasted_iota(jnp.int32, (EPG, tm), 0)
    row2 = lax.broadcasted_iota(jnp.int32, (2 * EPG, tm), 0)
    tok = lax.broadcasted_iota(jnp.int32, (2 * EPG, tm), 1)
    r2 = lax.broadcasted_iota(jnp.int32, (tm, tm), 0)
    c2 = lax.broadcasted_iota(jnp.int32, (tm, tm), 1)
    earlier = jnp.where(r2 < c2, 1.0, 0.0).astype(BF16)

    @pl.when(i == 0)
    def _():
        carry[...] = jnp.zeros_like(carry)

    total = carry[:, 0:1]
    for j in range(nsub):
        rs = slice(j * tm, (j + 1) * tm)
        oa = _rms(o_ref[rs, :], ga_ref[...]).astype(BF16)
        merged = jnp.concatenate([oa, mb_ref[rs, :]], axis=1)
        x2 = x_ref[rs, :] + _dot(merged, wo_ref[...])
        rows_ref[rs, 0:D_MODEL] = x2
        xb = _rms(x2, gf_ref[...]).astype(BF16)

        lt = (_dot(xb, wr_ref[...]) + br_ref[...]).T
        lg = jnp.where(row < N_GROUPS, lt[0:EPG], -jnp.inf)
        eg = jnp.exp(lg - jnp.max(lg, axis=0, keepdims=True))
        pg = eg / jnp.sum(eg, axis=0, keepdims=True)
        pg_top = jnp.max(pg, axis=0, keepdims=True)
        g_top = jnp.min(jnp.where(pg == pg_top, row, EPG), axis=0, keepdims=True)

        le = lt[EPG:2 * EPG]
        for g in range(1, N_GROUPS):
            le = jnp.where(g_top == g, lt[EPG * (g + 1):EPG * (g + 2)], le)
        ee = jnp.exp(le - jnp.max(le, axis=0, keepdims=True))
        pe = ee / jnp.sum(ee, axis=0, keepdims=True)
        v1 = jnp.max(pe, axis=0, keepdims=True)
        i1 = jnp.min(jnp.where(pe == v1, row, EPG), axis=0, keepdims=True)
        pe2 = jnp.where(row == i1, -1.0, pe)
        v2 = jnp.max(pe2, axis=0, keepdims=True)
        i2 = jnp.min(jnp.where(pe2 == v2, row, EPG), axis=0, keepdims=True)
        den = v1 + v2
        wg = jnp.where(row == i1, v1 / den * pg_top, 0.0) + jnp.where(row == i2, v2 / den * pg_top, 0.0)
        meta = jnp.where(row2 == META_ID, ((i * nsub + j) * tm + tok).astype(F32),
                         jnp.concatenate([wg, jnp.zeros((EPG, tm), F32)], axis=0))
        rows_ref[rs, D_MODEL:] = jnp.concatenate([meta, jnp.zeros((LANES - 2 * EPG, tm), F32)], axis=0).T

        onehot = jnp.where(row == g_top, 1.0, 0.0)
        rank = jnp.sum(onehot * (_dot(onehot.astype(BF16), earlier) + total), axis=0, keepdims=True)
        slot_ref[i * nsub + j] = g_top * (1 << GROUP_SHIFT) + rank.astype(jnp.int32)
        total = total + jnp.sum(onehot, axis=1, keepdims=True)

    carry[...] = jnp.broadcast_to(total, carry.shape)
    cnt_ref[...] = carry[...].astype(jnp.int32)

    @pl.when(i == pl.num_programs(0) - 1)
    def _():
        padded = jnp.floor((total + (et - 1)) * (1.0 / et)) * et
        o1 = padded[0:1]
        o2 = o1 + padded[1:2]
        o3 = o2 + padded[2:3]
        code = slot_ref[...]
        g = code >> GROUP_SHIFT
        first = jnp.where(g == 0, 0.0, jnp.where(g == 1, o1, jnp.where(g == 2, o2, o3)))
        slot_ref[...] = first.astype(jnp.int32) + (code & ((1 << GROUP_SHIFT) - 1))


def _mix(x, o, mb, wo, ga, gf, wr, br, tm, nsub):
    n = x.shape[0]
    assert n < (1 << GROUP_SHIFT)
    tb = tm * nsub
    full = lambda shape: pl.BlockSpec(shape, lambda i: (0,) * len(shape))
    return pl.pallas_call(
        functools.partial(_mix_kernel, _tile_rows(n)),
        grid=(n // tb,),
        in_specs=[
            pl.BlockSpec((tb, D_MODEL), lambda i: (i, 0)),
            pl.BlockSpec((tb, ATTN_W), lambda i: (i, 0)),
            pl.BlockSpec((tb, CONV_W), lambda i: (i, 0)),
            full((D_MODEL, D_MODEL)),
            full((1, ATTN_W)),
            full((1, D_MODEL)),
            full((D_MODEL, LANES)),
            full((1, LANES)),
        ],
        out_specs=[
            pl.BlockSpec((tb, ROW_W), lambda i: (i, 0)),
            full((n // tm, 1, tm)),
            full((EPG, LANES)),
        ],
        out_shape=[
            jax.ShapeDtypeStruct((n, ROW_W), F32),
            jax.ShapeDtypeStruct((n // tm, 1, tm), jnp.int32),
            jax.ShapeDtypeStruct((EPG, LANES), jnp.int32),
        ],
        scratch_shapes=[pltpu.VMEM((EPG, LANES), F32)],
        compiler_params=pltpu.CompilerParams(
            dimension_semantics=("arbitrary",), vmem_limit_bytes=VMEM_LIMIT),
        name="mix",
    )(x, o, mb, wo, ga, gf, wr, br)


def _sorted_tiles(n):
    return n // _tile_rows(n) + N_GROUPS


def _group_tiles(cnt_ref, et):
    nt = [(cnt_ref[g] + (et - 1)) >> (et.bit_length() - 1) for g in range(N_GROUPS)]
    s1 = nt[0]
    s2 = s1 + nt[1]
    s3 = s2 + nt[2]
    return s1, s2, s3, s3 + nt[3]


def _scatter_kernel(cnt_ref, rows_ref, slot_ref, xs_ref, zeros, sem, zsem):
    tm = rows_ref.shape[0]
    et = zeros.shape[0]

    @pl.when(pl.program_id(0) == 0)
    def _():
        zeros[...] = jnp.zeros_like(zeros)
        s1, s2, s3, tot = _group_tiles(cnt_ref, et)
        ends = (s1, s2, s3, tot)

        def fill(tile):
            return pltpu.make_async_copy(zeros, xs_ref.at[pl.ds(tile * et, et), :], zsem)

        for g in range(N_GROUPS):
            partial = (cnt_ref[g] & (et - 1)) != 0

            @pl.when(partial)
            def _():
                fill(ends[g] - 1).start()
                fill(ends[g] - 1).wait()

        def fill_tail(tile, c):
            fill(tile).start()
            fill(tile).wait()
            return c

        lax.fori_loop(tot, xs_ref.shape[0] // et, fill_tail, 0)

    copy_of_row = lambda r: pltpu.make_async_copy(
        rows_ref.at[pl.ds(r, 1), :], xs_ref.at[pl.ds(slot_ref[0, 0, r], 1), :], sem)
    for r in range(tm):
        copy_of_row(r).start(priority=r % 2)
    for r in range(tm):
        copy_of_row(r).wait()


def _scatter(cnt, rows, slots, tm):
    n = rows.shape[0]
    et = _tile_rows(n)
    return pl.pallas_call(
        _scatter_kernel,
        grid_spec=pltpu.PrefetchScalarGridSpec(
            num_scalar_prefetch=1,
            grid=(n // tm,),
            in_specs=[
                pl.BlockSpec((tm, ROW_W), lambda i, c: (i, 0)),
                pl.BlockSpec((1, 1, tm), lambda i, c: (i, 0, 0), memory_space=pltpu.SMEM),
            ],
            out_specs=pl.BlockSpec(memory_space=pl.ANY),
            scratch_shapes=[pltpu.VMEM((et, ROW_W), F32),
                            pltpu.SemaphoreType.DMA(()), pltpu.SemaphoreType.DMA(())],
        ),
        out_shape=jax.ShapeDtypeStruct((_sorted_tiles(n) * et, ROW_W), F32),
        compiler_params=pltpu.CompilerParams(
            dimension_semantics=("arbitrary",), vmem_limit_bytes=VMEM_LIMIT),
        name="scatter",
    )(cnt, rows, slots)


def _tile_of(t, cnt_ref, et):
    s1, s2, s3, tot = _group_tiles(cnt_ref, et)
    tc = jnp.minimum(t, tot - 1)
    g = (tc >= s1).astype(jnp.int32) + (tc >= s2).astype(jnp.int32) + (tc >= s3).astype(jnp.int32)
    first = jnp.where(g == 0, 0, jnp.where(g == 1, s1, jnp.where(g == 2, s2, s3)))
    cnt = jnp.where(g == 0, cnt_ref[0], jnp.where(g == 1, cnt_ref[1], jnp.where(g == 2, cnt_ref[2], cnt_ref[3])))
    return tc, g, jnp.where(t < tot, cnt - (tc - first) * et, 0)


def _experts_kernel(cnt_ref, xs_ref, wgu_ref, wdn_ref, gf_ref, gn_ref, y_hbm, yb0, yb1, idv, ids, pend, sem, isem):
    t = pl.program_id(0)
    et = xs_ref.shape[0]
    _, _, cur_valid = _tile_of(t, cnt_ref, et)
    prev_valid = jnp.where(t > 0, jnp.clip(_tile_of(t - 1, cnt_ref, et)[2], 0, et), 0)
    fast = (prev_valid == et) & (cur_valid > 0)

    @pl.when(t == 0)
    def _():
        pend[0] = 0

    def drain(k):
        one = pltpu.make_async_copy(yb0.at[pl.ds(0, 1), :], y_hbm.at[pl.ds(0, 1), :], sem.at[k])

        @pl.when(pend[0] == et)
        def _():
            for _ in range(et):
                one.wait()

        @pl.when(pend[0] != et)
        def _():
            def body(r, c):
                one.wait()
                return c
            lax.fori_loop(0, pend[0], body, 0)

    def compute(ybuf, par):
        idv[par:par + 1, :] = xs_ref[:, D_MODEL:].T[META_ID:META_ID + 1, :].astype(jnp.int32)
        pltpu.make_async_copy(idv.at[par], ids.at[par], isem).start()
        tot = xs_ref[:, 0:D_MODEL]
        xb = _rms(tot, gf_ref[...]).astype(BF16)
        wts = xs_ref[:, D_MODEL:]
        for e in range(EPG):
            gu = _dot(xb, wgu_ref[0, e])
            gate = gu[:, :D_EXPERT]
            hmid = gate * jax.nn.sigmoid(gate) * gu[:, D_EXPERT:] * wts[:, e:e + 1]
            tot = tot + _dot(hmid.astype(BF16), wdn_ref[0, e])
        y = _rms(tot, gn_ref[...])
        drain(1 - par)
        ybuf[...] = y

    def ids_wait(par):
        pltpu.make_async_copy(idv.at[par], ids.at[par], isem).wait()

    def row_copy(ybuf, par, r):
        return pltpu.make_async_copy(ybuf.at[pl.ds(r, 1), :], y_hbm.at[pl.ds(ids[1 - par, r], 1), :], sem.at[par])

    for par, (ycur, yprev) in enumerate(((yb0, yb1), (yb1, yb0))):
        @pl.when(((t % 2) == par) & fast)
        def _():
            ids_wait(1 - par)
            for r in range(et):
                row_copy(yprev, par, r).start(priority=r % 2)
            compute(ycur, par)

        @pl.when(((t % 2) == par) & jnp.logical_not(fast))
        def _():
            @pl.when(prev_valid > 0)
            def _():
                ids_wait(1 - par)

                def start(r, c):
                    row_copy(yprev, par, r).start()
                    return c
                lax.fori_loop(0, prev_valid, start, 0)

            @pl.when(cur_valid > 0)
            def _():
                compute(ycur, par)

            @pl.when(cur_valid <= 0)
            def _():
                drain(1 - par)

        @pl.when((t % 2) == par)
        def _():
            pend[0] = prev_valid

            @pl.when(t == pl.num_programs(0) - 1)
            def _():
                drain(par)


def _experts(cnt, xs, wgu, wdn, gf, gn, n):
    et = _tile_rows(n)
    tiles = xs.shape[0] // et
    grp = lambda t, c: (_tile_of(t, c, et)[1], 0, 0, 0)
    return pl.pallas_call(
        _experts_kernel,
        grid_spec=pltpu.PrefetchScalarGridSpec(
            num_scalar_prefetch=1,
            grid=(tiles + 1,),
            in_specs=[
                pl.BlockSpec((et, ROW_W), lambda t, c: (_tile_of(t, c, et)[0], 0)),
                pl.BlockSpec((1, EPG, D_MODEL, 2 * D_EXPERT), grp),
                pl.BlockSpec((1, EPG, D_EXPERT, D_MODEL), grp),
                pl.BlockSpec((1, D_MODEL), lambda t, c: (0, 0)),
                pl.BlockSpec((1, D_MODEL), lambda t, c: (0, 0)),
            ],
            out_specs=pl.BlockSpec(memory_space=pl.ANY),
            scratch_shapes=[pltpu.VMEM((et, D_MODEL), F32), pltpu.VMEM((et, D_MODEL), F32),
                            pltpu.VMEM((8, et), jnp.int32), pltpu.SMEM((2, et), jnp.int32),
                            pltpu.SMEM((1,), jnp.int32), pltpu.SemaphoreType.DMA((2,)), pltpu.SemaphoreType.DMA(())],
        ),
        out_shape=jax.ShapeDtypeStruct((n, D_MODEL), F32),
        compiler_params=pltpu.CompilerParams(
            dimension_semantics=("arbitrary",), vmem_limit_bytes=VMEM_LIMIT),
        name="experts",
    )(cnt, xs, wgu, wdn, gf, gn)


def _forget_lanes(f):
    z = lambda n: jnp.zeros(f.shape[:-1] + (n,), f.dtype)
    return jnp.concatenate([f, z(F_HI - HEADS), f, f, f, z(LANES - F_LO - HEADS)], axis=-1)


def _step_shapes(b, s):
    return (512, 2, 512, 2) if s % 512 == 0 else (s, b, b * s, 1)


def _trunk(x, attend, state, w):
    b, s, _ = x.shape
    tm_proj, nb_proj, tm_tok, nsub_tok = _step_shapes(b, s)
    k, v, lf, qa, ka, mb, ncv, *built = _proj(
        x, state, w["gm"], w["w_proj"], w["bias_f"], w["cw"], w["gc"], tm_proj, nb_proj)
    if built:
        w["w_proj"] = built[0]
    o = attend(qa, ka, v)
    n = b * s
    rows, slots, cnt = _mix(x.reshape(n, D_MODEL), o.reshape(n, ATTN_W), mb.reshape(n, CONV_W),
                            w["wo"], w["ga"], w["gf"], w["wr"], w["br"], tm_tok, nsub_tok)
    cnt = cnt[:, 0]
    y = _experts(cnt, _scatter(cnt, rows, slots, tm_tok), w["wgu"], w["wdn"], w["gf"], w["gn"], n)
    return (y.reshape(b, s, D_MODEL), k.reshape(1, b, s, HEADS, HEAD_DIM), v.reshape(1, b, s, HEADS, HEAD_DIM),
            lf.transpose(0, 2, 1).reshape(1, b, s, HEADS), ncv.reshape(1, b, 2, CONV_W))


def kernel(x_prompt, x_sample, cache_k, cache_v, cache_logf, state_conv, norm_mix_g, w_in, b_forget, conv_w, norm_attn_g, norm_conv_g, w_out, norm_ffn_g, w_router_group, b_router_group, w_router_expert, b_router_expert, w_expert_gate_up, w_expert_down, norm_final_g):
    assert w_in.shape[0] == 1, "single-layer trunk"
    wr = jnp.concatenate(
        [w_router_group[0], jnp.zeros((D_MODEL, EPG - N_GROUPS), F32),
         w_router_expert[0].transpose(1, 0, 2).reshape(D_MODEL, N_GROUPS * EPG),
         jnp.zeros((D_MODEL, LANES - EPG - N_GROUPS * EPG), F32)], axis=1).astype(BF16)
    br = jnp.concatenate(
        [b_router_group[0], jnp.zeros((EPG - N_GROUPS,), F32), b_router_expert[0].reshape(-1),
         jnp.zeros((LANES - EPG - N_GROUPS * EPG,), F32)]).reshape(1, LANES)
    w = dict(
        gm=norm_mix_g[0].reshape(1, D_MODEL), w_proj=w_in[0].T, bias_f=_forget_lanes(b_forget[0]).reshape(1, LANES),
        cw=jnp.concatenate([conv_w[0], jnp.zeros((8 - conv_w.shape[1], CONV_W), F32)], axis=0),
        gc=norm_conv_g[0].reshape(1, CONV_W), wo=w_out[0].astype(BF16), ga=norm_attn_g[0].reshape(1, ATTN_W),
        gf=norm_ffn_g[0].reshape(1, D_MODEL), wr=wr, br=br,
        wgu=w_expert_gate_up[0], wdn=w_expert_down[0],
        gn=norm_final_g.reshape(1, D_MODEL))

    bp = x_prompt.shape[0]
    bs, ss, _ = x_sample.shape
    plen = cache_k.shape[2]

    def attend_p(qa, ka, v):
        wgu, wdn = w["wgu"], w["wdn"]
        o, wgu_bf, wdn_bf = _attention(qa, ka, v, wgu.reshape((-1,) + wgu.shape[2:]), wdn.reshape((-1,) + wdn.shape[2:]))
        w["wgu"], w["wdn"] = wgu_bf.reshape(wgu.shape), wdn_bf.reshape(wdn.shape)
        return o

    yp, kp, vp, lfp, cvp = _trunk(x_prompt, attend_p, jnp.zeros((bp, 8, CONV_W), F32), w)

    ckt = cache_k[0].transpose(0, 2, 3, 1).reshape(bs, ATTN_W, plen)
    cvt = cache_v[0].transpose(0, 2, 3, 1).reshape(bs, ATTN_W, plen)
    clft = cache_logf[0].transpose(0, 2, 1)
    st = jnp.concatenate([jnp.zeros((bs, 6, CONV_W), F32), state_conv[0]], axis=1)
    attend_s = lambda qa, ka, v: _attention_cache(qa, ka, v, ckt, cvt, clft, 2)
    ys, ks, vs, lfs, cvs = _trunk(x_sample, attend_s, st, w)
    return (yp, ys, kp, vp, lfp, cvp, ks, vs, lfs, cvs)
```
